```python
import jax, jax.numpy as jnp
from jax import lax
import numpy as np

D_MODEL = 2048
BATCH = 1
SEQ = 16384
DEPTH = 1
DEC_BATCH = 2
DEC_SEQ = 16384
PAST_LEN = 128

HEAD_DIM = 128
DIL_GROUPS = ((128, 1), (512, 4), (2048, 16))
N_DIL_GROUPS = 3
DIL_HEADS = 4
DIL_QKV_WIDTH = N_DIL_GROUPS * DIL_HEADS * HEAD_DIM
DIL_OUT_WIDTH = DIL_HEADS * HEAD_DIM
BAND_BLOCK = 64
GQA_Q_HEADS = 12
GQA_KV_HEADS = 4
GQA_GROUP = GQA_Q_HEADS // GQA_KV_HEADS
GQA_Q_WIDTH = GQA_Q_HEADS * HEAD_DIM
GQA_KV_WIDTH = GQA_KV_HEADS * HEAD_DIM
GQA_OUT_WIDTH = GQA_Q_WIDTH
Q_BLOCK = 128
GRID_W = 64
ROPE_THETA = 10000.0
D_FF = 4 * D_MODEL
IN_SPLITS = (DIL_QKV_WIDTH, DIL_QKV_WIDTH, DIL_QKV_WIDTH,
             GQA_Q_WIDTH, GQA_KV_WIDTH, GQA_KV_WIDTH, D_MODEL, D_MODEL)
IN_WIDTH = sum(IN_SPLITS)
BRANCH_WIDTH = DIL_OUT_WIDTH + GQA_OUT_WIDTH
RMS_EPS = 1e-6
NEG_INF = -1e30

kernel_name = "hybrid_dilated_gqa_encoder"


def rms_norm(x, g):
    xf = x.astype(jnp.float32)
    y = xf * lax.rsqrt(jnp.mean(xf * xf, axis=-1, keepdims=True) + RMS_EPS)
    return (y * g.astype(jnp.float32)).astype(x.dtype)


def alibi_slopes():
    n = N_DIL_GROUPS * DIL_HEADS
    return 2.0 ** (-8.0 * jnp.arange(1, n + 1, dtype=jnp.float32) / n)


def dilated_band_attention(q, k, v, window, dilation, slopes):
    B, S, H, Dh = q.shape
    n_side = window // (2 * dilation)
    L = S // dilation
    nb = -(-L // BAND_BLOCK)
    Lp = nb * BAND_BLOCK

    def to_sub(a):
        return a.reshape(B, L, dilation, H, Dh).transpose(0, 2, 1, 3, 4)

    qs = jnp.pad(to_sub(q), ((0, 0), (0, 0), (0, Lp - L), (0, 0), (0, 0)))
    qs = qs.reshape(B, dilation, nb, BAND_BLOCK, H, Dh)

    def key_blocks(a):
        ap = jnp.pad(to_sub(a), ((0, 0), (0, 0), (BAND_BLOCK, Lp - L + BAND_BLOCK), (0, 0), (0, 0)))
        ap = ap.reshape(B, dilation, nb + 2, BAND_BLOCK, H, Dh)
        return jnp.concatenate([ap[:, :, :-2], ap[:, :, 1:-1], ap[:, :, 2:]], axis=3)

    kb = key_blocks(k)
    vb = key_blocks(v)
    s = jnp.einsum('brnqhd,brnkhd->brnhqk', qs, kb,
                   preferred_element_type=jnp.float32) * (HEAD_DIM ** -0.5)
    rel = (jnp.arange(3 * BAND_BLOCK)[None, :] - BAND_BLOCK
           - jnp.arange(BAND_BLOCK)[:, None])
    key_idx = (jnp.arange(nb)[:, None] * BAND_BLOCK - BAND_BLOCK
               + jnp.arange(3 * BAND_BLOCK)[None, :])
    mask = (jnp.abs(rel) <= n_side)[None] & ((key_idx >= 0) & (key_idx < L))[:, None, :]
    dist = (dilation * jnp.abs(rel)).astype(jnp.float32)
    s = s - slopes[:, None, None] * dist
    s = jnp.where(mask[:, None], s, NEG_INF)
    lse = jax.nn.logsumexp(s, axis=-1)
    p = jnp.exp(s - lse[..., None]).astype(v.dtype)
    o = jnp.einsum('brnhqk,brnkhd->brnqhd', p, vb)
    o = o.reshape(B, dilation, Lp, H, Dh)[:, :, :L].transpose(0, 2, 1, 3, 4).reshape(B, S, H, Dh)
    lse = lse.transpose(0, 1, 2, 4, 3).reshape(B, dilation, Lp, H)[:, :, :L]
    lse = lse.transpose(0, 2, 1, 3).reshape(B, S, H)
    return o, lse


def dilated_mixer(q, k, v):
    B, S, _ = q.shape
    shp = (B, S, N_DIL_GROUPS, DIL_HEADS, HEAD_DIM)
    q, k, v = q.reshape(shp), k.reshape(shp), v.reshape(shp)
    slopes = alibi_slopes().reshape(N_DIL_GROUPS, DIL_HEADS)
    outs, lses = [], []
    for g, (w, d) in enumerate(DIL_GROUPS):
        o, l = dilated_band_attention(q[:, :, g], k[:, :, g], v[:, :, g], w, d, slopes[g])
        outs.append(o)
        lses.append(l)
    wts = jax.nn.softmax(jnp.stack(lses, axis=0), axis=0)
    y = sum(wts[g][..., None] * outs[g].astype(jnp.float32) for g in range(N_DIL_GROUPS))
    return y.reshape(B, S, DIL_OUT_WIDTH).astype(q.dtype)


def axial_rope_tables(S):
    rows = S // GRID_W
    row = jnp.broadcast_to(jnp.arange(rows)[:, None], (rows, GRID_W)).reshape(S).astype(jnp.float32)
    col = jnp.broadcast_to(jnp.arange(GRID_W)[None, :], (rows, GRID_W)).reshape(S).astype(jnp.float32)
    half = HEAD_DIM // 2
    inv_freq = ROPE_THETA ** (-jnp.arange(0, half, 2, dtype=jnp.float32) / half)
    ang = jnp.stack([row[:, None] * inv_freq, col[:, None] * inv_freq], axis=1)
    return jnp.cos(ang), jnp.sin(ang)


def apply_axial_rope(x, cos, sin):
    B, S, H, _ = x.shape
    xf = x.astype(jnp.float32).reshape(B, S, H, 2, 2, HEAD_DIM // 4)
    x1, x2 = xf[..., 0, :], xf[..., 1, :]
    c = cos[None, :, None]
    s = sin[None, :, None]
    out = jnp.stack([x1 * c - x2 * s, x2 * c + x1 * s], axis=-2)
    return out.reshape(B, S, H, HEAD_DIM).astype(x.dtype)


def gqa_mixer(q, k, v, g_q, g_k):
    B, S, _ = q.shape
    q = rms_norm(q.reshape(B, S, GQA_Q_HEADS, HEAD_DIM), g_q)
    k = rms_norm(k.reshape(B, S, GQA_KV_HEADS, HEAD_DIM), g_k)
    v = v.reshape(B, S, GQA_KV_HEADS, HEAD_DIM)
    cos, sin = axial_rope_tables(S)
    q = apply_axial_rope(q, cos, sin)
    k = apply_axial_rope(k, cos, sin)
    nq = S // Q_BLOCK
    qb = q.reshape(B, nq, Q_BLOCK, GQA_KV_HEADS, GQA_GROUP, HEAD_DIM).transpose(1, 0, 2, 3, 4, 5)
    scale = HEAD_DIM ** -0.5

    def attend(qblk):
        s = jnp.einsum('bqhgd,bkhd->bhgqk', qblk, k, preferred_element_type=jnp.float32) * scale
        p = jax.nn.softmax(s, axis=-1).astype(v.dtype)
        return jnp.einsum('bhgqk,bkhd->bqhgd', p, v)

    o = lax.map(attend, qb)
    return o.transpose(1, 0, 2, 3, 4, 5).reshape(B, S, GQA_OUT_WIDTH)


def encoder_layer(x, g_mix, w_in, g_q, g_k, w_branch, w_out, g_mlp, w_ff1, w_ff2):
    h = rms_norm(x, g_mix)
    proj = h @ w_in
    idx = np.cumsum(IN_SPLITS)[:-1].tolist()
    aq, ak, av, bq, bk, bv, ga, gb = jnp.split(proj, idx, axis=-1)
    y_a = dilated_mixer(aq, ak, av)
    y_b = gqa_mixer(bq, bk, bv, g_q, g_k)
    o_a = y_a @ w_branch[:DIL_OUT_WIDTH]
    o_b = y_b @ w_branch[DIL_OUT_WIDTH:]
    merged = jax.nn.sigmoid(ga) * o_a + jax.nn.sigmoid(gb) * o_b
    x = x + merged @ w_out
    h2 = rms_norm(x, g_mlp)
    u = jax.nn.relu(h2 @ w_ff1)
    return x + (u * u) @ w_ff2


def trunk(x, g_mix, w_in, g_q, g_k, w_branch, w_out, g_mlp, w_ff1, w_ff2, g_final):
    for l in range(DEPTH):
        x = encoder_layer(x, g_mix[l], w_in[l], g_q[l], g_k[l], w_branch[l], w_out[l],
                          g_mlp[l], w_ff1[l], w_ff2[l])
    return rms_norm(x, g_final)


def setup_inputs(seed: int = 0) -> dict:
    key = jax.random.key(seed)
    ks = jax.random.split(key, 13)
    f32 = jnp.float32
    nrm = lambda k, shp: jax.random.normal(k, shp, f32)
    branch_scale = jnp.concatenate([jnp.full((DIL_OUT_WIDTH,), DIL_OUT_WIDTH ** -0.5, f32),
                                    jnp.full((GQA_OUT_WIDTH,), GQA_OUT_WIDTH ** -0.5, f32)])[:, None]
    return {
        "x_prompt": nrm(ks[0], (BATCH, SEQ, D_MODEL)),
        "x_sample": nrm(ks[1], (DEC_BATCH, DEC_SEQ, D_MODEL)),
        "g_mix": 1.0 + 0.02 * nrm(ks[2], (DEPTH, D_MODEL)),
        "w_in": nrm(ks[3], (DEPTH, D_MODEL, IN_WIDTH)) * D_MODEL ** -0.5,
        "g_q": 1.0 + 0.02 * nrm(ks[4], (DEPTH, HEAD_DIM)),
        "g_k": 1.0 + 0.02 * nrm(ks[5], (DEPTH, HEAD_DIM)),
        "w_branch": nrm(ks[6], (DEPTH, BRANCH_WIDTH, D_MODEL)) * branch_scale,
        "w_out": nrm(ks[7], (DEPTH, D_MODEL, D_MODEL)) * D_MODEL ** -0.5,
        "g_mlp": 1.0 + 0.02 * nrm(ks[8], (DEPTH, D_MODEL)),
        "w_ff1": nrm(ks[9], (DEPTH, D_MODEL, D_FF)) * D_MODEL ** -0.5,
        "w_ff2": nrm(ks[10], (DEPTH, D_FF, D_MODEL)) * D_FF ** -0.5,
        "g_final": 1.0 + 0.02 * nrm(ks[11], (D_MODEL,)),
    }


def reference(x_prompt, x_sample, g_mix, w_in, g_q, g_k, w_branch, w_out, g_mlp, w_ff1, w_ff2, g_final):
    y_prompt = trunk(x_prompt, g_mix, w_in, g_q, g_k, w_branch, w_out, g_mlp, w_ff1, w_ff2, g_final)
    y_sample = trunk(x_sample, g_mix, w_in, g_q, g_k, w_branch, w_out, g_mlp, w_ff1, w_ff2, g_final)
    return (y_prompt, y_sample)
```

```python
import functools

import jax
import jax.numpy as jnp
import numpy as np
from jax import lax
from jax.experimental import pallas as pl
from jax.experimental.pallas import tpu as pltpu

F32 = jnp.float32
BF16 = jnp.bfloat16

D_MODEL = 2048
HEAD_DIM = 128
DIL_GROUPS = ((128, 1), (512, 4), (2048, 16))
N_DIL_GROUPS = 3
DIL_HEADS = 4
DIL_WIDTH = DIL_HEADS * HEAD_DIM
GQA_Q_HEADS = 12
GQA_KV_HEADS = 4
GQA_GROUP = GQA_Q_HEADS // GQA_KV_HEADS
GQA_Q_WIDTH = GQA_Q_HEADS * HEAD_DIM
GQA_KV_WIDTH = GQA_KV_HEADS * HEAD_DIM
GRID_W = 64
ROPE_THETA = 10000.0
D_FF = 4 * D_MODEL
IN_WIDTH = 3 * N_DIL_GROUPS * DIL_WIDTH + GQA_Q_WIDTH + 2 * GQA_KV_WIDTH + 2 * D_MODEL
COL_AQ = 0
COL_AK = N_DIL_GROUPS * DIL_WIDTH
COL_AV = 2 * N_DIL_GROUPS * DIL_WIDTH
COL_BQ = 3 * N_DIL_GROUPS * DIL_WIDTH
COL_BK = COL_BQ + GQA_Q_WIDTH
COL_BV = COL_BK + GQA_KV_WIDTH
COL_GA = COL_BV + GQA_KV_WIDTH
COL_GB = COL_GA + D_MODEL
RMS_EPS = 1e-6
NEG_INF = -1e30
ATTN_SCALE = HEAD_DIM ** -0.5

V7X_VMEM_BYTES = 64 * 1024 * 1024
MIB = 1024 * 1024

IN_TM, IN_TN = 1024, 1024
DIL_TQ = 128
DIL_HALO = 64
PREP_TM = 512
FLASH_TQ, FLASH_TK = 512, 512
MERGE_TM = 256
MLP_TM, MLP_TF = 512, 512


def _vmem_limit(nbytes):
    return int(min(nbytes + 16 * MIB, V7X_VMEM_BYTES - 8 * MIB))


def _rms(x, g):
    return x * lax.rsqrt(jnp.mean(x * x, axis=-1, keepdims=True) + RMS_EPS) * g


def _in_proj_kernel(x_ref, g_ref, w_ref, o_ref, h_ref):
    @pl.when(pl.program_id(1) == 0)
    def _():
        h_ref[...] = _rms(x_ref[...], g_ref[...]).astype(BF16)

    o_ref[...] = jnp.dot(h_ref[...], w_ref[...], preferred_element_type=F32).astype(o_ref.dtype)


def _in_proj(x, g_mix, w_in):
    rows = x.shape[0]
    tm, tn = IN_TM, IN_TN
    est = 2 * tm * D_MODEL * 4 + tm * D_MODEL * 2 + 2 * D_MODEL * tn * 2 + 2 * tm * tn * 2
    return pl.pallas_call(
        _in_proj_kernel,
        grid=(rows // tm, IN_WIDTH // tn),
        in_specs=[
            pl.BlockSpec((tm, D_MODEL), lambda i, j: (i, 0)),
            pl.BlockSpec((1, D_MODEL), lambda i, j: (0, 0)),
            pl.BlockSpec((D_MODEL, tn), lambda i, j: (0, j)),
        ],
        out_specs=pl.BlockSpec((tm, tn), lambda i, j: (i, j)),
        out_shape=jax.ShapeDtypeStruct((rows, IN_WIDTH), BF16),
        scratch_shapes=[pltpu.VMEM((tm, D_MODEL), BF16)],
        compiler_params=pltpu.CompilerParams(
            dimension_semantics=("parallel", "arbitrary"), vmem_limit_bytes=_vmem_limit(est)),
        name="in_proj",
    )(x, g_mix, w_in)


def _dilated_kernel(*refs, dilation, sub_len, slopes, has_state, emit_final):
    q_ref, kp_ref, kc_ref, kn_ref, vp_ref, vc_ref, vn_ref = refs[:7]
    refs = refs[7:]
    if has_state:
        acc_in, ml_in = refs[:2]
        refs = refs[2:]
    if emit_final:
        (y_ref,) = refs
    else:
        acc_out, ml_out = refs

    tq, tk = DIL_TQ, DIL_TQ + 2 * DIL_HALO
    tile = pl.program_id(2)
    row = lax.broadcasted_iota(jnp.int32, (tq, tk), 0)
    col = lax.broadcasted_iota(jnp.int32, (tq, tk), 1)
    rel = col - DIL_HALO - row
    key_idx = tile * tq - DIL_HALO + col
    mask = (jnp.abs(rel) <= DIL_HALO) & (key_idx >= 0) & (key_idx < sub_len)
    dist = (dilation * jnp.abs(rel)).astype(F32)
    lane = lax.broadcasted_iota(jnp.int32, (tq, HEAD_DIM), 1)
    ml_new = jnp.zeros((tq, HEAD_DIM), F32)

    for h in range(DIL_HEADS):
        hs = slice(h * HEAD_DIM, (h + 1) * HEAD_DIM)
        k_win = jnp.concatenate([kp_ref[:, hs], kc_ref[:, hs], kn_ref[:, hs]], axis=0)
        v_win = jnp.concatenate([vp_ref[:, hs], vc_ref[:, hs], vn_ref[:, hs]], axis=0)
        s = lax.dot_general(q_ref[:, hs], k_win, (((1,), (1,)), ((), ())),
                            preferred_element_type=F32) * ATTN_SCALE
        s = jnp.where(mask, s - slopes[h] * dist, NEG_INF)
        m_cur = jnp.max(s, axis=-1, keepdims=True)
        if has_state:
            m_prev = ml_in[:, h:h + 1]
            l_prev = ml_in[:, DIL_HEADS + h:DIL_HEADS + h + 1]
            m_new = jnp.maximum(m_prev, m_cur)
            alpha = jnp.exp(m_prev - m_new)
        else:
            m_new = m_cur
        p = jnp.exp(s - m_new)
        l_new = jnp.sum(p, axis=-1, keepdims=True)
        acc = jnp.dot(p.astype(BF16), v_win, preferred_element_type=F32)
        if has_state:
            l_new = alpha * l_prev + l_new
            acc = alpha * acc_in[:, hs] + acc
        if emit_final:
            y_ref[:, hs] = (acc / l_new).astype(y_ref.dtype)
        else:
            acc_out[:, hs] = acc
            ml_new = jnp.where(lane == h, m_new, ml_new)
            ml_new = jnp.where(lane == DIL_HEADS + h, l_new, ml_new)
    if not emit_final:
        ml_out[...] = ml_new


def _dilated_call(proj, group, state, emit_final, n_seq, seq_len):
    _, dilation = DIL_GROUPS[group]
    sub_len = seq_len // dilation
    nt = sub_len // DIL_TQ
    sub_rows = n_seq * sub_len
    col_blocks = IN_WIDTH // DIL_WIDTH
    n_heads = N_DIL_GROUPS * DIL_HEADS
    slopes = tuple(float(2.0 ** (-8.0 * (group * DIL_HEADS + h + 1) / n_heads)) for h in range(DIL_HEADS))
    proj_v = proj.reshape(sub_rows, dilation * IN_WIDTH)

    def col_of(base):
        return lambda b, r, i: r * col_blocks + (base // DIL_WIDTH + group)

    def cur_spec(base):
        c = col_of(base)
        return pl.BlockSpec((DIL_TQ, DIL_WIDTH), lambda b, r, i: (b * nt + i, c(b, r, i)))

    def prev_spec(base):
        c = col_of(base)
        return pl.BlockSpec((DIL_HALO, DIL_WIDTH),
                            lambda b, r, i: (b * 2 * nt + jnp.maximum(2 * i - 1, 0), c(b, r, i)))

    def next_spec(base):
        c = col_of(base)
        return pl.BlockSpec((DIL_HALO, DIL_WIDTH),
                            lambda b, r, i: (b * 2 * nt + jnp.minimum(2 * i + 2, 2 * nt - 1), c(b, r, i)))

    acc_spec = pl.BlockSpec((DIL_TQ, DIL_WIDTH), lambda b, r, i: (b * nt + i, r))
    ml_spec = pl.BlockSpec((DIL_TQ, HEAD_DIM), lambda b, r, i: (b * nt + i, r))

    in_specs = [cur_spec(COL_AQ), prev_spec(COL_AK), cur_spec(COL_AK), next_spec(COL_AK),
                prev_spec(COL_AV), cur_spec(COL_AV), next_spec(COL_AV)]
    args = [proj_v] * 7
    has_state = state is not None
    if has_state:
        acc, ml = state
        in_specs += [acc_spec, ml_spec]
        args += [acc.reshape(sub_rows, dilation * DIL_WIDTH), ml.reshape(sub_rows, dilation * HEAD_DIM)]
    if emit_final:
        out_specs = acc_spec
        out_shape = jax.ShapeDtypeStruct((sub_rows, dilation * DIL_WIDTH), BF16)
    else:
        out_specs = [acc_spec, ml_spec]
        out_shape = [jax.ShapeDtypeStruct((sub_rows, dilation * DIL_WIDTH), F32),
                     jax.ShapeDtypeStruct((sub_rows, dilation * HEAD_DIM), F32)]
    kern = functools.partial(_dilated_kernel, dilation=dilation, sub_len=sub_len, slopes=slopes,
                             has_state=has_state, emit_final=emit_final)
    out = pl.pallas_call(
        kern,
        grid=(n_seq, dilation, nt),
        in_specs=in_specs,
        out_specs=out_specs,
        out_shape=out_shape,
        compiler_params=pltpu.CompilerParams(
            dimension_semantics=("parallel", "parallel", "arbitrary"), vmem_limit_bytes=_vmem_limit(8 * MIB)),
        name=f"dilated_g{group}",
    )(*args)
    rows = n_seq * seq_len
    if emit_final:
        return out.reshape(rows, DIL_WIDTH)
    return out[0].reshape(rows, DIL_WIDTH), out[1].reshape(rows, HEAD_DIM)


def _dilated_mixer(proj, n_seq, seq_len):
    state = _dilated_call(proj, 2, None, False, n_seq, seq_len)
    state = _dilated_call(proj, 1, state, False, n_seq, seq_len)
    return _dilated_call(proj, 0, state, True, n_seq, seq_len)


def _rope_tables(seq_len):
    pos = jnp.arange(seq_len)
    row = (pos // GRID_W).astype(F32)
    col = (pos % GRID_W).astype(F32)
    half = HEAD_DIM // 2
    inv_freq = ROPE_THETA ** (-jnp.arange(0, half, 2, dtype=F32) / half)
    ang_r = row[:, None] * inv_freq
    ang_c = col[:, None] * inv_freq
    cos = jnp.concatenate([jnp.cos(ang_r), jnp.cos(ang_r), jnp.cos(ang_c), jnp.cos(ang_c)], axis=1)
    sin = jnp.concatenate([-jnp.sin(ang_r), jnp.sin(ang_r), -jnp.sin(ang_c), jnp.sin(ang_c)], axis=1)
    return cos, sin


def _prep_kernel(q_ref, k_ref, cos_ref, sin_ref, gq_ref, gk_ref, qo_ref, ko_ref):
    cos = cos_ref[...]
    sin = sin_ref[...]
    quarter = HEAD_DIM // 4
    lane = lax.broadcasted_iota(jnp.int32, cos.shape, 1)
    upper = (lane & quarter) != 0

    def norm_rope(x, g, scale):
        y = _rms(x.astype(F32), g)
        partner = jnp.where(upper, pltpu.roll(y, quarter, 1), pltpu.roll(y, HEAD_DIM - quarter, 1))
        return (y * cos + partner * sin) * scale

    for h in range(GQA_Q_HEADS):
        hs = slice(h * HEAD_DIM, (h + 1) * HEAD_DIM)
        qo_ref[:, hs] = norm_rope(q_ref[:, hs], gq_ref[...], ATTN_SCALE).astype(qo_ref.dtype)
    for h in range(GQA_KV_HEADS):
        hs = slice(h * HEAD_DIM, (h + 1) * HEAD_DIM)
        ko_ref[:, hs] = norm_rope(k_ref[:, hs], gk_ref[...], 1.0).astype(ko_ref.dtype)


def _qk_prep(proj, cos, sin, g_q, g_k, seq_len):
    rows = proj.shape[0]
    tm = PREP_TM
    tpos = seq_len // tm
    return pl.pallas_call(
        _prep_kernel,
        grid=(rows // tm,),
        in_specs=[
            pl.BlockSpec((tm, GQA_Q_WIDTH), lambda i: (i, COL_BQ // GQA_Q_WIDTH)),
            pl.BlockSpec((tm, GQA_KV_WIDTH), lambda i: (i, COL_BK // GQA_KV_WIDTH)),
            pl.BlockSpec((tm, HEAD_DIM), lambda i: (i % tpos, 0)),
            pl.BlockSpec((tm, HEAD_DIM), lambda i: (i % tpos, 0)),
            pl.BlockSpec((1, HEAD_DIM), lambda i: (0, 0)),
            pl.BlockSpec((1, HEAD_DIM), lambda i: (0, 0)),
        ],
        out_specs=[pl.BlockSpec((tm, GQA_Q_WIDTH), lambda i: (i, 0)),
                   pl.BlockSpec((tm, GQA_KV_WIDTH), lambda i: (i, 0))],
        out_shape=[jax.ShapeDtypeStruct((rows, GQA_Q_WIDTH), BF16),
                   jax.ShapeDtypeStruct((rows, GQA_KV_WIDTH), BF16)],
        compiler_params=pltpu.CompilerParams(
            dimension_semantics=("parallel",), vmem_limit_bytes=_vmem_limit(12 * MIB)),
        name="qk_prep",
    )(proj, proj, cos, sin, g_q, g_k)


def _flash_kernel(q_ref, k_ref, v_ref, o_ref, m_sc, l_sc, acc_sc, *, seq_len):
    tq, tk = FLASH_TQ, FLASH_TK
    m_sc[...] = jnp.full(m_sc.shape, NEG_INF, F32)
    l_sc[...] = jnp.zeros(l_sc.shape, F32)
    acc_sc[...] = jnp.zeros(acc_sc.shape, F32)

    def body(c, carry):
        start = pl.multiple_of(c * tk, tk)
        k = k_ref[pl.ds(start, tk), :]
        v = v_ref[pl.ds(start, tk), :]
        for g in range(GQA_GROUP):
            q = q_ref[:, g * HEAD_DIM:(g + 1) * HEAD_DIM]
            s = lax.dot_general(q, k, (((1,), (1,)), ((), ())), preferred_element_type=F32)
            m_prev = m_sc[g]
            m_new = jnp.maximum(m_prev, jnp.max(s, axis=-1, keepdims=True))
            alpha = jnp.exp(m_prev - m_new)
            p = jnp.exp(s - m_new)
            l_sc[g] = alpha * l_sc[g] + jnp.sum(p, axis=-1, keepdims=True)
            acc_sc[g] = alpha * acc_sc[g] + jnp.dot(p.astype(BF16), v, preferred_element_type=F32)
            m_sc[g] = m_new
        return carry

    lax.fori_loop(0, seq_len // tk, body, 0)
    for g in range(GQA_GROUP):
        o_ref[:, g * HEAD_DIM:(g + 1) * HEAD_DIM] = (acc_sc[g] / l_sc[g]).astype(o_ref.dtype)


def _flash(q, k, proj, n_seq, seq_len):
    rows = q.shape[0]
    tq = FLASH_TQ
    nq = seq_len // tq
    gw = GQA_GROUP * HEAD_DIM
    est = 2 * 2 * seq_len * HEAD_DIM * 2 + 4 * tq * gw * 2 + GQA_GROUP * tq * (HEAD_DIM + 256) * 4
    return pl.pallas_call(
        functools.partial(_flash_kernel, seq_len=seq_len),
        grid=(n_seq, GQA_KV_HEADS, nq),
        in_specs=[
            pl.BlockSpec((tq, gw), lambda b, h, i: (b * nq + i, h)),
            pl.BlockSpec((seq_len, HEAD_DIM), lambda b, h, i: (b, h)),
            pl.BlockSpec((seq_len, HEAD_DIM), lambda b, h, i: (b, COL_BV // HEAD_DIM + h)),
        ],
        out_specs=pl.BlockSpec((tq, gw), lambda b, h, i: (b * nq + i, h)),
        out_shape=jax.ShapeDtypeStruct((rows, GQA_Q_WIDTH), BF16),
        scratch_shapes=[pltpu.VMEM((GQA_GROUP, tq, 1), F32), pltpu.VMEM((GQA_GROUP, tq, 1), F32),
                        pltpu.VMEM((GQA_GROUP, tq, HEAD_DIM), F32)],
        compiler_params=pltpu.CompilerParams(
            dimension_semantics=("parallel", "parallel", "arbitrary"), vmem_limit_bytes=_vmem_limit(est)),
        name="gqa_flash",
    )(q, k, proj)


def _sigmoid(x):
    return 1.0 / (1.0 + jnp.exp(-x))


def _merge_kernel(ya_ref, yb_ref, ga0_ref, ga1_ref, gb0_ref, gb1_ref, x_ref, wa_ref, wb_ref, wo_ref, g_ref,
                  x1_ref, h2_ref):
    half = D_MODEL // 2
    ya = ya_ref[...]
    yb = yb_ref[...]
    x1 = x_ref[...]
    for n, (ga_ref, gb_ref) in enumerate(((ga0_ref, gb0_ref), (ga1_ref, gb1_ref))):
        cs = slice(n * half, (n + 1) * half)
        o_a = jnp.dot(ya, wa_ref[:, cs], preferred_element_type=F32)
        o_b = jnp.dot(yb, wb_ref[:, cs], preferred_element_type=F32)
        merged = _sigmoid(ga_ref[...].astype(F32)) * o_a + _sigmoid(gb_ref[...].astype(F32)) * o_b
        x1 = x1 + jnp.dot(merged.astype(BF16), wo_ref[cs, :], preferred_element_type=F32)
    x1_ref[...] = x1
    h2_ref[...] = _rms(x1, g_ref[...]).astype(h2_ref.dtype)


def _merge(y_a, y_b, proj, x, w_a, w_b, w_out, g_mlp):
    rows = x.shape[0]
    tm = MERGE_TM
    half = D_MODEL // 2
    once = pl.Buffered(1)
    est = (2 * D_MODEL * D_MODEL * 2 + 2 * tm * (DIL_WIDTH + GQA_Q_WIDTH) * 2 + 2 * 4 * tm * half * 2
           + 4 * tm * D_MODEL * 4 + 2 * tm * D_MODEL * 2 + 6 * tm * D_MODEL * 4)
    return pl.pallas_call(
        _merge_kernel,
        grid=(rows // tm,),
        in_specs=[
            pl.BlockSpec((tm, DIL_WIDTH), lambda i: (i, 0)),
            pl.BlockSpec((tm, GQA_Q_WIDTH), lambda i: (i, 0)),
            pl.BlockSpec((tm, half), lambda i: (i, COL_GA // half)),
            pl.BlockSpec((tm, half), lambda i: (i, COL_GA // half + 1)),
            pl.BlockSpec((tm, half), lambda i: (i, COL_GB // half)),
            pl.BlockSpec((tm, half), lambda i: (i, COL_GB // half + 1)),
            pl.BlockSpec((tm, D_MODEL), lambda i: (i, 0)),
            pl.BlockSpec((DIL_WIDTH, D_MODEL), lambda i: (0, 0), pipeline_mode=once),
            pl.BlockSpec((GQA_Q_WIDTH, D_MODEL), lambda i: (0, 0), pipeline_mode=once),
            pl.BlockSpec((D_MODEL, D_MODEL), lambda i: (0, 0), pipeline_mode=once),
            pl.BlockSpec((1, D_MODEL), lambda i: (0, 0)),
        ],
        out_specs=[pl.BlockSpec((tm, D_MODEL), lambda i: (i, 0)),
                   pl.BlockSpec((tm, D_MODEL), lambda i: (i, 0))],
        out_shape=[jax.ShapeDtypeStruct((rows, D_MODEL), F32),
                   jax.ShapeDtypeStruct((rows, D_MODEL), BF16)],
        compiler_params=pltpu.CompilerParams(
            dimension_semantics=("parallel",), vmem_limit_bytes=_vmem_limit(est)),
        name="merge_out_proj",
    )(y_a, y_b, proj, proj, proj, proj, x, w_a, w_b, w_out, g_mlp)


def _mlp_kernel(h2_ref, x1_ref, w1_ref, w2_ref, g_ref, o_ref):
    f = pl.program_id(1)

    @pl.when(f == 0)
    def _():
        o_ref[...] = x1_ref[...]

    u = jnp.maximum(jnp.dot(h2_ref[...], w1_ref[...], preferred_element_type=F32), 0.0)
    o_ref[...] += jnp.dot((u * u).astype(BF16), w2_ref[...], preferred_element_type=F32)

    @pl.when(f == pl.num_programs(1) - 1)
    def _():
        o_ref[...] = _rms(o_ref[...], g_ref[...])


def _mlp(h2, x1, w_ff1, w_ff2, g_final):
    rows = h2.shape[0]
    tm, tf = MLP_TM, MLP_TF
    est = 2 * tm * D_MODEL * 2 + 4 * tm * D_MODEL * 4 + 4 * D_MODEL * tf * 2 + 3 * tm * tf * 4
    return pl.pallas_call(
        _mlp_kernel,
        grid=(rows // tm, D_FF // tf),
        in_specs=[
            pl.BlockSpec((tm, D_MODEL), lambda i, f: (i, 0)),
            pl.BlockSpec((tm, D_MODEL), lambda i, f: (i, 0)),
            pl.BlockSpec((D_MODEL, tf), lambda i, f: (0, f)),
            pl.BlockSpec((tf, D_MODEL), lambda i, f: (f, 0)),
            pl.BlockSpec((1, D_MODEL), lambda i, f: (0, 0)),
        ],
        out_specs=pl.BlockSpec((tm, D_MODEL), lambda i, f: (i, 0)),
        out_shape=jax.ShapeDtypeStruct((rows, D_MODEL), F32),
        compiler_params=pltpu.CompilerParams(
            dimension_semantics=("parallel", "arbitrary"), vmem_limit_bytes=_vmem_limit(est)),
        name="mlp_final_norm",
    )(h2, x1, w_ff1, w_ff2, g_final)


def _trunk(x3, weights, g_final, rope):
    n_seq, seq_len, _ = x3.shape
    x = x3.reshape(n_seq * seq_len, D_MODEL)
    cos, sin = rope
    g_mix, w_in, g_q, g_k, w_a, w_b, w_out, g_mlp, w_ff1, w_ff2 = weights
    proj = _in_proj(x, g_mix, w_in)
    y_a = _dilated_mixer(proj, n_seq, seq_len)
    q, k = _qk_prep(proj, cos, sin, g_q, g_k, seq_len)
    y_b = _flash(q, k, proj, n_seq, seq_len)
    x1, h2 = _merge(y_a, y_b, proj, x, w_a, w_b, w_out, g_mlp)
    y = _mlp(h2, x1, w_ff1, w_ff2, g_final)
    return y.reshape(n_seq, seq_len, D_MODEL)


def kernel(x_prompt, x_sample, g_mix, w_in, g_q, g_k, w_branch, w_out, g_mlp, w_ff1, w_ff2, g_final):
    assert w_in.shape[0] == 1, "single-layer stack only"
    wb = w_branch[0].astype(BF16)
    weights = (g_mix[0][None], w_in[0].astype(BF16), g_q[0][None], g_k[0][None],
               wb[:DIL_WIDTH], wb[DIL_WIDTH:], w_out[0].astype(BF16), g_mlp[0][None],
               w_ff1[0].astype(BF16), w_ff2[0].astype(BF16))
    gf = g_final[None]
    outs = []
    for x3 in (x_prompt, x_sample):
        rope = _rope_tables(x3.shape[1])
        outs.append(_trunk(x3, weights, gf, rope))
    return tuple(outs)
```

```python
import functools

import jax
import jax.numpy as jnp
import numpy as np
from jax import lax
from jax.experimental import pallas as pl
from jax.experimental.pallas import tpu as pltpu

F32 = jnp.float32
BF16 = jnp.bfloat16

D_MODEL = 2048
HEAD_DIM = 128
DIL_GROUPS = ((128, 1), (512, 4), (2048, 16))
N_DIL_GROUPS = 3
DIL_HEADS = 4
DIL_WIDTH = DIL_HEADS * HEAD_DIM
GQA_Q_HEADS = 12
GQA_KV_HEADS = 4
GQA_GROUP = GQA_Q_HEADS // GQA_KV_HEADS
GQA_Q_WIDTH = GQA_Q_HEADS * HEAD_DIM
GQA_KV_WIDTH = GQA_KV_HEADS * HEAD_DIM
GRID_W = 64
ROPE_THETA = 10000.0
D_FF = 4 * D_MODEL
IN_WIDTH = 3 * N_DIL_GROUPS * DIL_WIDTH + GQA_Q_WIDTH + 2 * GQA_KV_WIDTH + 2 * D_MODEL
COL_AQ = 0
COL_AK = N_DIL_GROUPS * DIL_WIDTH
COL_AV = 2 * N_DIL_GROUPS * DIL_WIDTH
COL_BQ = 3 * N_DIL_GROUPS * DIL_WIDTH
COL_BK = COL_BQ + GQA_Q_WIDTH
COL_BV = COL_BK + GQA_KV_WIDTH
COL_GA = COL_BV + GQA_KV_WIDTH
COL_GB = COL_GA + D_MODEL
RMS_EPS = 1e-6
NEG_INF = -1e30
ATTN_SCALE = HEAD_DIM ** -0.5
Q_PRESCALE = ATTN_SCALE * 1.4426950408889634

V7X_VMEM_BYTES = 64 * 1024 * 1024
MIB = 1024 * 1024

IN_TM, IN_TN = 1024, 1024
DIL_TQ = 128
DIL_HALO = 64
FLASH_TQ, FLASH_TK = 512, 512
FLASH_ONES_ROWS = 16
MERGE_TM = 256
MLP_TM, MLP_TF = 512, 512


def _vmem_limit(nbytes):
    return int(min(nbytes + 16 * MIB, V7X_VMEM_BYTES - 8 * MIB))


def _rms(x, g):
    return x * lax.rsqrt(jnp.mean(x * x, axis=-1, keepdims=True) + RMS_EPS) * g


def _in_proj_kernel(x_ref, g_ref, w_ref, o_ref, h_ref):
    @pl.when(pl.program_id(1) == 0)
    def _():
        h_ref[...] = _rms(x_ref[...], g_ref[...]).astype(BF16)

    o_ref[...] = jnp.dot(h_ref[...], w_ref[...], preferred_element_type=F32).astype(o_ref.dtype)


def _in_proj(x, g_mix, w_in):
    rows = x.shape[0]
    tm, tn = IN_TM, IN_TN
    est = 2 * tm * D_MODEL * 4 + tm * D_MODEL * 2 + 2 * D_MODEL * tn * 2 + 2 * tm * tn * 2
    return pl.pallas_call(
        _in_proj_kernel,
        grid=(rows // tm, IN_WIDTH // tn),
        in_specs=[
            pl.BlockSpec((tm, D_MODEL), lambda i, j: (i, 0)),
            pl.BlockSpec((1, D_MODEL), lambda i, j: (0, 0)),
            pl.BlockSpec((D_MODEL, tn), lambda i, j: (0, j)),
        ],
        out_specs=pl.BlockSpec((tm, tn), lambda i, j: (i, j)),
        out_shape=jax.ShapeDtypeStruct((rows, IN_WIDTH), BF16),
        scratch_shapes=[pltpu.VMEM((tm, D_MODEL), BF16)],
        compiler_params=pltpu.CompilerParams(
            dimension_semantics=("parallel", "arbitrary"), vmem_limit_bytes=_vmem_limit(est)),
        name="in_proj",
    )(x, g_mix, w_in)


def _dilated_kernel(*refs, dilation, sub_len, slopes, has_state, emit_final):
    q_ref, kp_ref, kc_ref, kn_ref, vp_ref, vc_ref, vn_ref = refs[:7]
    refs = refs[7:]
    if has_state:
        acc_in, ml_in = refs[:2]
        refs = refs[2:]
    if emit_final:
        (y_ref,) = refs
    else:
        acc_out, ml_out = refs

    tq, tk = DIL_TQ, DIL_TQ + 2 * DIL_HALO
    tile = pl.program_id(2)
    row = lax.broadcasted_iota(jnp.int32, (tq, tk), 0)
    col = lax.broadcasted_iota(jnp.int32, (tq, tk), 1)
    rel = col - DIL_HALO - row
    key_idx = tile * tq - DIL_HALO + col
    mask = (jnp.abs(rel) <= DIL_HALO) & (key_idx >= 0) & (key_idx < sub_len)
    dist = (dilation * jnp.abs(rel)).astype(F32)
    lane = lax.broadcasted_iota(jnp.int32, (tq, HEAD_DIM), 1)
    ml_new = jnp.zeros((tq, HEAD_DIM), F32)

    for h in range(DIL_HEADS):
        hs = slice(h * HEAD_DIM, (h + 1) * HEAD_DIM)
        k_win = jnp.concatenate([kp_ref[:, hs], kc_ref[:, hs], kn_ref[:, hs]], axis=0)
        v_win = jnp.concatenate([vp_ref[:, hs], vc_ref[:, hs], vn_ref[:, hs]], axis=0)
        s = lax.dot_general(q_ref[:, hs], k_win, (((1,), (1,)), ((), ())),
                            preferred_element_type=F32) * ATTN_SCALE
        s = jnp.where(mask, s - slopes[h] * dist, NEG_INF)
        m_cur = jnp.max(s, axis=-1, keepdims=True)
        if has_state:
            m_prev = ml_in[:, h:h + 1]
            l_prev = ml_in[:, DIL_HEADS + h:DIL_HEADS + h + 1]
            m_new = jnp.maximum(m_prev, m_cur)
            alpha = jnp.exp(m_prev - m_new)
        else:
            m_new = m_cur
        p = jnp.exp(s - m_new)
        l_new = jnp.sum(p, axis=-1, keepdims=True)
        acc = jnp.dot(p.astype(BF16), v_win, preferred_element_type=F32)
        if has_state:
            l_new = alpha * l_prev + l_new
            acc = alpha * acc_in[:, hs] + acc
        if emit_final:
            y_ref[:, hs] = (acc / l_new).astype(y_ref.dtype)
        else:
            acc_out[:, hs] = acc
            ml_new = jnp.where(lane == h, m_new, ml_new)
            ml_new = jnp.where(lane == DIL_HEADS + h, l_new, ml_new)
    if not emit_final:
        ml_out[...] = ml_new


def _dilated_call(proj, group, state, emit_final, n_seq, seq_len):
    _, dilation = DIL_GROUPS[group]
    sub_len = seq_len // dilation
    nt = sub_len // DIL_TQ
    sub_rows = n_seq * sub_len
    col_blocks = IN_WIDTH // DIL_WIDTH
    n_heads = N_DIL_GROUPS * DIL_HEADS
    slopes = tuple(float(2.0 ** (-8.0 * (group * DIL_HEADS + h + 1) / n_heads)) for h in range(DIL_HEADS))
    proj_v = proj.reshape(sub_rows, dilation * IN_WIDTH)

    def col_of(base):
        return lambda b, r, i: r * col_blocks + (base // DIL_WIDTH + group)

    def cur_spec(base):
        c = col_of(base)
        return pl.BlockSpec((DIL_TQ, DIL_WIDTH), lambda b, r, i: (b * nt + i, c(b, r, i)))

    def prev_spec(base):
        c = col_of(base)
        return pl.BlockSpec((DIL_HALO, DIL_WIDTH),
                            lambda b, r, i: (b * 2 * nt + jnp.maximum(2 * i - 1, 0), c(b, r, i)))

    def next_spec(base):
        c = col_of(base)
        return pl.BlockSpec((DIL_HALO, DIL_WIDTH),
                            lambda b, r, i: (b * 2 * nt + jnp.minimum(2 * i + 2, 2 * nt - 1), c(b, r, i)))

    acc_spec = pl.BlockSpec((DIL_TQ, DIL_WIDTH), lambda b, r, i: (b * nt + i, r))
    ml_spec = pl.BlockSpec((DIL_TQ, HEAD_DIM), lambda b, r, i: (b * nt + i, r))

    in_specs = [cur_spec(COL_AQ), prev_spec(COL_AK), cur_spec(COL_AK), next_spec(COL_AK),
                prev_spec(COL_AV), cur_spec(COL_AV), next_spec(COL_AV)]
    args = [proj_v] * 7
    has_state = state is not None
    if has_state:
        acc, ml = state
        in_specs += [acc_spec, ml_spec]
        args += [acc.reshape(sub_rows, dilation * DIL_WIDTH), ml.reshape(sub_rows, dilation * HEAD_DIM)]
    if emit_final:
        out_specs = acc_spec
        out_shape = jax.ShapeDtypeStruct((sub_rows, dilation * DIL_WIDTH), BF16)
    else:
        out_specs = [acc_spec, ml_spec]
        out_shape = [jax.ShapeDtypeStruct((sub_rows, dilation * DIL_WIDTH), F32),
                     jax.ShapeDtypeStruct((sub_rows, dilation * HEAD_DIM), F32)]
    kern = functools.partial(_dilated_kernel, dilation=dilation, sub_len=sub_len, slopes=slopes,
                             has_state=has_state, emit_final=emit_final)
    out = pl.pallas_call(
        kern,
        grid=(n_seq, dilation, nt),
        in_specs=in_specs,
        out_specs=out_specs,
        out_shape=out_shape,
        compiler_params=pltpu.CompilerParams(
            dimension_semantics=("parallel", "parallel", "arbitrary"), vmem_limit_bytes=_vmem_limit(8 * MIB)),
        name=f"dilated_g{group}",
    )(*args)
    rows = n_seq * seq_len
    if emit_final:
        return out.reshape(rows, DIL_WIDTH)
    return out[0].reshape(rows, DIL_WIDTH), out[1].reshape(rows, HEAD_DIM)


def _dilated_mixer(proj, n_seq, seq_len):
    state = _dilated_call(proj, 2, None, False, n_seq, seq_len)
    state = _dilated_call(proj, 1, state, False, n_seq, seq_len)
    return _dilated_call(proj, 0, state, True, n_seq, seq_len)


def _rope_tables(seq_len):
    pos = jnp.arange(seq_len)
    row = (pos // GRID_W).astype(F32)
    col = (pos % GRID_W).astype(F32)
    half = HEAD_DIM // 2
    inv_freq = ROPE_THETA ** (-jnp.arange(0, half, 2, dtype=F32) / half)
    ang_r = row[:, None] * inv_freq
    ang_c = col[:, None] * inv_freq
    cos = jnp.concatenate([jnp.cos(ang_r), jnp.cos(ang_r), jnp.cos(ang_c), jnp.cos(ang_c)], axis=1)
    sin = jnp.concatenate([-jnp.sin(ang_r), jnp.sin(ang_r), -jnp.sin(ang_c), jnp.sin(ang_c)], axis=1)
    return cos, sin


def _prep_kernel(q_ref, k_ref, v_ref, cos_ref, sin_ref, gq_ref, gk_ref, qo_ref, ko_ref, vt_ref):
    cos = cos_ref[...]
    sin = sin_ref[...]
    quarter = HEAD_DIM // 4
    lane = lax.broadcasted_iota(jnp.int32, cos.shape, 1)
    upper = (lane & quarter) != 0

    def norm_rope(x, g, scale):
        y = _rms(x.astype(F32), g)
        partner = jnp.where(upper, pltpu.roll(y, quarter, 1), pltpu.roll(y, HEAD_DIM - quarter, 1))
        return (y * cos + partner * sin) * scale

    for h in range(GQA_Q_HEADS):
        hs = slice(h * HEAD_DIM, (h + 1) * HEAD_DIM)
        qo_ref[:, hs] = norm_rope(q_ref[:, hs], gq_ref[...], Q_PRESCALE).astype(qo_ref.dtype)
    for h in range(GQA_KV_HEADS):
        hs = slice(h * HEAD_DIM, (h + 1) * HEAD_DIM)
        ko_ref[:, hs] = norm_rope(k_ref[:, hs], gk_ref[...], 1.0).astype(ko_ref.dtype)
        vt_ref[h, 0] = v_ref[:, hs].astype(F32).T.astype(vt_ref.dtype)


def _qk_prep(proj, cos, sin, g_q, g_k, n_seq, seq_len):
    rows = proj.shape[0]
    tm = FLASH_TK
    tpos = seq_len // tm
    return pl.pallas_call(
        _prep_kernel,
        grid=(rows // tm,),
        in_specs=[
            pl.BlockSpec((tm, GQA_Q_WIDTH), lambda i: (i, COL_BQ // GQA_Q_WIDTH)),
            pl.BlockSpec((tm, GQA_KV_WIDTH), lambda i: (i, COL_BK // GQA_KV_WIDTH)),
            pl.BlockSpec((tm, GQA_KV_WIDTH), lambda i: (i, COL_BV // GQA_KV_WIDTH)),
            pl.BlockSpec((tm, HEAD_DIM), lambda i: (i % tpos, 0)),
            pl.BlockSpec((tm, HEAD_DIM), lambda i: (i % tpos, 0)),
            pl.BlockSpec((1, HEAD_DIM), lambda i: (0, 0)),
            pl.BlockSpec((1, HEAD_DIM), lambda i: (0, 0)),
        ],
        out_specs=[pl.BlockSpec((tm, GQA_Q_WIDTH), lambda i: (i, 0)),
                   pl.BlockSpec((tm, GQA_KV_WIDTH), lambda i: (i, 0)),
                   pl.BlockSpec((None, GQA_KV_HEADS, 1, HEAD_DIM, tm), lambda i: (i // tpos, 0, i % tpos, 0, 0))],
        out_shape=[jax.ShapeDtypeStruct((rows, GQA_Q_WIDTH), BF16),
                   jax.ShapeDtypeStruct((rows, GQA_KV_WIDTH), BF16),
                   jax.ShapeDtypeStruct((n_seq, GQA_KV_HEADS, tpos, HEAD_DIM, tm), BF16)],
        compiler_params=pltpu.CompilerParams(
            dimension_semantics=("parallel",), vmem_limit_bytes=_vmem_limit(16 * MIB)),
        name="qk_prep",
    )(proj, proj, proj, cos, sin, g_q, g_k)


def _flash_kernel(q_ref, k_ref, vt_ref, o_ref, st_sc, mx_sc, m_sc, acc_sc, *, n_chunks):
    tk = FLASH_TK
    m_sc[...] = jnp.full(m_sc.shape, NEG_INF, F32)
    acc_sc[...] = jnp.zeros(acc_sc.shape, F32)
    ones = jnp.ones((FLASH_ONES_ROWS, tk), BF16)

    def scores(c, slot, g):
        k = k_ref[pl.ds(pl.multiple_of(c * tk, tk), tk), :]
        q = q_ref[:, g * HEAD_DIM:(g + 1) * HEAD_DIM]
        st = lax.dot_general(k, q, (((1,), (1,)), ((), ())), preferred_element_type=F32)
        st_sc[slot, g] = st
        mx_sc[slot, g] = jnp.max(st, axis=0, keepdims=True)

    def consume(c, slot, g):
        vt = jnp.concatenate([vt_ref[c], ones], axis=0)
        m_prev = m_sc[g]
        m_new = jnp.maximum(m_prev, mx_sc[slot, g])
        alpha = jnp.exp2(m_prev - m_new)
        p = jnp.exp2(st_sc[slot, g] - m_new).astype(BF16)
        acc_sc[g] = alpha * acc_sc[g] + jnp.dot(vt, p, preferred_element_type=F32)
        m_sc[g] = m_new

    def step(c, slot, has_next):
        for g in range(GQA_GROUP):
            if has_next:
                scores(c + 1, 1 - slot, g)
            consume(c, slot, g)

    for g in range(GQA_GROUP):
        scores(0, 0, g)

    def pair(i, carry):
        step(2 * i, 0, True)
        step(2 * i + 1, 1, True)
        return carry

    lax.fori_loop(0, n_chunks // 2 - 1, pair, 0)
    step(n_chunks - 2, 0, True)
    step(n_chunks - 1, 1, False)
    for g in range(GQA_GROUP):
        acc = acc_sc[g]
        o_t = acc[:HEAD_DIM] / acc[HEAD_DIM:HEAD_DIM + 1]
        o_ref[:, g * HEAD_DIM:(g + 1) * HEAD_DIM] = o_t.T.astype(o_ref.dtype)


def _flash(q, k, vt, n_seq, seq_len):
    rows = q.shape[0]
    tq, tk = FLASH_TQ, FLASH_TK
    nq = seq_len // tq
    n_chunks = seq_len // tk
    gw = GQA_GROUP * HEAD_DIM
    acc_rows = HEAD_DIM + FLASH_ONES_ROWS
    assert n_chunks % 2 == 0 and n_chunks >= 2
    est = (2 * 2 * seq_len * HEAD_DIM * 2 + 4 * tq * gw * 2 + GQA_GROUP * (acc_rows + 8 * 3) * tq * 4
           + (2 * GQA_GROUP + 3) * tk * tq * 4)
    return pl.pallas_call(
        functools.partial(_flash_kernel, n_chunks=n_chunks),
        grid=(n_seq, GQA_KV_HEADS, nq),
        in_specs=[
            pl.BlockSpec((tq, gw), lambda b, h, i: (b * nq + i, h)),
            pl.BlockSpec((seq_len, HEAD_DIM), lambda b, h, i: (b, h)),
            pl.BlockSpec((None, None, n_chunks, HEAD_DIM, tk), lambda b, h, i: (b, h, 0, 0, 0)),
        ],
        out_specs=pl.BlockSpec((tq, gw), lambda b, h, i: (b * nq + i, h)),
        out_shape=jax.ShapeDtypeStruct((rows, GQA_Q_WIDTH), BF16),
        scratch_shapes=[pltpu.VMEM((2, GQA_GROUP, tk, tq), F32),
                        pltpu.VMEM((2, GQA_GROUP, 1, tq), F32),
                        pltpu.VMEM((GQA_GROUP, 1, tq), F32),
                        pltpu.VMEM((GQA_GROUP, acc_rows, tq), F32)],
        compiler_params=pltpu.CompilerParams(
            dimension_semantics=("parallel", "parallel", "arbitrary"), vmem_limit_bytes=_vmem_limit(est)),
        name="gqa_flash",
    )(q, k, vt)


def _sigmoid(x):
    return 1.0 / (1.0 + jnp.exp(-x))


def _merge_kernel(ya_ref, yb_ref, ga0_ref, ga1_ref, gb0_ref, gb1_ref, x_ref, wa_ref, wb_ref, wo_ref, g_ref,
                  x1_ref, h2_ref):
    half = D_MODEL // 2
    ya = ya_ref[...]
    yb = yb_ref[...]
    x1 = x_ref[...]
    for n, (ga_ref, gb_ref) in enumerate(((ga0_ref, gb0_ref), (ga1_ref, gb1_ref))):
        cs = slice(n * half, (n + 1) * half)
        o_a = jnp.dot(ya, wa_ref[:, cs], preferred_element_type=F32)
        o_b = jnp.dot(yb, wb_ref[:, cs], preferred_element_type=F32)
        merged = _sigmoid(ga_ref[...].astype(F32)) * o_a + _sigmoid(gb_ref[...].astype(F32)) * o_b
        x1 = x1 + jnp.dot(merged.astype(BF16), wo_ref[cs, :], preferred_element_type=F32)
    x1_ref[...] = x1
    h2_ref[...] = _rms(x1, g_ref[...]).astype(h2_ref.dtype)


def _merge(y_a, y_b, proj, x, w_a, w_b, w_out, g_mlp):
    rows = x.shape[0]
    tm = MERGE_TM
    half = D_MODEL // 2
    once = pl.Buffered(1)
    est = (2 * D_MODEL * D_MODEL * 2 + 2 * tm * (DIL_WIDTH + GQA_Q_WIDTH) * 2 + 2 * 4 * tm * half * 2
           + 4 * tm * D_MODEL * 4 + 2 * tm * D_MODEL * 2 + 6 * tm * D_MODEL * 4)
    return pl.pallas_call(
        _merge_kernel,
        grid=(rows // tm,),
        in_specs=[
            pl.BlockSpec((tm, DIL_WIDTH), lambda i: (i, 0)),
            pl.BlockSpec((tm, GQA_Q_WIDTH), lambda i: (i, 0)),
            pl.BlockSpec((tm, half), lambda i: (i, COL_GA // half)),
            pl.BlockSpec((tm, half), lambda i: (i, COL_GA // half + 1)),
            pl.BlockSpec((tm, half), lambda i: (i, COL_GB // half)),
            pl.BlockSpec((tm, half), lambda i: (i, COL_GB // half + 1)),
            pl.BlockSpec((tm, D_MODEL), lambda i: (i, 0)),
            pl.BlockSpec((DIL_WIDTH, D_MODEL), lambda i: (0, 0), pipeline_mode=once),
            pl.BlockSpec((GQA_Q_WIDTH, D_MODEL), lambda i: (0, 0), pipeline_mode=once),
            pl.BlockSpec((D_MODEL, D_MODEL), lambda i: (0, 0), pipeline_mode=once),
            pl.BlockSpec((1, D_MODEL), lambda i: (0, 0)),
        ],
        out_specs=[pl.BlockSpec((tm, D_MODEL), lambda i: (i, 0)),
                   pl.BlockSpec((tm, D_MODEL), lambda i: (i, 0))],
        out_shape=[jax.ShapeDtypeStruct((rows, D_MODEL), F32),
                   jax.ShapeDtypeStruct((rows, D_MODEL), BF16)],
        compiler_params=pltpu.CompilerParams(
            dimension_semantics=("parallel",), vmem_limit_bytes=_vmem_limit(est)),
        name="merge_out_proj",
    )(y_a, y_b, proj, proj, proj, proj, x, w_a, w_b, w_out, g_mlp)


def _mlp_kernel(h2_ref, x1_ref, w1_ref, w2_ref, g_ref, o_ref):
    f = pl.program_id(1)

    @pl.when(f == 0)
    def _():
        o_ref[...] = x1_ref[...]

    u = jnp.maximum(jnp.dot(h2_ref[...], w1_ref[...], preferred_element_type=F32), 0.0)
    o_ref[...] += jnp.dot((u * u).astype(BF16), w2_ref[...], preferred_element_type=F32)

    @pl.when(f == pl.num_programs(1) - 1)
    def _():
        o_ref[...] = _rms(o_ref[...], g_ref[...])


def _mlp(h2, x1, w_ff1, w_ff2, g_final):
    rows = h2.shape[0]
    tm, tf = MLP_TM, MLP_TF
    est = 2 * tm * D_MODEL * 2 + 4 * tm * D_MODEL * 4 + 4 * D_MODEL * tf * 2 + 3 * tm * tf * 4
    return pl.pallas_call(
        _mlp_kernel,
        grid=(rows // tm, D_FF // tf),
        in_specs=[
            pl.BlockSpec((tm, D_MODEL), lambda i, f: (i, 0)),
            pl.BlockSpec((tm, D_MODEL), lambda i, f: (i, 0)),
            pl.BlockSpec((D_MODEL, tf), lambda i, f: (0, f)),
            pl.BlockSpec((tf, D_MODEL), lambda i, f: (f, 0)),
            pl.BlockSpec((1, D_MODEL), lambda i, f: (0, 0)),
        ],
        out_specs=pl.BlockSpec((tm, D_MODEL), lambda i, f: (i, 0)),
        out_shape=jax.ShapeDtypeStruct((rows, D_MODEL), F32),
        compiler_params=pltpu.CompilerParams(
            dimension_semantics=("parallel", "arbitrary"), vmem_limit_bytes=_vmem_limit(est)),
        name="mlp_final_norm",
    )(h2, x1, w_ff1, w_ff2, g_final)


def _trunk(x3, weights, g_final, rope):
    n_seq, seq_len, _ = x3.shape
    x = x3.reshape(n_seq * seq_len, D_MODEL)
    cos, sin = rope
    g_mix, w_in, g_q, g_k, w_a, w_b, w_out, g_mlp, w_ff1, w_ff2 = weights
    proj = _in_proj(x, g_mix, w_in)
    y_a = _dilated_mixer(proj, n_seq, seq_len)
    q, k, vt = _qk_prep(proj, cos, sin, g_q, g_k, n_seq, seq_len)
    y_b = _flash(q, k, vt, n_seq, seq_len)
    x1, h2 = _merge(y_a, y_b, proj, x, w_a, w_b, w_out, g_mlp)
    y = _mlp(h2, x1, w_ff1, w_ff2, g_final)
    return y.reshape(n_seq, seq_len, D_MODEL)


def kernel(x_prompt, x_sample, g_mix, w_in, g_q, g_k, w_branch, w_out, g_mlp, w_ff1, w_ff2, g_final):
    assert w_in.shape[0] == 1, "single-layer stack only"
    wb = w_branch[0].astype(BF16)
    weights = (g_mix[0][None], w_in[0].astype(BF16), g_q[0][None], g_k[0][None],
               wb[:DIL_WIDTH], wb[DIL_WIDTH:], w_out[0].astype(BF16), g_mlp[0][None],
               w_ff1[0].astype(BF16), w_ff2[0].astype(BF16))
    gf = g_final[None]
    outs = []
    for x3 in (x_prompt, x_sample):
        rope = _rope_tables(x3.shape[1])
        outs.append(_trunk(x3, weights, gf, rope))
    return tuple(outs)
```

```python
import functools

import jax
import jax.numpy as jnp
from jax import lax
from jax.experimental import pallas as pl
from jax.experimental.pallas import tpu as pltpu

F32 = jnp.float32
BF16 = jnp.bfloat16

D_MODEL = 2048
HEAD_DIM = 128
DIL_GROUPS = ((128, 1), (512, 4), (2048, 16))
N_DIL_GROUPS = 3
DIL_HEADS = 4
DIL_WIDTH = DIL_HEADS * HEAD_DIM
DIL_QKV_WIDTH = 3 * DIL_WIDTH
GQA_Q_HEADS = 12
GQA_KV_HEADS = 4
GQA_GROUP = GQA_Q_HEADS // GQA_KV_HEADS
GQA_Q_WIDTH = GQA_Q_HEADS * HEAD_DIM
GQA_KV_WIDTH = GQA_KV_HEADS * HEAD_DIM
GRID_W = 64
ROPE_THETA = 10000.0
D_FF = 4 * D_MODEL
W_AQ = 0
W_AK = N_DIL_GROUPS * DIL_WIDTH
W_AV = 2 * N_DIL_GROUPS * DIL_WIDTH
W_REST = 3 * N_DIL_GROUPS * DIL_WIDTH
COL_BQ = DIL_QKV_WIDTH
COL_BK = COL_BQ + GQA_Q_WIDTH
COL_BV = COL_BK + GQA_KV_WIDTH
COL_GA = COL_BV + GQA_KV_WIDTH
COL_GB = COL_GA + D_MODEL
MAIN_WIDTH = COL_GB + D_MODEL
RMS_EPS = 1e-6
NEG_INF = -1e30
ATTN_SCALE = HEAD_DIM ** -0.5
Q_PRESCALE = ATTN_SCALE * 1.4426950408889634

V7X_VMEM_BYTES = 64 * 1024 * 1024
MIB = 1024 * 1024

IN_TM, IN_TN = 1024, 1024
DIL_TQ = 128
DIL_HALO = 64
FLASH_TQ, FLASH_TK = 512, 512
FLASH_ONES_ROWS = 16
MERGE_TM = 256
MLP_TM, MLP_TF = 512, 512


def _vmem_limit(nbytes):
    return int(min(nbytes + 16 * MIB, V7X_VMEM_BYTES - 8 * MIB))


def _rms(x, g):
    return x * lax.rsqrt(jnp.mean(x * x, axis=-1, keepdims=True) + RMS_EPS) * g


def _in_proj_kernel(x_ref, g_ref, w_ref, o_ref, h_ref):
    @pl.when(pl.program_id(1) == 0)
    def _():
        h_ref[...] = _rms(x_ref[...], g_ref[...]).astype(BF16)

    o_ref[...] = jnp.dot(h_ref[...], w_ref[...], preferred_element_type=F32).astype(o_ref.dtype)


def _in_proj(x, g_mix, w_main):
    rows = x.shape[0]
    tm, tn = IN_TM, IN_TN
    est = 2 * tm * D_MODEL * 4 + tm * D_MODEL * 2 + 2 * D_MODEL * tn * 2 + 2 * tm * tn * 2
    return pl.pallas_call(
        _in_proj_kernel,
        grid=(rows // tm, MAIN_WIDTH // tn),
        in_specs=[
            pl.BlockSpec((tm, D_MODEL), lambda i, j: (i, 0)),
            pl.BlockSpec((1, D_MODEL), lambda i, j: (0, 0)),
            pl.BlockSpec((D_MODEL, tn), lambda i, j: (0, j)),
        ],
        out_specs=pl.BlockSpec((tm, tn), lambda i, j: (i, j)),
        out_shape=jax.ShapeDtypeStruct((rows, MAIN_WIDTH), BF16),
        scratch_shapes=[pltpu.VMEM((tm, D_MODEL), BF16)],
        compiler_params=pltpu.CompilerParams(
            dimension_semantics=("parallel", "arbitrary"), vmem_limit_bytes=_vmem_limit(est)),
        name="in_proj",
    )(x, g_mix, w_main)


def _in_proj_dil_kernel(x_ref, g_ref, w_ref, o_ref, h_ref, res_ref, *, dilation):
    @pl.when(pl.program_id(1) == 0)
    def _():
        h_ref[...] = _rms(x_ref[...], g_ref[...]).astype(BF16)

    res = jnp.dot(h_ref[...], w_ref[...], preferred_element_type=F32)
    strips, tm, lanes = res_ref.shape
    for s in range(strips):
        res_ref[s] = res[:, s * lanes:(s + 1) * lanes]
    sub = tm // dilation
    for r in range(dilation):
        for s in range(strips):
            o_ref[r, :, s * lanes:(s + 1) * lanes] = (
                res_ref[s, pl.ds(r, sub, stride=dilation), :].astype(o_ref.dtype))


def _in_proj_dil(x, g_mix, w_qkv, dilation, n_seq, seq_len):
    rows = x.shape[0]
    tm, tn = IN_TM, DIL_WIDTH
    tiles_per_seq = seq_len // tm
    est = 2 * tm * D_MODEL * 4 + tm * D_MODEL * 2 + 2 * D_MODEL * tn * 2 + 2 * tm * tn * 2 + tm * tn * 4
    return pl.pallas_call(
        functools.partial(_in_proj_dil_kernel, dilation=dilation),
        grid=(rows // tm, DIL_QKV_WIDTH // tn),
        in_specs=[
            pl.BlockSpec((tm, D_MODEL), lambda i, j: (i, 0)),
            pl.BlockSpec((1, D_MODEL), lambda i, j: (0, 0)),
            pl.BlockSpec((D_MODEL, tn), lambda i, j: (0, j)),
        ],
        out_specs=pl.BlockSpec((None, dilation, tm // dilation, tn),
                               lambda i, j: (i // tiles_per_seq, 0, i % tiles_per_seq, j)),
        out_shape=jax.ShapeDtypeStruct((n_seq, dilation, seq_len // dilation, DIL_QKV_WIDTH), BF16),
        scratch_shapes=[pltpu.VMEM((tm, D_MODEL), BF16), pltpu.VMEM((tn // HEAD_DIM, tm, HEAD_DIM), F32)],
        compiler_params=pltpu.CompilerParams(
            dimension_semantics=("parallel", "arbitrary"), vmem_limit_bytes=_vmem_limit(est)),
        name=f"in_proj_dil{dilation}",
    )(x, g_mix, w_qkv)


def _dilated_kernel(*refs, dilation, sub_len, slopes, has_state, emit_final):
    q_ref, kp_ref, kc_ref, kn_ref, vp_ref, vc_ref, vn_ref = refs[:7]
    refs = refs[7:]
    if has_state:
        acc_in, ml_in = refs[:2]
        refs = refs[2:]
    if emit_final:
        y_ref = refs[0]
        refs = refs[1:]
    else:
        acc_out, ml_out = refs[:2]
        refs = refs[2:]
    strided = dilation > 1
    if strided:
        acc_t, ml_t = refs

    tq, tk = DIL_TQ, DIL_TQ + 2 * DIL_HALO
    tile = pl.program_id(1)
    residue = pl.program_id(2)

    def rows_of(r):
        return pl.ds(r, tq, stride=dilation)

    if strided and has_state:
        for r in range(dilation):
            @pl.when(residue == r)
            def _():
                for h in range(DIL_HEADS):
                    acc_t[h] = acc_in[h, rows_of(r), :]
                ml_t[...] = ml_in[rows_of(r), :]
    if has_state:
        acc_prev, ml_prev = (acc_t, ml_t) if strided else (acc_in, ml_in)
    if not emit_final:
        acc_new, ml_dst = (acc_t, ml_t) if strided else (acc_out, ml_out)

    row = lax.broadcasted_iota(jnp.int32, (tq, tk), 0)
    col = lax.broadcasted_iota(jnp.int32, (tq, tk), 1)
    rel = col - DIL_HALO - row
    key_idx = tile * tq - DIL_HALO + col
    mask = (jnp.abs(rel) <= DIL_HALO) & (key_idx >= 0) & (key_idx < sub_len)
    dist = (dilation * jnp.abs(rel)).astype(F32)
    lane = lax.broadcasted_iota(jnp.int32, (tq, HEAD_DIM), 1)
    ml_new = jnp.zeros((tq, HEAD_DIM), F32)

    for h in range(DIL_HEADS):
        hs = slice(h * HEAD_DIM, (h + 1) * HEAD_DIM)
        k_win = jnp.concatenate([kp_ref[:, hs], kc_ref[:, hs], kn_ref[:, hs]], axis=0)
        v_win = jnp.concatenate([vp_ref[:, hs], vc_ref[:, hs], vn_ref[:, hs]], axis=0)
        s = lax.dot_general(q_ref[:, hs], k_win, (((1,), (1,)), ((), ())),
                            preferred_element_type=F32) * ATTN_SCALE
        s = jnp.where(mask, s - slopes[h] * dist, NEG_INF)
        m_cur = jnp.max(s, axis=-1, keepdims=True)
        if has_state:
            m_prev = ml_prev[:, h:h + 1]
            l_prev = ml_prev[:, DIL_HEADS + h:DIL_HEADS + h + 1]
            m_new = jnp.maximum(m_prev, m_cur)
            alpha = jnp.exp(m_prev - m_new)
        else:
            m_new = m_cur
        p = jnp.exp(s - m_new)
        l_new = jnp.sum(p, axis=-1, keepdims=True)
        acc = jnp.dot(p.astype(BF16), v_win, preferred_element_type=F32)
        if has_state:
            l_new = alpha * l_prev + l_new
            acc = alpha * acc_prev[h] + acc
        if emit_final:
            y_ref[:, hs] = (acc / l_new).astype(y_ref.dtype)
        else:
            acc_new[h] = acc
            ml_new = jnp.where(lane == h, m_new, ml_new)
            ml_new = jnp.where(lane == DIL_HEADS + h, l_new, ml_new)
    if not emit_final:
        ml_dst[...] = ml_new
        if strided:
            for r in range(dilation):
                @pl.when(residue == r)
                def _():
                    for h in range(DIL_HEADS):
                        acc_out[h, rows_of(r), :] = acc_t[h]
                    ml_out[rows_of(r), :] = ml_t[...]


def _dilated_call(qkv, col0, group, state, emit_final, n_seq, seq_len):
    _, dilation = DIL_GROUPS[group]
    assert not (emit_final and dilation > 1), "the bf16 output is written in token order only"
    sub_len = seq_len // dilation
    nt = sub_len // DIL_TQ
    rows = n_seq * seq_len
    n_heads = N_DIL_GROUPS * DIL_HEADS
    slopes = tuple(float(2.0 ** (-8.0 * (group * DIL_HEADS + h + 1) / n_heads)) for h in range(DIL_HEADS))

    def cur_spec(kind):
        return pl.BlockSpec((None, None, DIL_TQ, DIL_WIDTH), lambda b, i, r: (b, r, i, col0 + kind))

    def prev_spec(kind):
        return pl.BlockSpec((None, None, DIL_HALO, DIL_WIDTH),
                            lambda b, i, r: (b, r, jnp.maximum(2 * i - 1, 0), col0 + kind))

    def next_spec(kind):
        return pl.BlockSpec((None, None, DIL_HALO, DIL_WIDTH),
                            lambda b, i, r: (b, r, jnp.minimum(2 * i + 2, 2 * nt - 1), col0 + kind))

    span = DIL_TQ * dilation
    acc_spec = pl.BlockSpec((DIL_HEADS, span, HEAD_DIM), lambda b, i, r: (0, b * nt + i, 0))
    ml_spec = pl.BlockSpec((span, HEAD_DIM), lambda b, i, r: (b * nt + i, 0))
    y_spec = pl.BlockSpec((span, DIL_WIDTH), lambda b, i, r: (b * nt + i, 0))

    in_specs = [cur_spec(0), prev_spec(1), cur_spec(1), next_spec(1), prev_spec(2), cur_spec(2), next_spec(2)]
    args = [qkv] * 7
    has_state = state is not None
    if has_state:
        in_specs += [acc_spec, ml_spec]
        args += list(state)
    if emit_final:
        out_specs = y_spec
        out_shape = jax.ShapeDtypeStruct((rows, DIL_WIDTH), BF16)
    else:
        out_specs = [acc_spec, ml_spec]
        out_shape = [jax.ShapeDtypeStruct((DIL_HEADS, rows, HEAD_DIM), F32),
                     jax.ShapeDtypeStruct((rows, HEAD_DIM), F32)]
    scratch = []
    if dilation > 1:
        scratch = [pltpu.VMEM((DIL_HEADS, DIL_TQ, HEAD_DIM), F32), pltpu.VMEM((DIL_TQ, HEAD_DIM), F32)]
    kern = functools.partial(_dilated_kernel, dilation=dilation, sub_len=sub_len, slopes=slopes,
                             has_state=has_state, emit_final=emit_final)
    est = 4 * span * (DIL_WIDTH + HEAD_DIM) * 4 + 2 * 7 * DIL_TQ * DIL_WIDTH * 2
    return pl.pallas_call(
        kern,
        grid=(n_seq, nt, dilation),
        in_specs=in_specs,
        out_specs=out_specs,
        out_shape=out_shape,
        scratch_shapes=scratch,
        compiler_params=pltpu.CompilerParams(
            dimension_semantics=("parallel", "parallel", "arbitrary"), vmem_limit_bytes=_vmem_limit(est)),
        name=f"dilated_g{group}",
    )(*args)


def _dilated_mixer(proj, qkv4, qkv16, n_seq, seq_len):
    state = _dilated_call(qkv16, 0, 2, None, False, n_seq, seq_len)
    state = _dilated_call(qkv4, 0, 1, state, False, n_seq, seq_len)
    qkv1 = proj.reshape(n_seq, 1, seq_len, MAIN_WIDTH)
    return _dilated_call(qkv1, 0, 0, state, True, n_seq, seq_len)


def _rope_tables(seq_len):
    pos = jnp.arange(seq_len)
    row = (pos // GRID_W).astype(F32)
    col = (pos % GRID_W).astype(F32)
    half = HEAD_DIM // 2
    inv_freq = ROPE_THETA ** (-jnp.arange(0, half, 2, dtype=F32) / half)
    ang_r = row[:, None] * inv_freq
    ang_c = col[:, None] * inv_freq
    cos = jnp.concatenate([jnp.cos(ang_r), jnp.cos(ang_r), jnp.cos(ang_c), jnp.cos(ang_c)], axis=1)
    sin = jnp.concatenate([-jnp.sin(ang_r), jnp.sin(ang_r), -jnp.sin(ang_c), jnp.sin(ang_c)], axis=1)
    return cos, sin


def _prep_kernel(q_ref, k_ref, v_ref, cos_ref, sin_ref, gq_ref, gk_ref, qo_ref, ko_ref, vt_ref):
    cos = cos_ref[...]
    sin = sin_ref[...]
    quarter = HEAD_DIM // 4
    lane = lax.broadcasted_iota(jnp.int32, cos.shape, 1)
    upper = (lane & quarter) != 0

    def norm_rope(x, g, scale):
        y = _rms(x.astype(F32), g)
        partner = jnp.where(upper, pltpu.roll(y, quarter, 1), pltpu.roll(y, HEAD_DIM - quarter, 1))
        return (y * cos + partner * sin) * scale

    for h in range(GQA_Q_HEADS):
        hs = slice(h * HEAD_DIM, (h + 1) * HEAD_DIM)
        qo_ref[:, hs] = norm_rope(q_ref[:, hs], gq_ref[...], Q_PRESCALE).astype(qo_ref.dtype)
    for h in range(GQA_KV_HEADS):
        hs = slice(h * HEAD_DIM, (h + 1) * HEAD_DIM)
        ko_ref[:, hs] = norm_rope(k_ref[:, hs], gk_ref[...], 1.0).astype(ko_ref.dtype)
        vt_ref[h, 0] = v_ref[:, hs].astype(F32).T.astype(vt_ref.dtype)


def _qk_prep(proj, cos, sin, g_q, g_k, n_seq, seq_len):
    rows = proj.shape[0]
    tm = FLASH_TK
    tpos = seq_len // tm
    return pl.pallas_call(
        _prep_kernel,
        grid=(rows // tm,),
        in_specs=[
            pl.BlockSpec((tm, GQA_Q_WIDTH), lambda i: (i, COL_BQ // GQA_Q_WIDTH)),
            pl.BlockSpec((tm, GQA_KV_WIDTH), lambda i: (i, COL_BK // GQA_KV_WIDTH)),
            pl.BlockSpec((tm, GQA_KV_WIDTH), lambda i: (i, COL_BV // GQA_KV_WIDTH)),
            pl.BlockSpec((tm, HEAD_DIM), lambda i: (i % tpos, 0)),
            pl.BlockSpec((tm, HEAD_DIM), lambda i: (i % tpos, 0)),
            pl.BlockSpec((1, HEAD_DIM), lambda i: (0, 0)),
            pl.BlockSpec((1, HEAD_DIM), lambda i: (0, 0)),
        ],
        out_specs=[pl.BlockSpec((tm, GQA_Q_WIDTH), lambda i: (i, 0)),
                   pl.BlockSpec((tm, GQA_KV_WIDTH), lambda i: (i, 0)),
                   pl.BlockSpec((None, GQA_KV_HEADS, 1, HEAD_DIM, tm), lambda i: (i // tpos, 0, i % tpos, 0, 0))],
        out_shape=[jax.ShapeDtypeStruct((rows, GQA_Q_WIDTH), BF16),
                   jax.ShapeDtypeStruct((rows, GQA_KV_WIDTH), BF16),
                   jax.ShapeDtypeStruct((n_seq, GQA_KV_HEADS, tpos, HEAD_DIM, tm), BF16)],
        compiler_params=pltpu.CompilerParams(
            dimension_semantics=("parallel",), vmem_limit_bytes=_vmem_limit(16 * MIB)),
        name="qk_prep",
    )(proj, proj, proj, cos, sin, g_q, g_k)


def _flash_kernel(q_ref, k_ref, vt_ref, o_ref, st_sc, mx_sc, m_sc, acc_sc, *, n_chunks):
    tk = FLASH_TK
    m_sc[...] = jnp.full(m_sc.shape, NEG_INF, F32)
    acc_sc[...] = jnp.zeros(acc_sc.shape, F32)
    ones = jnp.ones((FLASH_ONES_ROWS, tk), BF16)

    def scores(c, slot, g):
        k = k_ref[pl.ds(pl.multiple_of(c * tk, tk), tk), :]
        q = q_ref[:, g * HEAD_DIM:(g + 1) * HEAD_DIM]
        st = lax.dot_general(k, q, (((1,), (1,)), ((), ())), preferred_element_type=F32)
        st_sc[slot, g] = st
        mx_sc[slot, g] = jnp.max(st, axis=0, keepdims=True)

    def consume(c, slot, g):
        vt = jnp.concatenate([vt_ref[c], ones], axis=0)
        m_prev = m_sc[g]
        m_new = jnp.maximum(m_prev, mx_sc[slot, g])
        alpha = jnp.exp2(m_prev - m_new)
        p = jnp.exp2(st_sc[slot, g] - m_new).astype(BF16)
        acc_sc[g] = alpha * acc_sc[g] + jnp.dot(vt, p, preferred_element_type=F32)
        m_sc[g] = m_new

    def step(c, slot, has_next):
        for g in range(GQA_GROUP):
            if has_next:
                scores(c + 1, 1 - slot, g)
            consume(c, slot, g)

    for g in range(GQA_GROUP):
        scores(0, 0, g)

    def pair(i, carry):
        step(2 * i, 0, True)
        step(2 * i + 1, 1, True)
        return carry

    lax.fori_loop(0, n_chunks // 2 - 1, pair, 0)
    step(n_chunks - 2, 0, True)
    step(n_chunks - 1, 1, False)
    for g in range(GQA_GROUP):
        acc = acc_sc[g]
        o_t = acc[:HEAD_DIM] / acc[HEAD_DIM:HEAD_DIM + 1]
        o_ref[:, g * HEAD_DIM:(g + 1) * HEAD_DIM] = o_t.T.astype(o_ref.dtype)


def _flash(q, k, vt, n_seq, seq_len):
    rows = q.shape[0]
    tq, tk = FLASH_TQ, FLASH_TK
    nq = seq_len // tq
    n_chunks = seq_len // tk
    gw = GQA_GROUP * HEAD_DIM
    acc_rows = HEAD_DIM + FLASH_ONES_ROWS
    assert n_chunks % 2 == 0 and n_chunks >= 2
    est = (2 * 2 * seq_len * HEAD_DIM * 2 + 4 * tq * gw * 2 + GQA_GROUP * (acc_rows + 8 * 3) * tq * 4
           + (2 * GQA_GROUP + 3) * tk * tq * 4)
    return pl.pallas_call(
        functools.partial(_flash_kernel, n_chunks=n_chunks),
        grid=(n_seq, GQA_KV_HEADS, nq),
        in_specs=[
            pl.BlockSpec((tq, gw), lambda b, h, i: (b * nq + i, h)),
            pl.BlockSpec((seq_len, HEAD_DIM), lambda b, h, i: (b, h)),
            pl.BlockSpec((None, None, n_chunks, HEAD_DIM, tk), lambda b, h, i: (b, h, 0, 0, 0)),
        ],
        out_specs=pl.BlockSpec((tq, gw), lambda b, h, i: (b * nq + i, h)),
        out_shape=jax.ShapeDtypeStruct((rows, GQA_Q_WIDTH), BF16),
        scratch_shapes=[pltpu.VMEM((2, GQA_GROUP, tk, tq), F32),
                        pltpu.VMEM((2, GQA_GROUP, 1, tq), F32),
                        pltpu.VMEM((GQA_GROUP, 1, tq), F32),
                        pltpu.VMEM((GQA_GROUP, acc_rows, tq), F32)],
        compiler_params=pltpu.CompilerParams(
            dimension_semantics=("parallel", "parallel", "arbitrary"), vmem_limit_bytes=_vmem_limit(est)),
        name="gqa_flash",
    )(q, k, vt)


def _sigmoid(x):
    return 1.0 / (1.0 + jnp.exp(-x))


def _merge_kernel(ya_ref, yb_ref, ga0_ref, ga1_ref, gb0_ref, gb1_ref, x_ref, wa_ref, wb_ref, wo_ref, g_ref,
                  x1_ref, h2_ref):
    half = D_MODEL // 2
    ya = ya_ref[...]
    yb = yb_ref[...]
    x1 = x_ref[...]
    for n, (ga_ref, gb_ref) in enumerate(((ga0_ref, gb0_ref), (ga1_ref, gb1_ref))):
        cs = slice(n * half, (n + 1) * half)
        o_a = jnp.dot(ya, wa_ref[:, cs], preferred_element_type=F32)
        o_b = jnp.dot(yb, wb_ref[:, cs], preferred_element_type=F32)
        merged = _sigmoid(ga_ref[...].astype(F32)) * o_a + _sigmoid(gb_ref[...].astype(F32)) * o_b
        x1 = x1 + jnp.dot(merged.astype(BF16), wo_ref[cs, :], preferred_element_type=F32)
    x1_ref[...] = x1
    h2_ref[...] = _rms(x1, g_ref[...]).astype(h2_ref.dtype)


def _merge(y_a, y_b, proj, x, w_a, w_b, w_out, g_mlp):
    rows = x.shape[0]
    tm = MERGE_TM
    half = D_MODEL // 2
    once = pl.Buffered(1)
    est = (2 * D_MODEL * D_MODEL * 2 + 2 * tm * (DIL_WIDTH + GQA_Q_WIDTH) * 2 + 2 * 4 * tm * half * 2
           + 4 * tm * D_MODEL * 4 + 2 * tm * D_MODEL * 2 + 6 * tm * D_MODEL * 4)
    return pl.pallas_call(
        _merge_kernel,
        grid=(rows // tm,),
        in_specs=[
            pl.BlockSpec((tm, DIL_WIDTH), lambda i: (i, 0)),
            pl.BlockSpec((tm, GQA_Q_WIDTH), lambda i: (i, 0)),
            pl.BlockSpec((tm, half), lambda i: (i, COL_GA // half)),
            pl.BlockSpec((tm, half), lambda i: (i, COL_GA // half + 1)),
            pl.BlockSpec((tm, half), lambda i: (i, COL_GB // half)),
            pl.BlockSpec((tm, half), lambda i: (i, COL_GB // half + 1)),
            pl.BlockSpec((tm, D_MODEL), lambda i: (i, 0)),
            pl.BlockSpec((DIL_WIDTH, D_MODEL), lambda i: (0, 0), pipeline_mode=once),
            pl.BlockSpec((GQA_Q_WIDTH, D_MODEL), lambda i: (0, 0), pipeline_mode=once),
            pl.BlockSpec((D_MODEL, D_MODEL), lambda i: (0, 0), pipeline_mode=once),
            pl.BlockSpec((1, D_MODEL), lambda i: (0, 0)),
        ],
        out_specs=[pl.BlockSpec((tm, D_MODEL), lambda i: (i, 0)),
                   pl.BlockSpec((tm, D_MODEL), lambda i: (i, 0))],
        out_shape=[jax.ShapeDtypeStruct((rows, D_MODEL), F32),
                   jax.ShapeDtypeStruct((rows, D_MODEL), BF16)],
        compiler_params=pltpu.CompilerParams(
            dimension_semantics=("parallel",), vmem_limit_bytes=_vmem_limit(est)),
        name="merge_out_proj",
    )(y_a, y_b, proj, proj, proj, proj, x, w_a, w_b, w_out, g_mlp)


def _mlp_kernel(h2_ref, x1_ref, w1_ref, w2_ref, g_ref, o_ref):
    f = pl.program_id(1)

    @pl.when(f == 0)
    def _():
        o_ref[...] = x1_ref[...]

    u = jnp.maximum(jnp.dot(h2_ref[...], w1_ref[...], preferred_element_type=F32), 0.0)
    o_ref[...] += jnp.dot((u * u).astype(BF16), w2_ref[...], preferred_element_type=F32)

    @pl.when(f == pl.num_programs(1) - 1)
    def _():
        o_ref[...] = _rms(o_ref[...], g_ref[...])


def _mlp(h2, x1, w_ff1, w_ff2, g_final):
    rows = h2.shape[0]
    tm, tf = MLP_TM, MLP_TF
    est = 2 * tm * D_MODEL * 2 + 4 * tm * D_MODEL * 4 + 4 * D_MODEL * tf * 2 + 3 * tm * tf * 4
    return pl.pallas_call(
        _mlp_kernel,
        grid=(rows // tm, D_FF // tf),
        in_specs=[
            pl.BlockSpec((tm, D_MODEL), lambda i, f: (i, 0)),
            pl.BlockSpec((tm, D_MODEL), lambda i, f: (i, 0)),
            pl.BlockSpec((D_MODEL, tf), lambda i, f: (0, f)),
            pl.BlockSpec((tf, D_MODEL), lambda i, f: (f, 0)),
            pl.BlockSpec((1, D_MODEL), lambda i, f: (0, 0)),
        ],
        out_specs=pl.BlockSpec((tm, D_MODEL), lambda i, f: (i, 0)),
        out_shape=jax.ShapeDtypeStruct((rows, D_MODEL), F32),
        compiler_params=pltpu.CompilerParams(
            dimension_semantics=("parallel", "arbitrary"), vmem_limit_bytes=_vmem_limit(est)),
        name="mlp_final_norm",
    )(h2, x1, w_ff1, w_ff2, g_final)


def _trunk(x3, weights, g_final, rope):
    n_seq, seq_len, _ = x3.shape
    x = x3.reshape(n_seq * seq_len, D_MODEL)
    cos, sin = rope
    g_mix, w_main, w_qkv4, w_qkv16, g_q, g_k, w_a, w_b, w_out, g_mlp, w_ff1, w_ff2 = weights
    proj = _in_proj(x, g_mix, w_main)
    qkv4 = _in_proj_dil(x, g_mix, w_qkv4, DIL_GROUPS[1][1], n_seq, seq_len)
    qkv16 = _in_proj_dil(x, g_mix, w_qkv16, DIL_GROUPS[2][1], n_seq, seq_len)
    y_a = _dilated_mixer(proj, qkv4, qkv16, n_seq, seq_len)
    q, k, vt = _qk_prep(proj, cos, sin, g_q, g_k, n_seq, seq_len)
    y_b = _flash(q, k, vt, n_seq, seq_len)
    x1, h2 = _merge(y_a, y_b, proj, x, w_a, w_b, w_out, g_mlp)
    y = _mlp(h2, x1, w_ff1, w_ff2, g_final)
    return y.reshape(n_seq, seq_len, D_MODEL)


def _group_qkv_columns(w, group):
    cols = [w[:, base + group * DIL_WIDTH: base + (group + 1) * DIL_WIDTH] for base in (W_AQ, W_AK, W_AV)]
    return jnp.concatenate(cols, axis=1)


def kernel(x_prompt, x_sample, g_mix, w_in, g_q, g_k, w_branch, w_out, g_mlp, w_ff1, w_ff2, g_final):
    assert w_in.shape[0] == 1, "single-layer stack only"
    w = w_in[0].astype(BF16)
    w_main = jnp.concatenate([_group_qkv_columns(w, 0), w[:, W_REST:]], axis=1)
    wb = w_branch[0].astype(BF16)
    weights = (g_mix[0][None], w_main, _group_qkv_columns(w, 1), _group_qkv_columns(w, 2),
               g_q[0][None], g_k[0][None], wb[:DIL_WIDTH], wb[DIL_WIDTH:], w_out[0].astype(BF16),
               g_mlp[0][None], w_ff1[0].astype(BF16), w_ff2[0].astype(BF16))
    gf = g_final[None]
    outs = []
    for x3 in (x_prompt, x_sample):
        rope = _rope_tables(x3.shape[1])
        outs.append(_trunk(x3, weights, gf, rope))
    return tuple(outs)
```

```python
import functools

import jax
import jax.numpy as jnp
from jax import lax
from jax.experimental import pallas as pl
from jax.experimental.pallas import tpu as pltpu

F32 = jnp.float32
BF16 = jnp.bfloat16

D_MODEL = 2048
HEAD_DIM = 128
DIL_GROUPS = ((128, 1), (512, 4), (2048, 16))
N_DIL_GROUPS = 3
DIL_HEADS = 4
DIL_WIDTH = DIL_HEADS * HEAD_DIM
DIL_QKV_WIDTH = 3 * DIL_WIDTH
GQA_Q_HEADS = 12
GQA_KV_HEADS = 4
GQA_GROUP = GQA_Q_HEADS // GQA_KV_HEADS
GQA_Q_WIDTH = GQA_Q_HEADS * HEAD_DIM
GQA_KV_WIDTH = GQA_KV_HEADS * HEAD_DIM
GRID_W = 64
ROPE_THETA = 10000.0
D_FF = 4 * D_MODEL
W_AQ = 0
W_AK = N_DIL_GROUPS * DIL_WIDTH
W_AV = 2 * N_DIL_GROUPS * DIL_WIDTH
W_REST = 3 * N_DIL_GROUPS * DIL_WIDTH
COL_BQ = DIL_QKV_WIDTH
COL_BK = COL_BQ + GQA_Q_WIDTH
COL_BV = COL_BK + GQA_KV_WIDTH
COL_GA = COL_BV + GQA_KV_WIDTH
COL_GB = COL_GA + D_MODEL
MAIN_WIDTH = COL_GB + D_MODEL
RMS_EPS = 1e-6
NEG_INF = -1e30
ATTN_SCALE = HEAD_DIM ** -0.5
Q_PRESCALE = ATTN_SCALE * 1.4426950408889634

V7X_VMEM_BYTES = 64 * 1024 * 1024
MIB = 1024 * 1024

IN_TM, IN_TN = 1024, 1024
DIL_TQ = 128
DIL_HALO = 64
FLASH_TQ, FLASH_TK = 512, 512
FLASH_ONES_ROWS = 16
FLASH_UNROLL = 4
MERGE_TM = 256
MLP_TM, MLP_TF = 1024, 512


def _vmem_limit(nbytes):
    return int(min(nbytes + 16 * MIB, V7X_VMEM_BYTES - 8 * MIB))


def _rms(x, g):
    return x * lax.rsqrt(jnp.mean(x * x, axis=-1, keepdims=True) + RMS_EPS) * g


def _in_proj_kernel(x_ref, g_ref, w_ref, o_ref, h_ref):
    @pl.when(pl.program_id(1) == 0)
    def _():
        h_ref[...] = _rms(x_ref[...], g_ref[...]).astype(BF16)

    o_ref[...] = jnp.dot(h_ref[...], w_ref[...], preferred_element_type=F32).astype(o_ref.dtype)


def _in_proj(x, g_mix, w_main):
    rows = x.shape[0]
    tm, tn = IN_TM, IN_TN
    est = 2 * tm * D_MODEL * 4 + tm * D_MODEL * 2 + 2 * D_MODEL * tn * 2 + 2 * tm * tn * 2
    return pl.pallas_call(
        _in_proj_kernel,
        grid=(rows // tm, MAIN_WIDTH // tn),
        in_specs=[
            pl.BlockSpec((tm, D_MODEL), lambda i, j: (i, 0)),
            pl.BlockSpec((1, D_MODEL), lambda i, j: (0, 0)),
            pl.BlockSpec((D_MODEL, tn), lambda i, j: (0, j)),
        ],
        out_specs=pl.BlockSpec((tm, tn), lambda i, j: (i, j)),
        out_shape=jax.ShapeDtypeStruct((rows, MAIN_WIDTH), BF16),
        scratch_shapes=[pltpu.VMEM((tm, D_MODEL), BF16)],
        compiler_params=pltpu.CompilerParams(
            dimension_semantics=("parallel", "arbitrary"), vmem_limit_bytes=_vmem_limit(est)),
        name="in_proj",
    )(x, g_mix, w_main)


def _in_proj_dil_kernel(x_ref, g_ref, w_ref, o_ref, h_ref, res_ref, *, dilation):
    @pl.when(pl.program_id(1) == 0)
    def _():
        h_ref[...] = _rms(x_ref[...], g_ref[...]).astype(BF16)

    res = jnp.dot(h_ref[...], w_ref[...], preferred_element_type=F32)
    strips, tm, lanes = res_ref.shape
    for s in range(strips):
        res_ref[s] = res[:, s * lanes:(s + 1) * lanes]
    sub = tm // dilation
    for r in range(dilation):
        for s in range(strips):
            o_ref[r, :, s * lanes:(s + 1) * lanes] = (
                res_ref[s, pl.ds(r, sub, stride=dilation), :].astype(o_ref.dtype))


def _in_proj_dil(x, g_mix, w_qkv, dilation, n_seq, seq_len):
    rows = x.shape[0]
    tm, tn = IN_TM, DIL_WIDTH
    tiles_per_seq = seq_len // tm
    est = 2 * tm * D_MODEL * 4 + tm * D_MODEL * 2 + 2 * D_MODEL * tn * 2 + 2 * tm * tn * 2 + tm * tn * 4
    return pl.pallas_call(
        functools.partial(_in_proj_dil_kernel, dilation=dilation),
        grid=(rows // tm, DIL_QKV_WIDTH // tn),
        in_specs=[
            pl.BlockSpec((tm, D_MODEL), lambda i, j: (i, 0)),
            pl.BlockSpec((1, D_MODEL), lambda i, j: (0, 0)),
            pl.BlockSpec((D_MODEL, tn), lambda i, j: (0, j)),
        ],
        out_specs=pl.BlockSpec((None, dilation, tm // dilation, tn),
                               lambda i, j: (i // tiles_per_seq, 0, i % tiles_per_seq, j)),
        out_shape=jax.ShapeDtypeStruct((n_seq, dilation, seq_len // dilation, DIL_QKV_WIDTH), BF16),
        scratch_shapes=[pltpu.VMEM((tm, D_MODEL), BF16), pltpu.VMEM((tn // HEAD_DIM, tm, HEAD_DIM), F32)],
        compiler_params=pltpu.CompilerParams(
            dimension_semantics=("parallel", "arbitrary"), vmem_limit_bytes=_vmem_limit(est)),
        name=f"in_proj_dil{dilation}",
    )(x, g_mix, w_qkv)


def _dilated_kernel(*refs, dilation, sub_len, slopes, has_state, emit_final):
    q_ref, kp_ref, kc_ref, kn_ref, vp_ref, vc_ref, vn_ref = refs[:7]
    refs = refs[7:]
    if has_state:
        acc_in, ml_in = refs[:2]
        refs = refs[2:]
    if emit_final:
        y_ref = refs[0]
        refs = refs[1:]
    else:
        acc_out, ml_out = refs[:2]
        refs = refs[2:]
    strided = dilation > 1
    if strided:
        acc_t, ml_t = refs

    tq, tk = DIL_TQ, DIL_TQ + 2 * DIL_HALO
    tile = pl.program_id(1)
    residue = pl.program_id(2)

    def rows_of(r):
        return pl.ds(r, tq, stride=dilation)

    if strided and has_state:
        for r in range(dilation):
            @pl.when(residue == r)
            def _():
                for h in range(DIL_HEADS):
                    acc_t[h] = acc_in[h, rows_of(r), :]
                ml_t[...] = ml_in[rows_of(r), :]
    if has_state:
        acc_prev, ml_prev = (acc_t, ml_t) if strided else (acc_in, ml_in)
    if not emit_final:
        acc_new, ml_dst = (acc_t, ml_t) if strided else (acc_out, ml_out)

    row = lax.broadcasted_iota(jnp.int32, (tq, tk), 0)
    col = lax.broadcasted_iota(jnp.int32, (tq, tk), 1)
    rel = col - DIL_HALO - row
    key_idx = tile * tq - DIL_HALO + col
    mask = (jnp.abs(rel) <= DIL_HALO) & (key_idx >= 0) & (key_idx < sub_len)
    dist = (dilation * jnp.abs(rel)).astype(F32)
    lane = lax.broadcasted_iota(jnp.int32, (tq, HEAD_DIM), 1)
    ml_new = jnp.zeros((tq, HEAD_DIM), F32)

    heads = range(DIL_HEADS)
    cols = [slice(h * HEAD_DIM, (h + 1) * HEAD_DIM) for h in heads]
    ones = jnp.ones((tk, HEAD_DIM), BF16)
    scores, m_news, alphas = [], [], []
    for h in heads:
        k_win = jnp.concatenate([kp_ref[:, cols[h]], kc_ref[:, cols[h]], kn_ref[:, cols[h]]], axis=0)
        s = lax.dot_general(q_ref[:, cols[h]], k_win, (((1,), (1,)), ((), ())),
                            preferred_element_type=F32) * ATTN_SCALE
        scores.append(jnp.where(mask, s - slopes[h] * dist, NEG_INF))
    for h in heads:
        m_cur = jnp.max(scores[h], axis=-1, keepdims=True)
        if has_state:
            m_prev = ml_prev[:, h:h + 1]
            m_news.append(jnp.maximum(m_prev, m_cur))
            alphas.append(jnp.exp(m_prev - m_news[h]))
        else:
            m_news.append(m_cur)
    probs = [jnp.exp(scores[h] - m_news[h]).astype(BF16) for h in heads]
    for h in heads:
        v_win = jnp.concatenate([vp_ref[:, cols[h]], vc_ref[:, cols[h]], vn_ref[:, cols[h]]], axis=0)
        both = jnp.dot(probs[h], jnp.concatenate([v_win, ones], axis=1), preferred_element_type=F32)
        acc, l_new = both[:, :HEAD_DIM], both[:, HEAD_DIM:]
        if has_state:
            l_new = alphas[h] * ml_prev[:, DIL_HEADS + h:DIL_HEADS + h + 1] + l_new
            acc = alphas[h] * acc_prev[h] + acc
        if emit_final:
            y_ref[:, cols[h]] = (acc / l_new).astype(y_ref.dtype)
        else:
            acc_new[h] = acc
            ml_new = jnp.where(lane == h, m_news[h], ml_new)
            ml_new = jnp.where(lane == DIL_HEADS + h, l_new, ml_new)
    if not emit_final:
        ml_dst[...] = ml_new
        if strided:
            for r in range(dilation):
                @pl.when(residue == r)
                def _():
                    for h in range(DIL_HEADS):
                        acc_out[h, rows_of(r), :] = acc_t[h]
                    ml_out[rows_of(r), :] = ml_t[...]


def _dilated_call(qkv, col0, group, state, emit_final, n_seq, seq_len):
    _, dilation = DIL_GROUPS[group]
    assert not (emit_final and dilation > 1), "the bf16 output is written in token order only"
    sub_len = seq_len // dilation
    nt = sub_len // DIL_TQ
    rows = n_seq * seq_len
    n_heads = N_DIL_GROUPS * DIL_HEADS
    slopes = tuple(float(2.0 ** (-8.0 * (group * DIL_HEADS + h + 1) / n_heads)) for h in range(DIL_HEADS))

    def cur_spec(kind):
        return pl.BlockSpec((None, None, DIL_TQ, DIL_WIDTH), lambda b, i, r: (b, r, i, col0 + kind))

    def prev_spec(kind):
        return pl.BlockSpec((None, None, DIL_HALO, DIL_WIDTH),
                            lambda b, i, r: (b, r, jnp.maximum(2 * i - 1, 0), col0 + kind))

    def next_spec(kind):
        return pl.BlockSpec((None, None, DIL_HALO, DIL_WIDTH),
                            lambda b, i, r: (b, r, jnp.minimum(2 * i + 2, 2 * nt - 1), col0 + kind))

    span = DIL_TQ * dilation
    acc_spec = pl.BlockSpec((DIL_HEADS, span, HEAD_DIM), lambda b, i, r: (0, b * nt + i, 0))
    ml_spec = pl.BlockSpec((span, HEAD_DIM), lambda b, i, r: (b * nt + i, 0))
    y_spec = pl.BlockSpec((span, DIL_WIDTH), lambda b, i, r: (b * nt + i, 0))

    in_specs = [cur_spec(0), prev_spec(1), cur_spec(1), next_spec(1), prev_spec(2), cur_spec(2), next_spec(2)]
    args = [qkv] * 7
    has_state = state is not None
    if has_state:
        in_specs += [acc_spec, ml_spec]
        args += list(state)
    if emit_final:
        out_specs = y_spec
        out_shape = jax.ShapeDtypeStruct((rows, DIL_WIDTH), BF16)
    else:
        out_specs = [acc_spec, ml_spec]
        out_shape = [jax.ShapeDtypeStruct((DIL_HEADS, rows, HEAD_DIM), F32),
                     jax.ShapeDtypeStruct((rows, HEAD_DIM), F32)]
    scratch = []
    if dilation > 1:
        scratch = [pltpu.VMEM((DIL_HEADS, DIL_TQ, HEAD_DIM), F32), pltpu.VMEM((DIL_TQ, HEAD_DIM), F32)]
    kern = functools.partial(_dilated_kernel, dilation=dilation, sub_len=sub_len, slopes=slopes,
                             has_state=has_state, emit_final=emit_final)
    est = 4 * span * (DIL_WIDTH + HEAD_DIM) * 4 + 2 * 7 * DIL_TQ * DIL_WIDTH * 2
    return pl.pallas_call(
        kern,
        grid=(n_seq, nt, dilation),
        in_specs=in_specs,
        out_specs=out_specs,
        out_shape=out_shape,
        scratch_shapes=scratch,
        compiler_params=pltpu.CompilerParams(
            dimension_semantics=("parallel", "parallel", "arbitrary"), vmem_limit_bytes=_vmem_limit(est)),
        name=f"dilated_g{group}",
    )(*args)


def _dilated_mixer(proj, qkv4, qkv16, n_seq, seq_len):
    state = _dilated_call(qkv16, 0, 2, None, False, n_seq, seq_len)
    state = _dilated_call(qkv4, 0, 1, state, False, n_seq, seq_len)
    qkv1 = proj.reshape(n_seq, 1, seq_len, MAIN_WIDTH)
    return _dilated_call(qkv1, 0, 0, state, True, n_seq, seq_len)


def _rope_tables(seq_len):
    pos = jnp.arange(seq_len)
    row = (pos // GRID_W).astype(F32)
    col = (pos % GRID_W).astype(F32)
    half = HEAD_DIM // 2
    inv_freq = ROPE_THETA ** (-jnp.arange(0, half, 2, dtype=F32) / half)
    ang_r = row[:, None] * inv_freq
    ang_c = col[:, None] * inv_freq
    cos = jnp.concatenate([jnp.cos(ang_r), jnp.cos(ang_r), jnp.cos(ang_c), jnp.cos(ang_c)], axis=1)
    sin = jnp.concatenate([-jnp.sin(ang_r), jnp.sin(ang_r), -jnp.sin(ang_c), jnp.sin(ang_c)], axis=1)
    return cos, sin


def _prep_kernel(q_ref, k_ref, v_ref, cos_ref, sin_ref, gq_ref, gk_ref, qo_ref, ko_ref, vt_ref):
    cos = cos_ref[...]
    sin = sin_ref[...]
    quarter = HEAD_DIM // 4
    lane = lax.broadcasted_iota(jnp.int32, cos.shape, 1)
    upper = (lane & quarter) != 0

    def norm_rope(x, g, scale):
        y = _rms(x.astype(F32), g)
        partner = jnp.where(upper, pltpu.roll(y, quarter, 1), pltpu.roll(y, HEAD_DIM - quarter, 1))
        return (y * cos + partner * sin) * scale

    for h in range(GQA_Q_HEADS):
        hs = slice(h * HEAD_DIM, (h + 1) * HEAD_DIM)
        qo_ref[:, hs] = norm_rope(q_ref[:, hs], gq_ref[...], Q_PRESCALE).astype(qo_ref.dtype)
    for h in range(GQA_KV_HEADS):
        hs = slice(h * HEAD_DIM, (h + 1) * HEAD_DIM)
        ko_ref[:, hs] = norm_rope(k_ref[:, hs], gk_ref[...], 1.0).astype(ko_ref.dtype)
        vt_ref[h, 0] = v_ref[:, hs].astype(F32).T.astype(vt_ref.dtype)


def _qk_prep(proj, cos, sin, g_q, g_k, n_seq, seq_len):
    rows = proj.shape[0]
    tm = FLASH_TK
    tpos = seq_len // tm
    return pl.pallas_call(
        _prep_kernel,
        grid=(rows // tm,),
        in_specs=[
            pl.BlockSpec((tm, GQA_Q_WIDTH), lambda i: (i, COL_BQ // GQA_Q_WIDTH)),
            pl.BlockSpec((tm, GQA_KV_WIDTH), lambda i: (i, COL_BK // GQA_KV_WIDTH)),
            pl.BlockSpec((tm, GQA_KV_WIDTH), lambda i: (i, COL_BV // GQA_KV_WIDTH)),
            pl.BlockSpec((tm, HEAD_DIM), lambda i: (i % tpos, 0)),
            pl.BlockSpec((tm, HEAD_DIM), lambda i: (i % tpos, 0)),
            pl.BlockSpec((1, HEAD_DIM), lambda i: (0, 0)),
            pl.BlockSpec((1, HEAD_DIM), lambda i: (0, 0)),
        ],
        out_specs=[pl.BlockSpec((tm, GQA_Q_WIDTH), lambda i: (i, 0)),
                   pl.BlockSpec((tm, GQA_KV_WIDTH), lambda i: (i, 0)),
                   pl.BlockSpec((None, GQA_KV_HEADS, 1, HEAD_DIM, tm), lambda i: (i // tpos, 0, i % tpos, 0, 0))],
        out_shape=[jax.ShapeDtypeStruct((rows, GQA_Q_WIDTH), BF16),
                   jax.ShapeDtypeStruct((rows, GQA_KV_WIDTH), BF16),
                   jax.ShapeDtypeStruct((n_seq, GQA_KV_HEADS, tpos, HEAD_DIM, tm), BF16)],
        compiler_params=pltpu.CompilerParams(
            dimension_semantics=("parallel",), vmem_limit_bytes=_vmem_limit(16 * MIB)),
        name="qk_prep",
    )(proj, proj, proj, cos, sin, g_q, g_k)


def _flash_kernel(q_ref, k_ref, vt_ref, o_ref, st_sc, mx_sc, m_sc, acc_sc, *, n_chunks):
    tk = FLASH_TK
    m_sc[...] = jnp.full(m_sc.shape, NEG_INF, F32)
    acc_sc[...] = jnp.zeros(acc_sc.shape, F32)
    ones = jnp.ones((FLASH_ONES_ROWS, tk), BF16)

    def scores(c, slot, g):
        k = k_ref[pl.ds(pl.multiple_of(c * tk, tk), tk), :]
        q = q_ref[:, g * HEAD_DIM:(g + 1) * HEAD_DIM]
        st = lax.dot_general(k, q, (((1,), (1,)), ((), ())), preferred_element_type=F32)
        st_sc[slot, g] = st
        mx_sc[slot, g] = jnp.max(st, axis=0, keepdims=True)

    def consume(c, slot, g):
        vt = jnp.concatenate([vt_ref[c], ones], axis=0)
        m_prev = m_sc[g]
        m_new = jnp.maximum(m_prev, mx_sc[slot, g])
        alpha = jnp.exp2(m_prev - m_new)
        p = jnp.exp2(st_sc[slot, g] - m_new).astype(BF16)
        acc_sc[g] = alpha * acc_sc[g] + jnp.dot(vt, p, preferred_element_type=F32)
        m_sc[g] = m_new

    def step(c, slot, has_next):
        for g in range(GQA_GROUP):
            if has_next:
                scores(c + 1, 1 - slot, g)
            consume(c, slot, g)

    for g in range(GQA_GROUP):
        scores(0, 0, g)

    unroll = FLASH_UNROLL

    def group(i, carry):
        for u in range(unroll):
            step(unroll * i + u, u % 2, True)
        return carry

    lax.fori_loop(0, n_chunks // unroll - 1, group, 0)
    for u in range(unroll):
        step(n_chunks - unroll + u, u % 2, u < unroll - 1)
    for g in range(GQA_GROUP):
        acc = acc_sc[g]
        o_t = acc[:HEAD_DIM] / acc[HEAD_DIM:HEAD_DIM + 1]
        o_ref[:, g * HEAD_DIM:(g + 1) * HEAD_DIM] = o_t.T.astype(o_ref.dtype)


def _flash(q, k, vt, n_seq, seq_len):
    rows = q.shape[0]
    tq, tk = FLASH_TQ, FLASH_TK
    nq = seq_len // tq
    n_chunks = seq_len // tk
    gw = GQA_GROUP * HEAD_DIM
    acc_rows = HEAD_DIM + FLASH_ONES_ROWS
    assert FLASH_UNROLL % 2 == 0 and n_chunks % FLASH_UNROLL == 0
    est = (2 * 2 * seq_len * HEAD_DIM * 2 + 4 * tq * gw * 2 + GQA_GROUP * (acc_rows + 8 * 3) * tq * 4
           + (2 * GQA_GROUP + 3) * tk * tq * 4)
    return pl.pallas_call(
        functools.partial(_flash_kernel, n_chunks=n_chunks),
        grid=(n_seq, GQA_KV_HEADS, nq),
        in_specs=[
            pl.BlockSpec((tq, gw), lambda b, h, i: (b * nq + i, h)),
            pl.BlockSpec((seq_len, HEAD_DIM), lambda b, h, i: (b, h)),
            pl.BlockSpec((None, None, n_chunks, HEAD_DIM, tk), lambda b, h, i: (b, h, 0, 0, 0)),
        ],
        out_specs=pl.BlockSpec((tq, gw), lambda b, h, i: (b * nq + i, h)),
        out_shape=jax.ShapeDtypeStruct((rows, GQA_Q_WIDTH), BF16),
        scratch_shapes=[pltpu.VMEM((2, GQA_GROUP, tk, tq), F32),
                        pltpu.VMEM((2, GQA_GROUP, 1, tq), F32),
                        pltpu.VMEM((GQA_GROUP, 1, tq), F32),
                        pltpu.VMEM((GQA_GROUP, acc_rows, tq), F32)],
        compiler_params=pltpu.CompilerParams(
            dimension_semantics=("parallel", "parallel", "arbitrary"), vmem_limit_bytes=_vmem_limit(est)),
        name="gqa_flash",
    )(q, k, vt)


def _sigmoid(x):
    return 1.0 / (1.0 + jnp.exp(-x))


def _merge_kernel(ya_ref, yb_ref, ga0_ref, ga1_ref, gb0_ref, gb1_ref, x_ref, wa_ref, wb_ref, wo_ref, g_ref,
                  x1_ref, h2_ref):
    half = D_MODEL // 2
    ya = ya_ref[...]
    yb = yb_ref[...]
    x1 = x_ref[...]
    for n, (ga_ref, gb_ref) in enumerate(((ga0_ref, gb0_ref), (ga1_ref, gb1_ref))):
        cs = slice(n * half, (n + 1) * half)
        o_a = jnp.dot(ya, wa_ref[:, cs], preferred_element_type=F32)
        o_b = jnp.dot(yb, wb_ref[:, cs], preferred_element_type=F32)
        merged = _sigmoid(ga_ref[...].astype(F32)) * o_a + _sigmoid(gb_ref[...].astype(F32)) * o_b
        x1 = x1 + jnp.dot(merged.astype(BF16), wo_ref[cs, :], preferred_element_type=F32)
    x1_ref[...] = x1
    h2_ref[...] = _rms(x1, g_ref[...]).astype(h2_ref.dtype)


def _merge(y_a, y_b, proj, x, w_a, w_b, w_out, g_mlp):
    rows = x.shape[0]
    tm = MERGE_TM
    half = D_MODEL // 2
    once = pl.Buffered(1)
    est = (2 * D_MODEL * D_MODEL * 2 + 2 * tm * (DIL_WIDTH + GQA_Q_WIDTH) * 2 + 2 * 4 * tm * half * 2
           + 4 * tm * D_MODEL * 4 + 2 * tm * D_MODEL * 2 + 6 * tm * D_MODEL * 4)
    return pl.pallas_call(
        _merge_kernel,
        grid=(rows // tm,),
        in_specs=[
            pl.BlockSpec((tm, DIL_WIDTH), lambda i: (i, 0)),
            pl.BlockSpec((tm, GQA_Q_WIDTH), lambda i: (i, 0)),
            pl.BlockSpec((tm, half), lambda i: (i, COL_GA // half)),
            pl.BlockSpec((tm, half), lambda i: (i, COL_GA // half + 1)),
            pl.BlockSpec((tm, half), lambda i: (i, COL_GB // half)),
            pl.BlockSpec((tm, half), lambda i: (i, COL_GB // half + 1)),
            pl.BlockSpec((tm, D_MODEL), lambda i: (i, 0)),
            pl.BlockSpec((DIL_WIDTH, D_MODEL), lambda i: (0, 0), pipeline_mode=once),
            pl.BlockSpec((GQA_Q_WIDTH, D_MODEL), lambda i: (0, 0), pipeline_mode=once),
            pl.BlockSpec((D_MODEL, D_MODEL), lambda i: (0, 0), pipeline_mode=once),
            pl.BlockSpec((1, D_MODEL), lambda i: (0, 0)),
        ],
        out_specs=[pl.BlockSpec((tm, D_MODEL), lambda i: (i, 0)),
                   pl.BlockSpec((tm, D_MODEL), lambda i: (i, 0))],
        out_shape=[jax.ShapeDtypeStruct((rows, D_MODEL), F32),
                   jax.ShapeDtypeStruct((rows, D_MODEL), BF16)],
        compiler_params=pltpu.CompilerParams(
            dimension_semantics=("parallel",), vmem_limit_bytes=_vmem_limit(est)),
        name="merge_out_proj",
    )(y_a, y_b, proj, proj, proj, proj, x, w_a, w_b, w_out, g_mlp)


def _mlp_kernel(h2_ref, x1_ref, w1_ref, w2_ref, g_ref, o_ref):
    f = pl.program_id(1)

    @pl.when(f == 0)
    def _():
        o_ref[...] = x1_ref[...]

    u = jnp.maximum(jnp.dot(h2_ref[...], w1_ref[...], preferred_element_type=F32), 0.0)
    o_ref[...] += jnp.dot((u * u).astype(BF16), w2_ref[...], preferred_element_type=F32)

    @pl.when(f == pl.num_programs(1) - 1)
    def _():
        o_ref[...] = _rms(o_ref[...], g_ref[...])


def _mlp(h2, x1, w_ff1, w_ff2, g_final):
    rows = h2.shape[0]
    tm, tf = MLP_TM, MLP_TF
    est = 2 * tm * D_MODEL * 2 + 3 * tm * D_MODEL * 4 + 4 * D_MODEL * tf * 2
    return pl.pallas_call(
        _mlp_kernel,
        grid=(rows // tm, D_FF // tf),
        in_specs=[
            pl.BlockSpec((tm, D_MODEL), lambda i, f: (i, 0)),
            pl.BlockSpec((tm, D_MODEL), lambda i, f: (i, 0), pipeline_mode=pl.Buffered(1)),
            pl.BlockSpec((D_MODEL, tf), lambda i, f: (0, f)),
            pl.BlockSpec((tf, D_MODEL), lambda i, f: (f, 0)),
            pl.BlockSpec((1, D_MODEL), lambda i, f: (0, 0)),
        ],
        out_specs=pl.BlockSpec((tm, D_MODEL), lambda i, f: (i, 0)),
        out_shape=jax.ShapeDtypeStruct((rows, D_MODEL), F32),
        compiler_params=pltpu.CompilerParams(
            dimension_semantics=("parallel", "arbitrary"), vmem_limit_bytes=_vmem_limit(est)),
        name="mlp_final_norm",
    )(h2, x1, w_ff1, w_ff2, g_final)


def _trunk(x3, weights, g_final, rope):
    n_seq, seq_len, _ = x3.shape
    x = x3.reshape(n_seq * seq_len, D_MODEL)
    cos, sin = rope
    g_mix, w_main, w_qkv4, w_qkv16, g_q, g_k, w_a, w_b, w_out, g_mlp, w_ff1, w_ff2 = weights
    proj = _in_proj(x, g_mix, w_main)
    qkv4 = _in_proj_dil(x, g_mix, w_qkv4, DIL_GROUPS[1][1], n_seq, seq_len)
    qkv16 = _in_proj_dil(x, g_mix, w_qkv16, DIL_GROUPS[2][1], n_seq, seq_len)
    y_a = _dilated_mixer(proj, qkv4, qkv16, n_seq, seq_len)
    q, k, vt = _qk_prep(proj, cos, sin, g_q, g_k, n_seq, seq_len)
    y_b = _flash(q, k, vt, n_seq, seq_len)
    x1, h2 = _merge(y_a, y_b, proj, x, w_a, w_b, w_out, g_mlp)
    y = _mlp(h2, x1, w_ff1, w_ff2, g_final)
    return y.reshape(n_seq, seq_len, D_MODEL)


def _group_qkv_columns(w, group):
    cols = [w[:, base + group * DIL_WIDTH: base + (group + 1) * DIL_WIDTH] for base in (W_AQ, W_AK, W_AV)]
    return jnp.concatenate(cols, axis=1)


def kernel(x_prompt, x_sample, g_mix, w_in, g_q, g_k, w_branch, w_out, g_mlp, w_ff1, w_ff2, g_final):
    assert w_in.shape[0] == 1, "single-layer stack only"
    w = w_in[0].astype(BF16)
    w_main = jnp.concatenate([_group_qkv_columns(w, 0), w[:, W_REST:]], axis=1)
    wb = w_branch[0].astype(BF16)
    weights = (g_mix[0][None], w_main, _group_qkv_columns(w, 1), _group_qkv_columns(w, 2),
               g_q[0][None], g_k[0][None], wb[:DIL_WIDTH], wb[DIL_WIDTH:], w_out[0].astype(BF16),
               g_mlp[0][None], w_ff1[0].astype(BF16), w_ff2[0].astype(BF16))
    gf = g_final[None]
    outs = []
    for x3 in (x_prompt, x_sample):
        rope = _rope_tables(x3.shape[1])
        outs.append(_trunk(x3, weights, gf, rope))
    return tuple(outs)
```

```python
import functools

import jax
import jax.numpy as jnp
from jax import lax
from jax.experimental import pallas as pl
from jax.experimental.pallas import tpu as pltpu

F32 = jnp.float32
BF16 = jnp.bfloat16

D_MODEL = 2048
HEAD_DIM = 128
DIL_GROUPS = ((128, 1), (512, 4), (2048, 16))
N_DIL_GROUPS = 3
DIL_HEADS = 4
DIL_WIDTH = DIL_HEADS * HEAD_DIM
DIL_QKV_WIDTH = 3 * DIL_WIDTH
GQA_Q_HEADS = 12
GQA_KV_HEADS = 4
GQA_GROUP = GQA_Q_HEADS // GQA_KV_HEADS
GQA_Q_WIDTH = GQA_Q_HEADS * HEAD_DIM
GQA_KV_WIDTH = GQA_KV_HEADS * HEAD_DIM
GRID_W = 64
ROPE_THETA = 10000.0
D_FF = 4 * D_MODEL
W_AQ = 0
W_AK = N_DIL_GROUPS * DIL_WIDTH
W_AV = 2 * N_DIL_GROUPS * DIL_WIDTH
W_REST = 3 * N_DIL_GROUPS * DIL_WIDTH
COL_BQ = DIL_QKV_WIDTH
COL_BK = COL_BQ + GQA_Q_WIDTH
COL_BV = COL_BK + GQA_KV_WIDTH
COL_GA = COL_BV + GQA_KV_WIDTH
COL_GB = COL_GA + D_MODEL
MAIN_WIDTH = COL_GB + D_MODEL
RMS_EPS = 1e-6
NEG_INF = -1e30
ATTN_SCALE = HEAD_DIM ** -0.5
Q_PRESCALE = ATTN_SCALE * 1.4426950408889634

V7X_VMEM_BYTES = 64 * 1024 * 1024
MIB = 1024 * 1024

IN_TM, IN_TN = 1024, 1024
DIL_TQ = 128
DIL_HALO = 64
DIL_UNITS = 4
FLASH_TQ, FLASH_TK = 512, 512
FLASH_ONES_ROWS = 16
FLASH_UNROLL = 4
MERGE_TM = 256
MLP_TM, MLP_TF = 1024, 512


def _vmem_limit(nbytes):
    return int(min(nbytes + 16 * MIB, V7X_VMEM_BYTES - 8 * MIB))


def _rms(x, g):
    return x * lax.rsqrt(jnp.mean(x * x, axis=-1, keepdims=True) + RMS_EPS) * g


def _in_proj_kernel(x_ref, g_ref, w_ref, o_ref, h_ref):
    @pl.when(pl.program_id(1) == 0)
    def _():
        h_ref[...] = _rms(x_ref[...], g_ref[...]).astype(BF16)

    o_ref[...] = jnp.dot(h_ref[...], w_ref[...], preferred_element_type=F32).astype(o_ref.dtype)


def _in_proj(x, g_mix, w_main):
    rows = x.shape[0]
    tm, tn = IN_TM, IN_TN
    est = 2 * tm * D_MODEL * 4 + tm * D_MODEL * 2 + 2 * D_MODEL * tn * 2 + 2 * tm * tn * 2
    return pl.pallas_call(
        _in_proj_kernel,
        grid=(rows // tm, MAIN_WIDTH // tn),
        in_specs=[
            pl.BlockSpec((tm, D_MODEL), lambda i, j: (i, 0)),
            pl.BlockSpec((1, D_MODEL), lambda i, j: (0, 0)),
            pl.BlockSpec((D_MODEL, tn), lambda i, j: (0, j)),
        ],
        out_specs=pl.BlockSpec((tm, tn), lambda i, j: (i, j)),
        out_shape=jax.ShapeDtypeStruct((rows, MAIN_WIDTH), BF16),
        scratch_shapes=[pltpu.VMEM((tm, D_MODEL), BF16)],
        compiler_params=pltpu.CompilerParams(
            dimension_semantics=("parallel", "arbitrary"), vmem_limit_bytes=_vmem_limit(est)),
        name="in_proj",
    )(x, g_mix, w_main)


def _in_proj_dil_kernel(x_ref, g_ref, w_ref, o_ref, h_ref, res_ref, *, dilation):
    @pl.when(pl.program_id(1) == 0)
    def _():
        h_ref[...] = _rms(x_ref[...], g_ref[...]).astype(BF16)

    res = jnp.dot(h_ref[...], w_ref[...], preferred_element_type=F32)
    strips, tm, lanes = res_ref.shape
    for s in range(strips):
        res_ref[s] = res[:, s * lanes:(s + 1) * lanes]
    sub = tm // dilation
    for r in range(dilation):
        for s in range(strips):
            o_ref[r, :, s * lanes:(s + 1) * lanes] = (
                res_ref[s, pl.ds(r, sub, stride=dilation), :].astype(o_ref.dtype))


def _in_proj_dil(x, g_mix, w_qkv, dilation, n_seq, seq_len):
    rows = x.shape[0]
    tm, tn = IN_TM, DIL_WIDTH
    tiles_per_seq = seq_len // tm
    est = 2 * tm * D_MODEL * 4 + tm * D_MODEL * 2 + 2 * D_MODEL * tn * 2 + 2 * tm * tn * 2 + tm * tn * 4
    return pl.pallas_call(
        functools.partial(_in_proj_dil_kernel, dilation=dilation),
        grid=(rows // tm, DIL_QKV_WIDTH // tn),
        in_specs=[
            pl.BlockSpec((tm, D_MODEL), lambda i, j: (i, 0)),
            pl.BlockSpec((1, D_MODEL), lambda i, j: (0, 0)),
            pl.BlockSpec((D_MODEL, tn), lambda i, j: (0, j)),
        ],
        out_specs=pl.BlockSpec((None, dilation, tm // dilation, tn),
                               lambda i, j: (i // tiles_per_seq, 0, i % tiles_per_seq, j)),
        out_shape=jax.ShapeDtypeStruct((n_seq, dilation, seq_len // dilation, DIL_QKV_WIDTH), BF16),
        scratch_shapes=[pltpu.VMEM((tm, D_MODEL), BF16), pltpu.VMEM((tn // HEAD_DIM, tm, HEAD_DIM), F32)],
        compiler_params=pltpu.CompilerParams(
            dimension_semantics=("parallel", "arbitrary"), vmem_limit_bytes=_vmem_limit(est)),
        name=f"in_proj_dil{dilation}",
    )(x, g_mix, w_qkv)


def _dilated_kernel(*refs, dilation, sub_len, slopes, has_state, emit_final):
    q_ref, kp_ref, kc_ref, kn_ref, vp_ref, vc_ref, vn_ref = refs[:7]
    refs = refs[7:]
    if has_state:
        acc_in, ml_in = refs[:2]
        refs = refs[2:]
    if emit_final:
        y_ref = refs[0]
        refs = refs[1:]
    else:
        acc_out, ml_out = refs[:2]
        refs = refs[2:]
    strided = dilation > 1
    if strided:
        acc_t, ml_t = refs

    tq, tk = DIL_TQ, DIL_TQ + 2 * DIL_HALO
    units = range(DIL_UNITS)
    heads = range(DIL_HEADS)
    cols = [slice(h * HEAD_DIM, (h + 1) * HEAD_DIM) for h in heads]
    step = pl.program_id(1)
    n_rgroups = dilation // DIL_UNITS if strided else 1

    def for_residue_group(fn):
        if n_rgroups == 1:
            fn(0)
        else:
            for g in range(n_rgroups):
                pl.when(pl.program_id(2) == g)(functools.partial(fn, g * DIL_UNITS))

    def state_rows(r0, u):
        return pl.ds(r0 + u, tq, stride=dilation)

    if strided and has_state:
        def gather(r0):
            for u in units:
                for h in heads:
                    acc_t[u, h] = acc_in[h, state_rows(r0, u), :]
                ml_t[u] = ml_in[state_rows(r0, u), :]
        for_residue_group(gather)

    def unit_rows(u):
        return slice(u * tq, (u + 1) * tq)

    def prev_acc(u, h):
        return acc_t[u, h] if strided else acc_in[h, unit_rows(u), :]

    def prev_ml(u):
        return ml_t[u] if strided else ml_in[unit_rows(u), :]

    if strided:
        def q_of(u, h):
            return q_ref[u, :, cols[h]]

        def window(p_ref, c_ref, n_ref, u, h):
            return jnp.concatenate([p_ref[u, :, cols[h]], c_ref[u, :, cols[h]], n_ref[u, :, cols[h]]], axis=0)
    else:
        def q_of(u, h):
            return q_ref[unit_rows(u), cols[h]]

        def window(p_ref, c_ref, n_ref, u, h):
            lo, hi = u * tq - DIL_HALO, u * tq + tq + DIL_HALO
            parts = []
            if lo < 0:
                parts.append(p_ref[:, cols[h]])
            parts.append(c_ref[max(lo, 0):min(hi, DIL_UNITS * tq), cols[h]])
            if hi > DIL_UNITS * tq:
                parts.append(n_ref[:, cols[h]])
            return jnp.concatenate(parts, axis=0) if len(parts) > 1 else parts[0]

    row = lax.broadcasted_iota(jnp.int32, (tq, tk), 0)
    col = lax.broadcasted_iota(jnp.int32, (tq, tk), 1)
    rel = col - DIL_HALO - row
    band = jnp.abs(rel) <= DIL_HALO
    dist = (dilation * jnp.abs(rel)).astype(F32)

    def mask_of(tile):
        key_idx = tile * tq - DIL_HALO + col
        return band & (key_idx >= 0) & (key_idx < sub_len)

    if strided:
        shared_mask = mask_of(step)
        masks = [shared_mask for _ in units]
    else:
        masks = [mask_of(step * DIL_UNITS + u) for u in units]
    lane = lax.broadcasted_iota(jnp.int32, (tq, HEAD_DIM), 1)
    ones = jnp.ones((tk, HEAD_DIM), BF16)

    pairs = [(u, h) for u in units for h in heads]
    scores, m_news, alphas, probs = {}, {}, {}, {}
    for u, h in pairs:
        s = lax.dot_general(q_of(u, h), window(kp_ref, kc_ref, kn_ref, u, h), (((1,), (1,)), ((), ())),
                            preferred_element_type=F32) * ATTN_SCALE
        scores[u, h] = jnp.where(masks[u], s - slopes[h] * dist, NEG_INF)
    ml_prevs = [prev_ml(u) for u in units] if has_state else None
    for u, h in pairs:
        m_cur = jnp.max(scores[u, h], axis=-1, keepdims=True)
        if has_state:
            m_prev = ml_prevs[u][:, h:h + 1]
            m_news[u, h] = jnp.maximum(m_prev, m_cur)
            alphas[u, h] = jnp.exp(m_prev - m_news[u, h])
        else:
            m_news[u, h] = m_cur
    for u, h in pairs:
        probs[u, h] = jnp.exp(scores[u, h] - m_news[u, h]).astype(BF16)
    ml_news = [jnp.zeros((tq, HEAD_DIM), F32) for _ in units]
    for u, h in pairs:
        v_aug = jnp.concatenate([window(vp_ref, vc_ref, vn_ref, u, h), ones], axis=1)
        both = jnp.dot(probs[u, h], v_aug, preferred_element_type=F32)
        acc, l_new = both[:, :HEAD_DIM], both[:, HEAD_DIM:]
        if has_state:
            l_new = alphas[u, h] * ml_prevs[u][:, DIL_HEADS + h:DIL_HEADS + h + 1] + l_new
            acc = alphas[u, h] * prev_acc(u, h) + acc
        if emit_final:
            y_ref[unit_rows(u), cols[h]] = (acc / l_new).astype(y_ref.dtype)
        elif strided:
            acc_t[u, h] = acc
        else:
            acc_out[h, unit_rows(u), :] = acc
        ml_news[u] = jnp.where(lane == h, m_news[u, h], ml_news[u])
        ml_news[u] = jnp.where(lane == DIL_HEADS + h, l_new, ml_news[u])
    if not emit_final:
        if strided:
            for u in units:
                ml_t[u] = ml_news[u]

            def scatter(r0):
                for u in units:
                    for h in heads:
                        acc_out[h, state_rows(r0, u), :] = acc_t[u, h]
                    ml_out[state_rows(r0, u), :] = ml_t[u]
            for_residue_group(scatter)
        else:
            for u in units:
                ml_out[unit_rows(u), :] = ml_news[u]


def _dilated_call(qkv, col0, group, state, emit_final, n_seq, seq_len):
    window, dilation = DIL_GROUPS[group]
    assert window == 2 * DIL_HALO * dilation
    assert not (emit_final and dilation > 1), "the bf16 output is written in token order only"
    sub_len = seq_len // dilation
    nt = sub_len // DIL_TQ
    rows = n_seq * seq_len
    n_heads = N_DIL_GROUPS * DIL_HEADS
    slopes = tuple(float(2.0 ** (-8.0 * (group * DIL_HEADS + h + 1) / n_heads)) for h in range(DIL_HEADS))

    halo_blocks = 2 * nt
    if dilation == 1:
        assert nt % DIL_UNITS == 0
        n_steps, n_rgroups = nt // DIL_UNITS, 1
        per_step = 2 * DIL_UNITS
        cur_block, halo_block = (None, None, DIL_UNITS * DIL_TQ, DIL_WIDTH), (None, None, DIL_HALO, DIL_WIDTH)
    else:
        assert dilation % DIL_UNITS == 0
        n_steps, n_rgroups = nt, dilation // DIL_UNITS
        per_step = 2
        cur_block, halo_block = (None, DIL_UNITS, DIL_TQ, DIL_WIDTH), (None, DIL_UNITS, DIL_HALO, DIL_WIDTH)

    def cur_spec(kind):
        return pl.BlockSpec(cur_block, lambda b, i, r: (b, r, i, col0 + kind))

    def prev_spec(kind):
        return pl.BlockSpec(halo_block, lambda b, i, r: (b, r, jnp.maximum(per_step * i - 1, 0), col0 + kind))

    def next_spec(kind):
        return pl.BlockSpec(halo_block,
                            lambda b, i, r: (b, r, jnp.minimum(per_step * (i + 1), halo_blocks - 1), col0 + kind))

    span = seq_len // n_steps
    acc_spec = pl.BlockSpec((DIL_HEADS, span, HEAD_DIM), lambda b, i, r: (0, b * n_steps + i, 0))
    ml_spec = pl.BlockSpec((span, HEAD_DIM), lambda b, i, r: (b * n_steps + i, 0))
    y_spec = pl.BlockSpec((span, DIL_WIDTH), lambda b, i, r: (b * n_steps + i, 0))

    in_specs = [cur_spec(0), prev_spec(1), cur_spec(1), next_spec(1), prev_spec(2), cur_spec(2), next_spec(2)]
    args = [qkv] * 7
    has_state = state is not None
    if has_state:
        in_specs += [acc_spec, ml_spec]
        args += list(state)
    if emit_final:
        out_specs = y_spec
        out_shape = jax.ShapeDtypeStruct((rows, DIL_WIDTH), BF16)
    else:
        out_specs = [acc_spec, ml_spec]
        out_shape = [jax.ShapeDtypeStruct((DIL_HEADS, rows, HEAD_DIM), F32),
                     jax.ShapeDtypeStruct((rows, HEAD_DIM), F32)]
    scratch = []
    if dilation > 1:
        scratch = [pltpu.VMEM((DIL_UNITS, DIL_HEADS, DIL_TQ, HEAD_DIM), F32),
                   pltpu.VMEM((DIL_UNITS, DIL_TQ, HEAD_DIM), F32)]
    kern = functools.partial(_dilated_kernel, dilation=dilation, sub_len=sub_len, slopes=slopes,
                             has_state=has_state, emit_final=emit_final)
    est = (4 * span * (DIL_WIDTH + HEAD_DIM) * 4 + 2 * 7 * DIL_UNITS * DIL_TQ * DIL_WIDTH * 2
           + 2 * DIL_UNITS * DIL_TQ * (DIL_WIDTH + HEAD_DIM) * 4)
    return pl.pallas_call(
        kern,
        grid=(n_seq, n_steps, n_rgroups),
        in_specs=in_specs,
        out_specs=out_specs,
        out_shape=out_shape,
        scratch_shapes=scratch,
        compiler_params=pltpu.CompilerParams(
            dimension_semantics=("parallel", "parallel", "arbitrary"), vmem_limit_bytes=_vmem_limit(est)),
        name=f"dilated_g{group}",
    )(*args)


def _dilated_mixer(proj, qkv4, qkv16, n_seq, seq_len):
    state = _dilated_call(qkv16, 0, 2, None, False, n_seq, seq_len)
    state = _dilated_call(qkv4, 0, 1, state, False, n_seq, seq_len)
    qkv1 = proj.reshape(n_seq, 1, seq_len, MAIN_WIDTH)
    return _dilated_call(qkv1, 0, 0, state, True, n_seq, seq_len)


def _rope_tables(seq_len):
    pos = jnp.arange(seq_len)
    row = (pos // GRID_W).astype(F32)
    col = (pos % GRID_W).astype(F32)
    half = HEAD_DIM // 2
    inv_freq = ROPE_THETA ** (-jnp.arange(0, half, 2, dtype=F32) / half)
    ang_r = row[:, None] * inv_freq
    ang_c = col[:, None] * inv_freq
    cos = jnp.concatenate([jnp.cos(ang_r), jnp.cos(ang_r), jnp.cos(ang_c), jnp.cos(ang_c)], axis=1)
    sin = jnp.concatenate([-jnp.sin(ang_r), jnp.sin(ang_r), -jnp.sin(ang_c), jnp.sin(ang_c)], axis=1)
    return cos, sin


def _prep_kernel(q_ref, k_ref, v_ref, cos_ref, sin_ref, gq_ref, gk_ref, qo_ref, ko_ref, vt_ref):
    cos = cos_ref[...]
    sin = sin_ref[...]
    quarter = HEAD_DIM // 4
    lane = lax.broadcasted_iota(jnp.int32, cos.shape, 1)
    upper = (lane & quarter) != 0

    def norm_rope(x, g, scale):
        y = _rms(x.astype(F32), g)
        partner = jnp.where(upper, pltpu.roll(y, quarter, 1), pltpu.roll(y, HEAD_DIM - quarter, 1))
        return (y * cos + partner * sin) * scale

    for h in range(GQA_Q_HEADS):
        hs = slice(h * HEAD_DIM, (h + 1) * HEAD_DIM)
        qo_ref[:, hs] = norm_rope(q_ref[:, hs], gq_ref[...], Q_PRESCALE).astype(qo_ref.dtype)
    for h in range(GQA_KV_HEADS):
        hs = slice(h * HEAD_DIM, (h + 1) * HEAD_DIM)
        ko_ref[:, hs] = norm_rope(k_ref[:, hs], gk_ref[...], 1.0).astype(ko_ref.dtype)
        vt_ref[h, 0] = v_ref[:, hs].astype(F32).T.astype(vt_ref.dtype)


def _qk_prep(proj, cos, sin, g_q, g_k, n_seq, seq_len):
    rows = proj.shape[0]
    tm = FLASH_TK
    tpos = seq_len // tm
    return pl.pallas_call(
        _prep_kernel,
        grid=(rows // tm,),
        in_specs=[
            pl.BlockSpec((tm, GQA_Q_WIDTH), lambda i: (i, COL_BQ // GQA_Q_WIDTH)),
            pl.BlockSpec((tm, GQA_KV_WIDTH), lambda i: (i, COL_BK // GQA_KV_WIDTH)),
            pl.BlockSpec((tm, GQA_KV_WIDTH), lambda i: (i, COL_BV // GQA_KV_WIDTH)),
            pl.BlockSpec((tm, HEAD_DIM), lambda i: (i % tpos, 0)),
            pl.BlockSpec((tm, HEAD_DIM), lambda i: (i % tpos, 0)),
            pl.BlockSpec((1, HEAD_DIM), lambda i: (0, 0)),
            pl.BlockSpec((1, HEAD_DIM), lambda i: (0, 0)),
        ],
        out_specs=[pl.BlockSpec((tm, GQA_Q_WIDTH), lambda i: (i, 0)),
                   pl.BlockSpec((tm, GQA_KV_WIDTH), lambda i: (i, 0)),
                   pl.BlockSpec((None, GQA_KV_HEADS, 1, HEAD_DIM, tm), lambda i: (i // tpos, 0, i % tpos, 0, 0))],
        out_shape=[jax.ShapeDtypeStruct((rows, GQA_Q_WIDTH), BF16),
                   jax.ShapeDtypeStruct((rows, GQA_KV_WIDTH), BF16),
                   jax.ShapeDtypeStruct((n_seq, GQA_KV_HEADS, tpos, HEAD_DIM, tm), BF16)],
        compiler_params=pltpu.CompilerParams(
            dimension_semantics=("parallel",), vmem_limit_bytes=_vmem_limit(16 * MIB)),
        name="qk_prep",
    )(proj, proj, proj, cos, sin, g_q, g_k)


def _flash_kernel(q_ref, k_ref, vt_ref, o_ref, st_sc, mx_sc, m_sc, acc_sc, *, n_chunks):
    tk = FLASH_TK
    m_sc[...] = jnp.full(m_sc.shape, NEG_INF, F32)
    acc_sc[...] = jnp.zeros(acc_sc.shape, F32)
    ones = jnp.ones((FLASH_ONES_ROWS, tk), BF16)

    def scores(c, slot, g):
        k = k_ref[pl.ds(pl.multiple_of(c * tk, tk), tk), :]
        q = q_ref[:, g * HEAD_DIM:(g + 1) * HEAD_DIM]
        st = lax.dot_general(k, q, (((1,), (1,)), ((), ())), preferred_element_type=F32)
        st_sc[slot, g] = st
        mx_sc[slot, g] = jnp.max(st, axis=0, keepdims=True)

    def consume(c, slot, g):
        vt = jnp.concatenate([vt_ref[c], ones], axis=0)
        m_prev = m_sc[g]
        m_new = jnp.maximum(m_prev, mx_sc[slot, g])
        alpha = jnp.exp2(m_prev - m_new)
        p = jnp.exp2(st_sc[slot, g] - m_new).astype(BF16)
        acc_sc[g] = alpha * acc_sc[g] + jnp.dot(vt, p, preferred_element_type=F32)
        m_sc[g] = m_new

    def step(c, slot, has_next):
        for g in range(GQA_GROUP):
            if has_next:
                scores(c + 1, 1 - slot, g)
            consume(c, slot, g)

    for g in range(GQA_GROUP):
        scores(0, 0, g)

    unroll = FLASH_UNROLL

    def group(i, carry):
        for u in range(unroll):
            step(unroll * i + u, u % 2, True)
        return carry

    lax.fori_loop(0, n_chunks // unroll - 1, group, 0)
    for u in range(unroll):
        step(n_chunks - unroll + u, u % 2, u < unroll - 1)
    for g in range(GQA_GROUP):
        acc = acc_sc[g]
        o_t = acc[:HEAD_DIM] / acc[HEAD_DIM:HEAD_DIM + 1]
        o_ref[:, g * HEAD_DIM:(g + 1) * HEAD_DIM] = o_t.T.astype(o_ref.dtype)


def _flash(q, k, vt, n_seq, seq_len):
    rows = q.shape[0]
    tq, tk = FLASH_TQ, FLASH_TK
    nq = seq_len // tq
    n_chunks = seq_len // tk
    gw = GQA_GROUP * HEAD_DIM
    acc_rows = HEAD_DIM + FLASH_ONES_ROWS
    assert FLASH_UNROLL % 2 == 0 and n_chunks % FLASH_UNROLL == 0
    est = (2 * 2 * seq_len * HEAD_DIM * 2 + 4 * tq * gw * 2 + GQA_GROUP * (acc_rows + 8 * 3) * tq * 4
           + (2 * GQA_GROUP + 3) * tk * tq * 4)
    return pl.pallas_call(
        functools.partial(_flash_kernel, n_chunks=n_chunks),
        grid=(n_seq, GQA_KV_HEADS, nq),
        in_specs=[
            pl.BlockSpec((tq, gw), lambda b, h, i: (b * nq + i, h)),
            pl.BlockSpec((seq_len, HEAD_DIM), lambda b, h, i: (b, h)),
            pl.BlockSpec((None, None, n_chunks, HEAD_DIM, tk), lambda b, h, i: (b, h, 0, 0, 0)),
        ],
        out_specs=pl.BlockSpec((tq, gw), lambda b, h, i: (b * nq + i, h)),
        out_shape=jax.ShapeDtypeStruct((rows, GQA_Q_WIDTH), BF16),
        scratch_shapes=[pltpu.VMEM((2, GQA_GROUP, tk, tq), F32),
                        pltpu.VMEM((2, GQA_GROUP, 1, tq), F32),
                        pltpu.VMEM((GQA_GROUP, 1, tq), F32),
                        pltpu.VMEM((GQA_GROUP, acc_rows, tq), F32)],
        compiler_params=pltpu.CompilerParams(
            dimension_semantics=("parallel", "parallel", "arbitrary"), vmem_limit_bytes=_vmem_limit(est)),
        name="gqa_flash",
    )(q, k, vt)


def _sigmoid(x):
    return 1.0 / (1.0 + jnp.exp(-x))


def _merge_kernel(ya_ref, yb_ref, ga0_ref, ga1_ref, gb0_ref, gb1_ref, x_ref, wa_ref, wb_ref, wo_ref, g_ref,
                  x1_ref, h2_ref):
    half = D_MODEL // 2
    ya = ya_ref[...]
    yb = yb_ref[...]
    x1 = x_ref[...]
    for n, (ga_ref, gb_ref) in enumerate(((ga0_ref, gb0_ref), (ga1_ref, gb1_ref))):
        cs = slice(n * half, (n + 1) * half)
        o_a = jnp.dot(ya, wa_ref[:, cs], preferred_element_type=F32)
        o_b = jnp.dot(yb, wb_ref[:, cs], preferred_element_type=F32)
        merged = _sigmoid(ga_ref[...].astype(F32)) * o_a + _sigmoid(gb_ref[...].astype(F32)) * o_b
        x1 = x1 + jnp.dot(merged.astype(BF16), wo_ref[cs, :], preferred_element_type=F32)
    x1_ref[...] = x1
    h2_ref[...] = _rms(x1, g_ref[...]).astype(h2_ref.dtype)


def _merge(y_a, y_b, proj, x, w_a, w_b, w_out, g_mlp):
    rows = x.shape[0]
    tm = MERGE_TM
    half = D_MODEL // 2
    once = pl.Buffered(1)
    est = (2 * D_MODEL * D_MODEL * 2 + 2 * tm * (DIL_WIDTH + GQA_Q_WIDTH) * 2 + 2 * 4 * tm * half * 2
           + 4 * tm * D_MODEL * 4 + 2 * tm * D_MODEL * 2 + 6 * tm * D_MODEL * 4)
    return pl.pallas_call(
        _merge_kernel,
        grid=(rows // tm,),
        in_specs=[
            pl.BlockSpec((tm, DIL_WIDTH), lambda i: (i, 0)),
            pl.BlockSpec((tm, GQA_Q_WIDTH), lambda i: (i, 0)),
            pl.BlockSpec((tm, half), lambda i: (i, COL_GA // half)),
            pl.BlockSpec((tm, half), lambda i: (i, COL_GA // half + 1)),
            pl.BlockSpec((tm, half), lambda i: (i, COL_GB // half)),
            pl.BlockSpec((tm, half), lambda i: (i, COL_GB // half + 1)),
            pl.BlockSpec((tm, D_MODEL), lambda i: (i, 0)),
            pl.BlockSpec((DIL_WIDTH, D_MODEL), lambda i: (0, 0), pipeline_mode=once),
            pl.BlockSpec((GQA_Q_WIDTH, D_MODEL), lambda i: (0, 0), pipeline_mode=once),
            pl.BlockSpec((D_MODEL, D_MODEL), lambda i: (0, 0), pipeline_mode=once),
            pl.BlockSpec((1, D_MODEL), lambda i: (0, 0)),
        ],
        out_specs=[pl.BlockSpec((tm, D_MODEL), lambda i: (i, 0)),
                   pl.BlockSpec((tm, D_MODEL), lambda i: (i, 0))],
        out_shape=[jax.ShapeDtypeStruct((rows, D_MODEL), F32),
                   jax.ShapeDtypeStruct((rows, D_MODEL), BF16)],
        compiler_params=pltpu.CompilerParams(
            dimension_semantics=("parallel",), vmem_limit_bytes=_vmem_limit(est)),
        name="merge_out_proj",
    )(y_a, y_b, proj, proj, proj, proj, x, w_a, w_b, w_out, g_mlp)


def _mlp_kernel(h2_ref, x1_ref, w1_ref, w2_ref, g_ref, o_ref):
    f = pl.program_id(1)

    @pl.when(f == 0)
    def _():
        o_ref[...] = x1_ref[...]

    u = jnp.maximum(jnp.dot(h2_ref[...], w1_ref[...], preferred_element_type=F32), 0.0)
    o_ref[...] += jnp.dot((u * u).astype(BF16), w2_ref[...], preferred_element_type=F32)

    @pl.when(f == pl.num_programs(1) - 1)
    def _():
        o_ref[...] = _rms(o_ref[...], g_ref[...])


def _mlp(h2, x1, w_ff1, w_ff2, g_final):
    rows = h2.shape[0]
    tm, tf = MLP_TM, MLP_TF
    est = 2 * tm * D_MODEL * 2 + 3 * tm * D_MODEL * 4 + 4 * D_MODEL * tf * 2
    return pl.pallas_call(
        _mlp_kernel,
        grid=(rows // tm, D_FF // tf),
        in_specs=[
            pl.BlockSpec((tm, D_MODEL), lambda i, f: (i, 0)),
            pl.BlockSpec((tm, D_MODEL), lambda i, f: (i, 0), pipeline_mode=pl.Buffered(1)),
            pl.BlockSpec((D_MODEL, tf), lambda i, f: (0, f)),
            pl.BlockSpec((tf, D_MODEL), lambda i, f: (f, 0)),
            pl.BlockSpec((1, D_MODEL), lambda i, f: (0, 0)),
        ],
        out_specs=pl.BlockSpec((tm, D_MODEL), lambda i, f: (i, 0)),
        out_shape=jax.ShapeDtypeStruct((rows, D_MODEL), F32),
        compiler_params=pltpu.CompilerParams(
            dimension_semantics=("parallel", "arbitrary"), vmem_limit_bytes=_vmem_limit(est)),
        name="mlp_final_norm",
    )(h2, x1, w_ff1, w_ff2, g_final)


def _trunk(x3, weights, g_final, rope):
    n_seq, seq_len, _ = x3.shape
    x = x3.reshape(n_seq * seq_len, D_MODEL)
    cos, sin = rope
    g_mix, w_main, w_qkv4, w_qkv16, g_q, g_k, w_a, w_b, w_out, g_mlp, w_ff1, w_ff2 = weights
    proj = _in_proj(x, g_mix, w_main)
    qkv4 = _in_proj_dil(x, g_mix, w_qkv4, DIL_GROUPS[1][1], n_seq, seq_len)
    qkv16 = _in_proj_dil(x, g_mix, w_qkv16, DIL_GROUPS[2][1], n_seq, seq_len)
    y_a = _dilated_mixer(proj, qkv4, qkv16, n_seq, seq_len)
    q, k, vt = _qk_prep(proj, cos, sin, g_q, g_k, n_seq, seq_len)
    y_b = _flash(q, k, vt, n_seq, seq_len)
    x1, h2 = _merge(y_a, y_b, proj, x, w_a, w_b, w_out, g_mlp)
    y = _mlp(h2, x1, w_ff1, w_ff2, g_final)
    return y.reshape(n_seq, seq_len, D_MODEL)


def _group_qkv_columns(w, group):
    cols = [w[:, base + group * DIL_WIDTH: base + (group + 1) * DIL_WIDTH] for base in (W_AQ, W_AK, W_AV)]
    return jnp.concatenate(cols, axis=1)


def kernel(x_prompt, x_sample, g_mix, w_in, g_q, g_k, w_branch, w_out, g_mlp, w_ff1, w_ff2, g_final):
    assert w_in.shape[0] == 1, "single-layer stack only"
    w = w_in[0].astype(BF16)
    w_main = jnp.concatenate([_group_qkv_columns(w, 0), w[:, W_REST:]], axis=1)
    wb = w_branch[0].astype(BF16)
    weights = (g_mix[0][None], w_main, _group_qkv_columns(w, 1), _group_qkv_columns(w, 2),
               g_q[0][None], g_k[0][None], wb[:DIL_WIDTH], wb[DIL_WIDTH:], w_out[0].astype(BF16),
               g_mlp[0][None], w_ff1[0].astype(BF16), w_ff2[0].astype(BF16))
    gf = g_final[None]
    outs = []
    for x3 in (x_prompt, x_sample):
        rope = _rope_tables(x3.shape[1])
        outs.append(_trunk(x3, weights, gf, rope))
    return tuple(outs)
```

```python
import functools

import jax
import jax.numpy as jnp
from jax import lax
from jax.experimental import pallas as pl
from jax.experimental.pallas import tpu as pltpu

F32 = jnp.float32
BF16 = jnp.bfloat16

D_MODEL = 2048
HEAD_DIM = 128
DIL_GROUPS = ((128, 1), (512, 4), (2048, 16))
N_DIL_GROUPS = 3
DIL_HEADS = 4
DIL_WIDTH = DIL_HEADS * HEAD_DIM
DIL_QKV_WIDTH = 3 * DIL_WIDTH
GQA_Q_HEADS = 12
GQA_KV_HEADS = 4
GQA_GROUP = GQA_Q_HEADS // GQA_KV_HEADS
GQA_Q_WIDTH = GQA_Q_HEADS * HEAD_DIM
GQA_KV_WIDTH = GQA_KV_HEADS * HEAD_DIM
GRID_W = 64
ROPE_THETA = 10000.0
D_FF = 4 * D_MODEL
W_AQ = 0
W_AK = N_DIL_GROUPS * DIL_WIDTH
W_AV = 2 * N_DIL_GROUPS * DIL_WIDTH
W_REST = 3 * N_DIL_GROUPS * DIL_WIDTH
COL_BQ = DIL_QKV_WIDTH
COL_BK = COL_BQ + GQA_Q_WIDTH
COL_BV = COL_BK + GQA_KV_WIDTH
COL_GA = COL_BV + GQA_KV_WIDTH
COL_GB = COL_GA + D_MODEL
MAIN_WIDTH = COL_GB + D_MODEL
RMS_EPS = 1e-6
NEG_INF = -1e30
ATTN_SCALE = HEAD_DIM ** -0.5
Q_PRESCALE = ATTN_SCALE * 1.4426950408889634

V7X_VMEM_BYTES = 64 * 1024 * 1024
MIB = 1024 * 1024

IN_TM, IN_TN = 1024, 1024
DIL_TQ = 128
DIL_HALO = 64
DIL_UNITS = 4
FLASH_TQ, FLASH_TK = 512, 512
FLASH_ONES_ROWS = 16
FLASH_UNROLL = 4
MERGE_TM = 256
MLP_TM, MLP_TF = 1024, 512


def _vmem_limit(nbytes):
    return int(min(nbytes + 16 * MIB, V7X_VMEM_BYTES - 8 * MIB))


def _rms(x, g):
    return x * lax.rsqrt(jnp.mean(x * x, axis=-1, keepdims=True) + RMS_EPS) * g


def _in_proj_kernel(x_ref, g_ref, w_ref, o_ref, h_ref):
    @pl.when(pl.program_id(1) == 0)
    def _():
        h_ref[...] = _rms(x_ref[...], g_ref[...]).astype(BF16)

    o_ref[...] = jnp.dot(h_ref[...], w_ref[...], preferred_element_type=F32).astype(o_ref.dtype)


def _in_proj(x, g_mix, w_main):
    rows = x.shape[0]
    tm, tn = IN_TM, IN_TN
    est = 2 * tm * D_MODEL * 4 + 2 * tm * D_MODEL * 2 + 2 * D_MODEL * tn * 2 + 2 * tm * tn * 2
    return pl.pallas_call(
        _in_proj_kernel,
        grid=(rows // tm, MAIN_WIDTH // tn),
        in_specs=[
            pl.BlockSpec((tm, D_MODEL), lambda i, j: (i, 0)),
            pl.BlockSpec((1, D_MODEL), lambda i, j: (0, 0)),
            pl.BlockSpec((D_MODEL, tn), lambda i, j: (0, j)),
        ],
        out_specs=[pl.BlockSpec((tm, tn), lambda i, j: (i, j)),
                   pl.BlockSpec((tm, D_MODEL), lambda i, j: (i, 0))],
        out_shape=[jax.ShapeDtypeStruct((rows, MAIN_WIDTH), BF16),
                   jax.ShapeDtypeStruct((rows, D_MODEL), BF16)],
        compiler_params=pltpu.CompilerParams(
            dimension_semantics=("parallel", "arbitrary"), vmem_limit_bytes=_vmem_limit(est)),
        name="in_proj",
    )(x, g_mix, w_main)


def _in_proj_dil_kernel(h_ref, w_ref, o_ref, res_ref, *, dilation):
    res = jnp.dot(h_ref[...], w_ref[...], preferred_element_type=F32)
    strips, tm, lanes = res_ref.shape
    for s in range(strips):
        res_ref[s] = res[:, s * lanes:(s + 1) * lanes]
    sub = tm // dilation
    for r in range(dilation):
        for s in range(strips):
            o_ref[r, :, s * lanes:(s + 1) * lanes] = (
                res_ref[s, pl.ds(r, sub, stride=dilation), :].astype(o_ref.dtype))


def _in_proj_dil(h, w_qkv, dilation, n_seq, seq_len):
    rows = h.shape[0]
    tm, tn = IN_TM, DIL_WIDTH
    tiles_per_seq = seq_len // tm
    est = 2 * tm * D_MODEL * 2 + 2 * D_MODEL * tn * 2 + 2 * tm * tn * 2 + tm * tn * 4
    return pl.pallas_call(
        functools.partial(_in_proj_dil_kernel, dilation=dilation),
        grid=(rows // tm, DIL_QKV_WIDTH // tn),
        in_specs=[
            pl.BlockSpec((tm, D_MODEL), lambda i, j: (i, 0)),
            pl.BlockSpec((D_MODEL, tn), lambda i, j: (0, j)),
        ],
        out_specs=pl.BlockSpec((None, dilation, tm // dilation, tn),
                               lambda i, j: (i // tiles_per_seq, 0, i % tiles_per_seq, j)),
        out_shape=jax.ShapeDtypeStruct((n_seq, dilation, seq_len // dilation, DIL_QKV_WIDTH), BF16),
        scratch_shapes=[pltpu.VMEM((tn // HEAD_DIM, tm, HEAD_DIM), F32)],
        compiler_params=pltpu.CompilerParams(
            dimension_semantics=("parallel", "parallel"), vmem_limit_bytes=_vmem_limit(est)),
        name=f"in_proj_dil{dilation}",
    )(h, w_qkv)


def _dilated_kernel(*refs, dilation, sub_len, slopes, has_state, emit_final):
    q_ref, kp_ref, kc_ref, kn_ref, vp_ref, vc_ref, vn_ref = refs[:7]
    refs = refs[7:]
    if has_state:
        acc_in, ml_in = refs[:2]
        refs = refs[2:]
    if emit_final:
        y_ref = refs[0]
        refs = refs[1:]
    else:
        acc_out, ml_out = refs[:2]
        refs = refs[2:]
    strided = dilation > 1
    if strided:
        acc_t, ml_t = refs

    tq, tk = DIL_TQ, DIL_TQ + 2 * DIL_HALO
    units = range(DIL_UNITS)
    heads = range(DIL_HEADS)
    cols = [slice(h * HEAD_DIM, (h + 1) * HEAD_DIM) for h in heads]
    step = pl.program_id(1)
    n_rgroups = dilation // DIL_UNITS if strided else 1

    def for_residue_group(fn):
        if n_rgroups == 1:
            fn(0)
        else:
            for g in range(n_rgroups):
                pl.when(pl.program_id(2) == g)(functools.partial(fn, g * DIL_UNITS))

    def state_rows(r0, u):
        return pl.ds(r0 + u, tq, stride=dilation)

    if strided and has_state:
        def gather(r0):
            for u in units:
                for h in heads:
                    acc_t[u, h] = acc_in[h, state_rows(r0, u), :]
                ml_t[u] = ml_in[state_rows(r0, u), :]
        for_residue_group(gather)

    def unit_rows(u):
        return slice(u * tq, (u + 1) * tq)

    def prev_acc(u, h):
        return acc_t[u, h] if strided else acc_in[h, unit_rows(u), :]

    def prev_ml(u):
        return ml_t[u] if strided else ml_in[unit_rows(u), :]

    if strided:
        def q_of(u, h):
            return q_ref[u, :, cols[h]]

        def window(p_ref, c_ref, n_ref, u, h):
            return jnp.concatenate([p_ref[u, :, cols[h]], c_ref[u, :, cols[h]], n_ref[u, :, cols[h]]], axis=0)
    else:
        def q_of(u, h):
            return q_ref[unit_rows(u), cols[h]]

        def window(p_ref, c_ref, n_ref, u, h):
            lo, hi = u * tq - DIL_HALO, u * tq + tq + DIL_HALO
            parts = []
            if lo < 0:
                parts.append(p_ref[:, cols[h]])
            parts.append(c_ref[max(lo, 0):min(hi, DIL_UNITS * tq), cols[h]])
            if hi > DIL_UNITS * tq:
                parts.append(n_ref[:, cols[h]])
            return jnp.concatenate(parts, axis=0) if len(parts) > 1 else parts[0]

    row = lax.broadcasted_iota(jnp.int32, (tq, tk), 0)
    col = lax.broadcasted_iota(jnp.int32, (tq, tk), 1)
    rel = col - DIL_HALO - row
    band = jnp.abs(rel) <= DIL_HALO
    dist = (dilation * jnp.abs(rel)).astype(F32)
    alibi = [slopes[h] * dist for h in heads]

    def mask_of(tile):
        key_idx = tile * tq - DIL_HALO + col
        return band & (key_idx >= 0) & (key_idx < sub_len)

    if strided:
        shared_mask = mask_of(step)
        masks = [shared_mask for _ in units]
    else:
        masks = [mask_of(step * DIL_UNITS + u) for u in units]
    lane = lax.broadcasted_iota(jnp.int32, (tq, HEAD_DIM), 1)
    ones = jnp.ones((tk, HEAD_DIM), BF16)

    pairs = [(u, h) for u in units for h in heads]
    scores, m_news, alphas, probs = {}, {}, {}, {}
    for u, h in pairs:
        s = lax.dot_general(q_of(u, h), window(kp_ref, kc_ref, kn_ref, u, h), (((1,), (1,)), ((), ())),
                            preferred_element_type=F32)
        scores[u, h] = jnp.where(masks[u], s - alibi[h], NEG_INF)
    ml_prevs = [prev_ml(u) for u in units] if has_state else None
    for u, h in pairs:
        m_cur = jnp.max(scores[u, h], axis=-1, keepdims=True)
        if has_state:
            m_prev = ml_prevs[u][:, h:h + 1]
            m_news[u, h] = jnp.maximum(m_prev, m_cur)
            alphas[u, h] = jnp.exp(m_prev - m_news[u, h])
        else:
            m_news[u, h] = m_cur
    for u, h in pairs:
        probs[u, h] = jnp.exp(scores[u, h] - m_news[u, h]).astype(BF16)
    ml_news = [jnp.zeros((tq, HEAD_DIM), F32) for _ in units]
    for u, h in pairs:
        v_aug = jnp.concatenate([window(vp_ref, vc_ref, vn_ref, u, h), ones], axis=1)
        both = jnp.dot(probs[u, h], v_aug, preferred_element_type=F32)
        acc, l_new = both[:, :HEAD_DIM], both[:, HEAD_DIM:]
        if has_state:
            l_new = alphas[u, h] * ml_prevs[u][:, DIL_HEADS + h:DIL_HEADS + h + 1] + l_new
            acc = alphas[u, h] * prev_acc(u, h) + acc
        if emit_final:
            y_ref[unit_rows(u), cols[h]] = (acc / l_new).astype(y_ref.dtype)
        elif strided:
            acc_t[u, h] = acc
        else:
            acc_out[h, unit_rows(u), :] = acc
        ml_news[u] = jnp.where(lane == h, m_news[u, h], ml_news[u])
        ml_news[u] = jnp.where(lane == DIL_HEADS + h, l_new, ml_news[u])
    if not emit_final:
        if strided:
            for u in units:
                ml_t[u] = ml_news[u]

            def scatter(r0):
                for u in units:
                    for h in heads:
                        acc_out[h, state_rows(r0, u), :] = acc_t[u, h]
                    ml_out[state_rows(r0, u), :] = ml_t[u]
            for_residue_group(scatter)
        else:
            for u in units:
                ml_out[unit_rows(u), :] = ml_news[u]


def _dilated_call(qkv, col0, group, state, emit_final, n_seq, seq_len):
    window, dilation = DIL_GROUPS[group]
    assert window == 2 * DIL_HALO * dilation
    assert not (emit_final and dilation > 1), "the bf16 output is written in token order only"
    sub_len = seq_len // dilation
    nt = sub_len // DIL_TQ
    rows = n_seq * seq_len
    n_heads = N_DIL_GROUPS * DIL_HEADS
    slopes = tuple(float(2.0 ** (-8.0 * (group * DIL_HEADS + h + 1) / n_heads)) for h in range(DIL_HEADS))

    halo_blocks = 2 * nt
    if dilation == 1:
        assert nt % DIL_UNITS == 0
        n_steps, n_rgroups = nt // DIL_UNITS, 1
        per_step = 2 * DIL_UNITS
        cur_block, halo_block = (None, None, DIL_UNITS * DIL_TQ, DIL_WIDTH), (None, None, DIL_HALO, DIL_WIDTH)
    else:
        assert dilation % DIL_UNITS == 0
        n_steps, n_rgroups = nt, dilation // DIL_UNITS
        per_step = 2
        cur_block, halo_block = (None, DIL_UNITS, DIL_TQ, DIL_WIDTH), (None, DIL_UNITS, DIL_HALO, DIL_WIDTH)

    def cur_spec(kind):
        return pl.BlockSpec(cur_block, lambda b, i, r: (b, r, i, col0 + kind))

    def prev_spec(kind):
        return pl.BlockSpec(halo_block, lambda b, i, r: (b, r, jnp.maximum(per_step * i - 1, 0), col0 + kind))

    def next_spec(kind):
        return pl.BlockSpec(halo_block,
                            lambda b, i, r: (b, r, jnp.minimum(per_step * (i + 1), halo_blocks - 1), col0 + kind))

    span = seq_len // n_steps
    acc_spec = pl.BlockSpec((DIL_HEADS, span, HEAD_DIM), lambda b, i, r: (0, b * n_steps + i, 0))
    ml_spec = pl.BlockSpec((span, HEAD_DIM), lambda b, i, r: (b * n_steps + i, 0))
    y_spec = pl.BlockSpec((span, DIL_WIDTH), lambda b, i, r: (b * n_steps + i, 0))

    in_specs = [cur_spec(0), prev_spec(1), cur_spec(1), next_spec(1), prev_spec(2), cur_spec(2), next_spec(2)]
    args = [qkv] * 7
    has_state = state is not None
    if has_state:
        in_specs += [acc_spec, ml_spec]
        args += list(state)
    if emit_final:
        out_specs = y_spec
        out_shape = jax.ShapeDtypeStruct((rows, DIL_WIDTH), BF16)
    else:
        out_specs = [acc_spec, ml_spec]
        out_shape = [jax.ShapeDtypeStruct((DIL_HEADS, rows, HEAD_DIM), F32),
                     jax.ShapeDtypeStruct((rows, HEAD_DIM), F32)]
    scratch = []
    if dilation > 1:
        scratch = [pltpu.VMEM((DIL_UNITS, DIL_HEADS, DIL_TQ, HEAD_DIM), F32),
                   pltpu.VMEM((DIL_UNITS, DIL_TQ, HEAD_DIM), F32)]
    kern = functools.partial(_dilated_kernel, dilation=dilation, sub_len=sub_len, slopes=slopes,
                             has_state=has_state, emit_final=emit_final)
    est = (4 * span * (DIL_WIDTH + HEAD_DIM) * 4 + 2 * 7 * DIL_UNITS * DIL_TQ * DIL_WIDTH * 2
           + 2 * DIL_UNITS * DIL_TQ * (DIL_WIDTH + HEAD_DIM) * 4)
    return pl.pallas_call(
        kern,
        grid=(n_seq, n_steps, n_rgroups),
        in_specs=in_specs,
        out_specs=out_specs,
        out_shape=out_shape,
        scratch_shapes=scratch,
        compiler_params=pltpu.CompilerParams(
            dimension_semantics=("parallel", "parallel", "arbitrary"), vmem_limit_bytes=_vmem_limit(est)),
        name=f"dilated_g{group}",
    )(*args)


def _dilated_mixer(proj, qkv4, qkv16, n_seq, seq_len):
    state = _dilated_call(qkv16, 0, 2, None, False, n_seq, seq_len)
    state = _dilated_call(qkv4, 0, 1, state, False, n_seq, seq_len)
    qkv1 = proj.reshape(n_seq, 1, seq_len, MAIN_WIDTH)
    return _dilated_call(qkv1, 0, 0, state, True, n_seq, seq_len)


def _rope_tables(seq_len):
    pos = jnp.arange(seq_len)
    row = (pos // GRID_W).astype(F32)
    col = (pos % GRID_W).astype(F32)
    half = HEAD_DIM // 2
    inv_freq = ROPE_THETA ** (-jnp.arange(0, half, 2, dtype=F32) / half)
    ang_r = row[:, None] * inv_freq
    ang_c = col[:, None] * inv_freq
    cos = jnp.concatenate([jnp.cos(ang_r), jnp.cos(ang_r), jnp.cos(ang_c), jnp.cos(ang_c)], axis=1)
    sin = jnp.concatenate([-jnp.sin(ang_r), jnp.sin(ang_r), -jnp.sin(ang_c), jnp.sin(ang_c)], axis=1)
    return cos, sin


def _swap_halves_matrix():
    quarter = HEAD_DIM // 4
    i = jnp.arange(2 * HEAD_DIM)
    partner = jnp.where((i & quarter) != 0, i - quarter, i + quarter)
    return (i[:, None] == partner[None, :]).astype(BF16)


def _prep_kernel(q_ref, k_ref, v_ref, cos_ref, sin_ref, gq_ref, gk_ref, perm_ref, qo_ref, ko_ref, vt_ref):
    cos = cos_ref[...]
    sin = sin_ref[...]
    perm = perm_ref[...]
    quarter = HEAD_DIM // 4
    lane = lax.broadcasted_iota(jnp.int32, (8, HEAD_DIM), 1)
    upper = (lane & quarter) != 0

    def gain_tables(g_ref, scale):
        g = jnp.broadcast_to(g_ref[...], (8, HEAD_DIM))
        g_partner = jnp.where(upper, pltpu.roll(g, quarter, 1), pltpu.roll(g, HEAD_DIM - quarter, 1))
        return cos * (g[:1] * scale), sin * (g_partner[:1] * scale)

    def norm_rope(src_ref, dst_ref, n_heads, tables):
        cos_g, sin_g = tables
        for pair in range(n_heads // 2):
            ps = slice(2 * pair * HEAD_DIM, 2 * (pair + 1) * HEAD_DIM)
            x2 = src_ref[:, ps]
            swapped = jnp.dot(x2, perm, preferred_element_type=F32)
            xf = x2.astype(F32)
            for half in range(2):
                hs = slice(half * HEAD_DIM, (half + 1) * HEAD_DIM)
                x = xf[:, hs]
                inv = lax.rsqrt(jnp.mean(x * x, axis=-1, keepdims=True) + RMS_EPS)
                out = (x * cos_g + swapped[:, hs] * sin_g) * inv
                dst_ref[:, (2 * pair + half) * HEAD_DIM:(2 * pair + half + 1) * HEAD_DIM] = out.astype(dst_ref.dtype)

    norm_rope(q_ref, qo_ref, GQA_Q_HEADS, gain_tables(gq_ref, Q_PRESCALE))
    norm_rope(k_ref, ko_ref, GQA_KV_HEADS, gain_tables(gk_ref, 1.0))
    for h in range(GQA_KV_HEADS):
        hs = slice(h * HEAD_DIM, (h + 1) * HEAD_DIM)
        vt_ref[h, 0] = v_ref[:, hs].astype(F32).T.astype(vt_ref.dtype)


def _qk_prep(proj, cos, sin, g_q, g_k, n_seq, seq_len):
    rows = proj.shape[0]
    tm = FLASH_TK
    tpos = seq_len // tm
    return pl.pallas_call(
        _prep_kernel,
        grid=(rows // tm,),
        in_specs=[
            pl.BlockSpec((tm, GQA_Q_WIDTH), lambda i: (i, COL_BQ // GQA_Q_WIDTH)),
            pl.BlockSpec((tm, GQA_KV_WIDTH), lambda i: (i, COL_BK // GQA_KV_WIDTH)),
            pl.BlockSpec((tm, GQA_KV_WIDTH), lambda i: (i, COL_BV // GQA_KV_WIDTH)),
            pl.BlockSpec((tm, HEAD_DIM), lambda i: (i % tpos, 0)),
            pl.BlockSpec((tm, HEAD_DIM), lambda i: (i % tpos, 0)),
            pl.BlockSpec((1, HEAD_DIM), lambda i: (0, 0)),
            pl.BlockSpec((1, HEAD_DIM), lambda i: (0, 0)),
            pl.BlockSpec((2 * HEAD_DIM, 2 * HEAD_DIM), lambda i: (0, 0)),
        ],
        out_specs=[pl.BlockSpec((tm, GQA_Q_WIDTH), lambda i: (i, 0)),
                   pl.BlockSpec((tm, GQA_KV_WIDTH), lambda i: (i, 0)),
                   pl.BlockSpec((None, GQA_KV_HEADS, 1, HEAD_DIM, tm), lambda i: (i // tpos, 0, i % tpos, 0, 0))],
        out_shape=[jax.ShapeDtypeStruct((rows, GQA_Q_WIDTH), BF16),
                   jax.ShapeDtypeStruct((rows, GQA_KV_WIDTH), BF16),
                   jax.ShapeDtypeStruct((n_seq, GQA_KV_HEADS, tpos, HEAD_DIM, tm), BF16)],
        compiler_params=pltpu.CompilerParams(
            dimension_semantics=("parallel",), vmem_limit_bytes=_vmem_limit(16 * MIB)),
        name="qk_prep",
    )(proj, proj, proj, cos, sin, g_q, g_k, _swap_halves_matrix())


def _flash_kernel(q_ref, k_ref, vt_ref, o_ref, st_sc, mx_sc, m_sc, acc_sc, *, n_chunks):
    tk = FLASH_TK
    m_sc[...] = jnp.full(m_sc.shape, NEG_INF, F32)
    acc_sc[...] = jnp.zeros(acc_sc.shape, F32)
    ones = jnp.ones((FLASH_ONES_ROWS, tk), BF16)

    def scores(c, slot, g):
        k = k_ref[pl.ds(pl.multiple_of(c * tk, tk), tk), :]
        q = q_ref[:, g * HEAD_DIM:(g + 1) * HEAD_DIM]
        st = lax.dot_general(k, q, (((1,), (1,)), ((), ())), preferred_element_type=F32)
        st_sc[slot, g] = st
        mx_sc[slot, g] = jnp.max(st, axis=0, keepdims=True)

    def consume(c, slot, g):
        vt = jnp.concatenate([vt_ref[c], ones], axis=0)
        m_prev = m_sc[g]
        m_new = jnp.maximum(m_prev, mx_sc[slot, g])
        alpha = jnp.exp2(m_prev - m_new)
        p = jnp.exp2(st_sc[slot, g] - m_new).astype(BF16)
        acc_sc[g] = alpha * acc_sc[g] + jnp.dot(vt, p, preferred_element_type=F32)
        m_sc[g] = m_new

    def step(c, slot, has_next):
        for g in range(GQA_GROUP):
            if has_next:
                scores(c + 1, 1 - slot, g)
            consume(c, slot, g)

    for g in range(GQA_GROUP):
        scores(0, 0, g)

    unroll = FLASH_UNROLL

    def group(i, carry):
        for u in range(unroll):
            step(unroll * i + u, u % 2, True)
        return carry

    lax.fori_loop(0, n_chunks // unroll - 1, group, 0)
    for u in range(unroll):
        step(n_chunks - unroll + u, u % 2, u < unroll - 1)
    for g in range(GQA_GROUP):
        acc = acc_sc[g]
        o_t = acc[:HEAD_DIM] / acc[HEAD_DIM:HEAD_DIM + 1]
        o_ref[:, g * HEAD_DIM:(g + 1) * HEAD_DIM] = o_t.T.astype(o_ref.dtype)


def _flash(q, k, vt, n_seq, seq_len):
    rows = q.shape[0]
    tq, tk = FLASH_TQ, FLASH_TK
    nq = seq_len // tq
    n_chunks = seq_len // tk
    gw = GQA_GROUP * HEAD_DIM
    acc_rows = HEAD_DIM + FLASH_ONES_ROWS
    assert FLASH_UNROLL % 2 == 0 and n_chunks % FLASH_UNROLL == 0
    est = (2 * 2 * seq_len * HEAD_DIM * 2 + 4 * tq * gw * 2 + GQA_GROUP * (acc_rows + 8 * 3) * tq * 4
           + (2 * GQA_GROUP + 3) * tk * tq * 4)
    return pl.pallas_call(
        functools.partial(_flash_kernel, n_chunks=n_chunks),
        grid=(n_seq, GQA_KV_HEADS, nq),
        in_specs=[
            pl.BlockSpec((tq, gw), lambda b, h, i: (b * nq + i, h)),
            pl.BlockSpec((seq_len, HEAD_DIM), lambda b, h, i: (b, h)),
            pl.BlockSpec((None, None, n_chunks, HEAD_DIM, tk), lambda b, h, i: (b, h, 0, 0, 0)),
        ],
        out_specs=pl.BlockSpec((tq, gw), lambda b, h, i: (b * nq + i, h)),
        out_shape=jax.ShapeDtypeStruct((rows, GQA_Q_WIDTH), BF16),
        scratch_shapes=[pltpu.VMEM((2, GQA_GROUP, tk, tq), F32),
                        pltpu.VMEM((2, GQA_GROUP, 1, tq), F32),
                        pltpu.VMEM((GQA_GROUP, 1, tq), F32),
                        pltpu.VMEM((GQA_GROUP, acc_rows, tq), F32)],
        compiler_params=pltpu.CompilerParams(
            dimension_semantics=("parallel", "parallel", "arbitrary"), vmem_limit_bytes=_vmem_limit(est)),
        name="gqa_flash",
    )(q, k, vt)


def _sigmoid(x):
    return 1.0 / (1.0 + jnp.exp(-x))


def _merge_kernel(ya_ref, yb_ref, ga0_ref, ga1_ref, gb0_ref, gb1_ref, x_ref, wa_ref, wb_ref, wo_ref, g_ref,
                  x1_ref, h2_ref):
    half = D_MODEL // 2
    ya = ya_ref[...]
    yb = yb_ref[...]
    x1 = x_ref[...]
    for n, (ga_ref, gb_ref) in enumerate(((ga0_ref, gb0_ref), (ga1_ref, gb1_ref))):
        cs = slice(n * half, (n + 1) * half)
        o_a = jnp.dot(ya, wa_ref[:, cs], preferred_element_type=F32)
        o_b = jnp.dot(yb, wb_ref[:, cs], preferred_element_type=F32)
        merged = _sigmoid(ga_ref[...].astype(F32)) * o_a + _sigmoid(gb_ref[...].astype(F32)) * o_b
        x1 = x1 + jnp.dot(merged.astype(BF16), wo_ref[cs, :], preferred_element_type=F32)
    x1_ref[...] = x1
    h2_ref[...] = _rms(x1, g_ref[...]).astype(h2_ref.dtype)


def _merge(y_a, y_b, proj, x, w_a, w_b, w_out, g_mlp):
    rows = x.shape[0]
    tm = MERGE_TM
    half = D_MODEL // 2
    once = pl.Buffered(1)
    est = (2 * D_MODEL * D_MODEL * 2 + 2 * tm * (DIL_WIDTH + GQA_Q_WIDTH) * 2 + 2 * 4 * tm * half * 2
           + 4 * tm * D_MODEL * 4 + 2 * tm * D_MODEL * 2 + 6 * tm * D_MODEL * 4)
    return pl.pallas_call(
        _merge_kernel,
        grid=(rows // tm,),
        in_specs=[
            pl.BlockSpec((tm, DIL_WIDTH), lambda i: (i, 0)),
            pl.BlockSpec((tm, GQA_Q_WIDTH), lambda i: (i, 0)),
            pl.BlockSpec((tm, half), lambda i: (i, COL_GA // half)),
            pl.BlockSpec((tm, half), lambda i: (i, COL_GA // half + 1)),
            pl.BlockSpec((tm, half), lambda i: (i, COL_GB // half)),
            pl.BlockSpec((tm, half), lambda i: (i, COL_GB // half + 1)),
            pl.BlockSpec((tm, D_MODEL), lambda i: (i, 0)),
            pl.BlockSpec((DIL_WIDTH, D_MODEL), lambda i: (0, 0), pipeline_mode=once),
            pl.BlockSpec((GQA_Q_WIDTH, D_MODEL), lambda i: (0, 0), pipeline_mode=once),
            pl.BlockSpec((D_MODEL, D_MODEL), lambda i: (0, 0), pipeline_mode=once),
            pl.BlockSpec((1, D_MODEL), lambda i: (0, 0)),
        ],
        out_specs=[pl.BlockSpec((tm, D_MODEL), lambda i: (i, 0)),
                   pl.BlockSpec((tm, D_MODEL), lambda i: (i, 0))],
        out_shape=[jax.ShapeDtypeStruct((rows, D_MODEL), F32),
                   jax.ShapeDtypeStruct((rows, D_MODEL), BF16)],
        compiler_params=pltpu.CompilerParams(
            dimension_semantics=("parallel",), vmem_limit_bytes=_vmem_limit(est)),
        name="merge_out_proj",
    )(y_a, y_b, proj, proj, proj, proj, x, w_a, w_b, w_out, g_mlp)


def _mlp_kernel(h2_ref, x1_ref, w1_ref, w2_ref, g_ref, o_ref):
    f = pl.program_id(1)

    @pl.when(f == 0)
    def _():
        o_ref[...] = x1_ref[...]

    u = jnp.maximum(jnp.dot(h2_ref[...], w1_ref[...], preferred_element_type=F32), 0.0)
    o_ref[...] += jnp.dot((u * u).astype(BF16), w2_ref[...], preferred_element_type=F32)

    @pl.when(f == pl.num_programs(1) - 1)
    def _():
        o_ref[...] = _rms(o_ref[...], g_ref[...])


def _mlp(h2, x1, w_ff1, w_ff2, g_final):
    rows = h2.shape[0]
    tm, tf = MLP_TM, MLP_TF
    est = 2 * tm * D_MODEL * 2 + 3 * tm * D_MODEL * 4 + 4 * D_MODEL * tf * 2
    return pl.pallas_call(
        _mlp_kernel,
        grid=(rows // tm, D_FF // tf),
        in_specs=[
            pl.BlockSpec((tm, D_MODEL), lambda i, f: (i, 0)),
            pl.BlockSpec((tm, D_MODEL), lambda i, f: (i, 0), pipeline_mode=pl.Buffered(1)),
            pl.BlockSpec((D_MODEL, tf), lambda i, f: (0, f)),
            pl.BlockSpec((tf, D_MODEL), lambda i, f: (f, 0)),
            pl.BlockSpec((1, D_MODEL), lambda i, f: (0, 0)),
        ],
        out_specs=pl.BlockSpec((tm, D_MODEL), lambda i, f: (i, 0)),
        out_shape=jax.ShapeDtypeStruct((rows, D_MODEL), F32),
        compiler_params=pltpu.CompilerParams(
            dimension_semantics=("parallel", "arbitrary"), vmem_limit_bytes=_vmem_limit(est)),
        name="mlp_final_norm",
    )(h2, x1, w_ff1, w_ff2, g_final)


def _trunk(x3, weights, g_final, rope):
    n_seq, seq_len, _ = x3.shape
    x = x3.reshape(n_seq * seq_len, D_MODEL)
    cos, sin = rope
    g_mix, w_main, w_qkv4, w_qkv16, g_q, g_k, w_a, w_b, w_out, g_mlp, w_ff1, w_ff2 = weights
    proj, h = _in_proj(x, g_mix, w_main)
    qkv4 = _in_proj_dil(h, w_qkv4, DIL_GROUPS[1][1], n_seq, seq_len)
    qkv16 = _in_proj_dil(h, w_qkv16, DIL_GROUPS[2][1], n_seq, seq_len)
    y_a = _dilated_mixer(proj, qkv4, qkv16, n_seq, seq_len)
    q, k, vt = _qk_prep(proj, cos, sin, g_q, g_k, n_seq, seq_len)
    y_b = _flash(q, k, vt, n_seq, seq_len)
    x1, h2 = _merge(y_a, y_b, proj, x, w_a, w_b, w_out, g_mlp)
    y = _mlp(h2, x1, w_ff1, w_ff2, g_final)
    return y.reshape(n_seq, seq_len, D_MODEL)


def _group_qkv_columns(w, group):
    cols = [w[:, base + group * DIL_WIDTH: base + (group + 1) * DIL_WIDTH] for base in (W_AQ, W_AK, W_AV)]
    return jnp.concatenate(cols, axis=1)


def kernel(x_prompt, x_sample, g_mix, w_in, g_q, g_k, w_branch, w_out, g_mlp, w_ff1, w_ff2, g_final):
    assert w_in.shape[0] == 1, "single-layer stack only"
    col_scale = jnp.concatenate([jnp.full((W_AK,), ATTN_SCALE, F32), jnp.ones((w_in.shape[2] - W_AK,), F32)])
    w = (w_in[0] * col_scale).astype(BF16)
    w_main = jnp.concatenate([_group_qkv_columns(w, 0), w[:, W_REST:]], axis=1)
    wb = w_branch[0].astype(BF16)
    weights = (g_mix[0][None], w_main, _group_qkv_columns(w, 1), _group_qkv_columns(w, 2),
               g_q[0][None], g_k[0][None], wb[:DIL_WIDTH], wb[DIL_WIDTH:], w_out[0].astype(BF16),
               g_mlp[0][None], w_ff1[0].astype(BF16), w_ff2[0].astype(BF16))
    gf = g_final[None]
    ropes = {}
    outs = []
    for x3 in (x_prompt, x_sample):
        seq_len = x3.shape[1]
        if seq_len not in ropes:
            ropes[seq_len] = _rope_tables(seq_len)
        outs.append(_trunk(x3, weights, gf, ropes[seq_len]))
    return tuple(outs)
```

```python
import functools
import math

import jax
import jax.numpy as jnp
from jax import lax
from jax.experimental import pallas as pl
from jax.experimental.pallas import tpu as pltpu

F32 = jnp.float32
BF16 = jnp.bfloat16

D_MODEL = 2048
HEAD_DIM = 128
DIL_GROUPS = ((128, 1), (512, 4), (2048, 16))
N_DIL_GROUPS = 3
DIL_HEADS = 4
DIL_WIDTH = DIL_HEADS * HEAD_DIM
DIL_QKV_WIDTH = 3 * DIL_WIDTH
GQA_Q_HEADS = 12
GQA_KV_HEADS = 4
GQA_GROUP = GQA_Q_HEADS // GQA_KV_HEADS
GQA_Q_WIDTH = GQA_Q_HEADS * HEAD_DIM
GQA_KV_WIDTH = GQA_KV_HEADS * HEAD_DIM
GRID_W = 64
ROPE_THETA = 10000.0
D_FF = 4 * D_MODEL
W_AQ = 0
W_AK = N_DIL_GROUPS * DIL_WIDTH
W_AV = 2 * N_DIL_GROUPS * DIL_WIDTH
W_REST = 3 * N_DIL_GROUPS * DIL_WIDTH
COL_BQ = DIL_QKV_WIDTH
COL_BK = COL_BQ + GQA_Q_WIDTH
COL_BV = COL_BK + GQA_KV_WIDTH
COL_GA = COL_BV + GQA_KV_WIDTH
COL_GB = COL_GA + D_MODEL
MAIN_WIDTH = COL_GB + D_MODEL
RMS_EPS = 1e-6
NEG_INF = -1e30
ATTN_SCALE = HEAD_DIM ** -0.5
Q_PRESCALE = ATTN_SCALE * 1.4426950408889634

V7X_VMEM_BYTES = 64 * 1024 * 1024
MIB = 1024 * 1024

IN_TM, IN_TN = 1024, 2048
DIL_TQ = 128
DIL_HALO = 64
DIL_UNITS = 4
FLASH_TQ, FLASH_TK = 512, 512
FLASH_ONES_ROWS = 16
FLASH_UNROLL = 8
MERGE_TM = 256
MLP_TM, MLP_TF = 1024, 1024


def _vmem_limit(nbytes):
    return int(min(nbytes + 16 * MIB, V7X_VMEM_BYTES - 8 * MIB))


def _rms(x, g):
    return x * lax.rsqrt(jnp.mean(x * x, axis=-1, keepdims=True) + RMS_EPS) * g


def _in_proj_kernel(x_ref, g_ref, w_ref, o_ref, h_ref):
    @pl.when(pl.program_id(1) == 0)
    def _():
        h_ref[...] = _rms(x_ref[...], g_ref[...]).astype(BF16)

    o_ref[...] = jnp.dot(h_ref[...], w_ref[...], preferred_element_type=F32).astype(o_ref.dtype)


def _in_proj(x, g_mix, w_main):
    rows = x.shape[0]
    tm, tn = IN_TM, IN_TN
    est = 2 * tm * D_MODEL * 4 + 2 * tm * D_MODEL * 2 + 2 * D_MODEL * tn * 2 + 2 * tm * tn * 2
    return pl.pallas_call(
        _in_proj_kernel,
        grid=(rows // tm, MAIN_WIDTH // tn),
        in_specs=[
            pl.BlockSpec((tm, D_MODEL), lambda i, j: (i, 0)),
            pl.BlockSpec((1, D_MODEL), lambda i, j: (0, 0)),
            pl.BlockSpec((D_MODEL, tn), lambda i, j: (0, j)),
        ],
        out_specs=[pl.BlockSpec((tm, tn), lambda i, j: (i, j)),
                   pl.BlockSpec((tm, D_MODEL), lambda i, j: (i, 0))],
        out_shape=[jax.ShapeDtypeStruct((rows, MAIN_WIDTH), BF16),
                   jax.ShapeDtypeStruct((rows, D_MODEL), BF16)],
        compiler_params=pltpu.CompilerParams(
            dimension_semantics=("parallel", "arbitrary"), vmem_limit_bytes=_vmem_limit(est)),
        name="in_proj",
    )(x, g_mix, w_main)


def _in_proj_dil_kernel(h_ref, w_ref, o_ref, res_ref, *, dilation):
    res = jnp.dot(h_ref[...], w_ref[...], preferred_element_type=F32)
    strips, tm, lanes = res_ref.shape
    for s in range(strips):
        res_ref[s] = res[:, s * lanes:(s + 1) * lanes]
    sub = tm // dilation
    for r in range(dilation):
        for s in range(strips):
            o_ref[r, :, s * lanes:(s + 1) * lanes] = (
                res_ref[s, pl.ds(r, sub, stride=dilation), :].astype(o_ref.dtype))


def _in_proj_dil(h, w_qkv, dilation, n_seq, seq_len):
    rows = h.shape[0]
    tm, tn = IN_TM, DIL_WIDTH
    tiles_per_seq = seq_len // tm
    est = 2 * tm * D_MODEL * 2 + 2 * D_MODEL * tn * 2 + 2 * tm * tn * 2 + tm * tn * 4
    return pl.pallas_call(
        functools.partial(_in_proj_dil_kernel, dilation=dilation),
        grid=(rows // tm, DIL_QKV_WIDTH // tn),
        in_specs=[
            pl.BlockSpec((tm, D_MODEL), lambda i, j: (i, 0)),
            pl.BlockSpec((D_MODEL, tn), lambda i, j: (0, j)),
        ],
        out_specs=pl.BlockSpec((None, dilation, tm // dilation, tn),
                               lambda i, j: (i // tiles_per_seq, 0, i % tiles_per_seq, j)),
        out_shape=jax.ShapeDtypeStruct((n_seq, dilation, seq_len // dilation, DIL_QKV_WIDTH), BF16),
        scratch_shapes=[pltpu.VMEM((tn // HEAD_DIM, tm, HEAD_DIM), F32)],
        compiler_params=pltpu.CompilerParams(
            dimension_semantics=("parallel", "parallel"), vmem_limit_bytes=_vmem_limit(est)),
        name=f"in_proj_dil{dilation}",
    )(h, w_qkv)


def _dilated_kernel(*refs, dilation, sub_len, slopes, has_state, emit_final):
    q_ref, kp_ref, kc_ref, kn_ref, vp_ref, vc_ref, vn_ref = refs[:7]
    refs = refs[7:]
    if has_state:
        acc_in, ml_in = refs[:2]
        refs = refs[2:]
    if emit_final:
        y_ref = refs[0]
        refs = refs[1:]
    else:
        acc_out, ml_out = refs[:2]
        refs = refs[2:]
    strided = dilation > 1
    if strided:
        acc_t, ml_t = refs

    tq, tk = DIL_TQ, DIL_TQ + 2 * DIL_HALO
    units = range(DIL_UNITS)
    heads = range(DIL_HEADS)
    cols = [slice(h * HEAD_DIM, (h + 1) * HEAD_DIM) for h in heads]
    step = pl.program_id(1)
    n_rgroups = dilation // DIL_UNITS if strided else 1

    def for_residue_group(fn):
        if n_rgroups == 1:
            fn(0)
        else:
            for g in range(n_rgroups):
                pl.when(pl.program_id(2) == g)(functools.partial(fn, g * DIL_UNITS))

    def state_rows(r0, u):
        return pl.ds(r0 + u, tq, stride=dilation)

    if strided and has_state:
        def gather(r0):
            for u in units:
                for h in heads:
                    acc_t[u, h] = acc_in[h, state_rows(r0, u), :]
                ml_t[u] = ml_in[state_rows(r0, u), :]
        for_residue_group(gather)

    def unit_rows(u):
        return slice(u * tq, (u + 1) * tq)

    def prev_acc(u, h):
        return acc_t[u, h] if strided else acc_in[h, unit_rows(u), :]

    def prev_ml(u):
        return ml_t[u] if strided else ml_in[unit_rows(u), :]

    if strided:
        def q_of(u, h):
            return q_ref[u, :, cols[h]]

        def window(p_ref, c_ref, n_ref, u, h):
            return jnp.concatenate([p_ref[u, :, cols[h]], c_ref[u, :, cols[h]], n_ref[u, :, cols[h]]], axis=0)
    else:
        def q_of(u, h):
            return q_ref[unit_rows(u), cols[h]]

        def window(p_ref, c_ref, n_ref, u, h):
            lo, hi = u * tq - DIL_HALO, u * tq + tq + DIL_HALO
            parts = []
            if lo < 0:
                parts.append(p_ref[:, cols[h]])
            parts.append(c_ref[max(lo, 0):min(hi, DIL_UNITS * tq), cols[h]])
            if hi > DIL_UNITS * tq:
                parts.append(n_ref[:, cols[h]])
            return jnp.concatenate(parts, axis=0) if len(parts) > 1 else parts[0]

    row = lax.broadcasted_iota(jnp.int32, (tq, tk), 0)
    col = lax.broadcasted_iota(jnp.int32, (tq, tk), 1)
    rel = col - DIL_HALO - row
    band = jnp.abs(rel) <= DIL_HALO
    dist = (dilation * jnp.abs(rel)).astype(F32)
    alibi = [slopes[h] * dist for h in heads]

    def mask_of(tile):
        key_idx = tile * tq - DIL_HALO + col
        return band & (key_idx >= 0) & (key_idx < sub_len)

    if strided:
        shared_mask = mask_of(step)
        masks = [shared_mask for _ in units]
    else:
        masks = [mask_of(step * DIL_UNITS + u) for u in units]
    lane = lax.broadcasted_iota(jnp.int32, (tq, HEAD_DIM), 1)
    ones = jnp.ones((tk, HEAD_DIM), BF16)

    pairs = [(u, h) for u in units for h in heads]
    scores, m_news, alphas, probs = {}, {}, {}, {}
    for u, h in pairs:
        s = lax.dot_general(q_of(u, h), window(kp_ref, kc_ref, kn_ref, u, h), (((1,), (1,)), ((), ())),
                            preferred_element_type=F32)
        scores[u, h] = jnp.where(masks[u], s - alibi[h], NEG_INF)
    ml_prevs = [prev_ml(u) for u in units] if has_state else None
    for u, h in pairs:
        m_cur = jnp.max(scores[u, h], axis=-1, keepdims=True)
        if has_state:
            m_prev = ml_prevs[u][:, h:h + 1]
            m_news[u, h] = jnp.maximum(m_prev, m_cur)
            alphas[u, h] = jnp.exp(m_prev - m_news[u, h])
        else:
            m_news[u, h] = m_cur
    for u, h in pairs:
        probs[u, h] = jnp.exp(scores[u, h] - m_news[u, h]).astype(BF16)
    ml_news = [jnp.zeros((tq, HEAD_DIM), F32) for _ in units]
    for u, h in pairs:
        v_aug = jnp.concatenate([window(vp_ref, vc_ref, vn_ref, u, h), ones], axis=1)
        both = jnp.dot(probs[u, h], v_aug, preferred_element_type=F32)
        acc, l_new = both[:, :HEAD_DIM], both[:, HEAD_DIM:]
        if has_state:
            l_new = alphas[u, h] * ml_prevs[u][:, DIL_HEADS + h:DIL_HEADS + h + 1] + l_new
            acc = alphas[u, h] * prev_acc(u, h) + acc
        if emit_final:
            y_ref[unit_rows(u), cols[h]] = (acc / l_new).astype(y_ref.dtype)
        elif strided:
            acc_t[u, h] = acc
        else:
            acc_out[h, unit_rows(u), :] = acc
        ml_news[u] = jnp.where(lane == h, m_news[u, h], ml_news[u])
        ml_news[u] = jnp.where(lane == DIL_HEADS + h, l_new, ml_news[u])
    if not emit_final:
        if strided:
            for u in units:
                ml_t[u] = ml_news[u]

            def scatter(r0):
                for u in units:
                    for h in heads:
                        acc_out[h, state_rows(r0, u), :] = acc_t[u, h]
                    ml_out[state_rows(r0, u), :] = ml_t[u]
            for_residue_group(scatter)
        else:
            for u in units:
                ml_out[unit_rows(u), :] = ml_news[u]


def _dilated_call(qkv, col0, group, state, emit_final, n_seq, seq_len):
    window, dilation = DIL_GROUPS[group]
    assert window == 2 * DIL_HALO * dilation
    assert not (emit_final and dilation > 1), "the bf16 output is written in token order only"
    sub_len = seq_len // dilation
    nt = sub_len // DIL_TQ
    rows = n_seq * seq_len
    n_heads = N_DIL_GROUPS * DIL_HEADS
    slopes = tuple(float(2.0 ** (-8.0 * (group * DIL_HEADS + h + 1) / n_heads)) for h in range(DIL_HEADS))

    halo_blocks = 2 * nt
    if dilation == 1:
        assert nt % DIL_UNITS == 0
        n_steps, n_rgroups = nt // DIL_UNITS, 1
        per_step = 2 * DIL_UNITS
        cur_block, halo_block = (None, None, DIL_UNITS * DIL_TQ, DIL_WIDTH), (None, None, DIL_HALO, DIL_WIDTH)
    else:
        assert dilation % DIL_UNITS == 0
        n_steps, n_rgroups = nt, dilation // DIL_UNITS
        per_step = 2
        cur_block, halo_block = (None, DIL_UNITS, DIL_TQ, DIL_WIDTH), (None, DIL_UNITS, DIL_HALO, DIL_WIDTH)

    def cur_spec(kind):
        return pl.BlockSpec(cur_block, lambda b, i, r: (b, r, i, col0 + kind))

    def prev_spec(kind):
        return pl.BlockSpec(halo_block, lambda b, i, r: (b, r, jnp.maximum(per_step * i - 1, 0), col0 + kind))

    def next_spec(kind):
        return pl.BlockSpec(halo_block,
                            lambda b, i, r: (b, r, jnp.minimum(per_step * (i + 1), halo_blocks - 1), col0 + kind))

    span = seq_len // n_steps
    acc_spec = pl.BlockSpec((DIL_HEADS, span, HEAD_DIM), lambda b, i, r: (0, b * n_steps + i, 0))
    ml_spec = pl.BlockSpec((span, HEAD_DIM), lambda b, i, r: (b * n_steps + i, 0))
    y_spec = pl.BlockSpec((span, DIL_WIDTH), lambda b, i, r: (b * n_steps + i, 0))

    in_specs = [cur_spec(0), prev_spec(1), cur_spec(1), next_spec(1), prev_spec(2), cur_spec(2), next_spec(2)]
    args = [qkv] * 7
    has_state = state is not None
    if has_state:
        in_specs += [acc_spec, ml_spec]
        args += list(state)
    if emit_final:
        out_specs = y_spec
        out_shape = jax.ShapeDtypeStruct((rows, DIL_WIDTH), BF16)
    else:
        out_specs = [acc_spec, ml_spec]
        out_shape = [jax.ShapeDtypeStruct((DIL_HEADS, rows, HEAD_DIM), F32),
                     jax.ShapeDtypeStruct((rows, HEAD_DIM), F32)]
    scratch = []
    if dilation > 1:
        scratch = [pltpu.VMEM((DIL_UNITS, DIL_HEADS, DIL_TQ, HEAD_DIM), F32),
                   pltpu.VMEM((DIL_UNITS, DIL_TQ, HEAD_DIM), F32)]
    kern = functools.partial(_dilated_kernel, dilation=dilation, sub_len=sub_len, slopes=slopes,
                             has_state=has_state, emit_final=emit_final)
    est = (4 * span * (DIL_WIDTH + HEAD_DIM) * 4 + 2 * 7 * DIL_UNITS * DIL_TQ * DIL_WIDTH * 2
           + 2 * DIL_UNITS * DIL_TQ * (DIL_WIDTH + HEAD_DIM) * 4)
    return pl.pallas_call(
        kern,
        grid=(n_seq, n_steps, n_rgroups),
        in_specs=in_specs,
        out_specs=out_specs,
        out_shape=out_shape,
        scratch_shapes=scratch,
        compiler_params=pltpu.CompilerParams(
            dimension_semantics=("parallel", "parallel", "arbitrary"), vmem_limit_bytes=_vmem_limit(est)),
        name=f"dilated_g{group}",
    )(*args)


def _dilated_mixer(proj, qkv4, qkv16, n_seq, seq_len):
    state = _dilated_call(qkv16, 0, 2, None, False, n_seq, seq_len)
    state = _dilated_call(qkv4, 0, 1, state, False, n_seq, seq_len)
    qkv1 = proj.reshape(n_seq, 1, seq_len, MAIN_WIDTH)
    return _dilated_call(qkv1, 0, 0, state, True, n_seq, seq_len)


def _rope_tables(seq_len):
    pos = jnp.arange(seq_len)
    row = (pos // GRID_W).astype(F32)
    col = (pos % GRID_W).astype(F32)
    half = HEAD_DIM // 2
    inv_freq = ROPE_THETA ** (-jnp.arange(0, half, 2, dtype=F32) / half)
    ang_r = row[:, None] * inv_freq
    ang_c = col[:, None] * inv_freq
    cos = jnp.concatenate([jnp.cos(ang_r), jnp.cos(ang_r), jnp.cos(ang_c), jnp.cos(ang_c)], axis=1)
    sin = jnp.concatenate([-jnp.sin(ang_r), jnp.sin(ang_r), -jnp.sin(ang_c), jnp.sin(ang_c)], axis=1)
    return cos, sin


def _swap_halves_matrix():
    quarter = HEAD_DIM // 4
    i = jnp.arange(2 * HEAD_DIM)
    partner = jnp.where((i & quarter) != 0, i - quarter, i + quarter)
    return (i[:, None] == partner[None, :]).astype(BF16)


def _prep_kernel(q_ref, k_ref, v_ref, cos_ref, sin_ref, gq_ref, gk_ref, perm_ref, qo_ref, ko_ref, vt_ref):
    cos = cos_ref[...]
    sin = sin_ref[...]
    perm = perm_ref[...]
    quarter = HEAD_DIM // 4
    lane = lax.broadcasted_iota(jnp.int32, (8, HEAD_DIM), 1)
    upper = (lane & quarter) != 0

    def gain_tables(g_ref, scale):
        g = jnp.broadcast_to(g_ref[...], (8, HEAD_DIM))
        g_partner = jnp.where(upper, pltpu.roll(g, quarter, 1), pltpu.roll(g, HEAD_DIM - quarter, 1))
        return cos * (g[:1] * scale), sin * (g_partner[:1] * scale)

    def norm_rope(src_ref, dst_ref, n_heads, tables):
        cos_g, sin_g = tables
        for pair in range(n_heads // 2):
            ps = slice(2 * pair * HEAD_DIM, 2 * (pair + 1) * HEAD_DIM)
            x2 = src_ref[:, ps]
            swapped = jnp.dot(x2, perm, preferred_element_type=F32)
            xf = x2.astype(F32)
            for half in range(2):
                hs = slice(half * HEAD_DIM, (half + 1) * HEAD_DIM)
                x = xf[:, hs]
                inv = lax.rsqrt(jnp.mean(x * x, axis=-1, keepdims=True) + RMS_EPS)
                out = (x * cos_g + swapped[:, hs] * sin_g) * inv
                dst_ref[:, (2 * pair + half) * HEAD_DIM:(2 * pair + half + 1) * HEAD_DIM] = out.astype(dst_ref.dtype)

    norm_rope(q_ref, qo_ref, GQA_Q_HEADS, gain_tables(gq_ref, Q_PRESCALE))
    norm_rope(k_ref, ko_ref, GQA_KV_HEADS, gain_tables(gk_ref, 1.0))
    for h in range(GQA_KV_HEADS):
        hs = slice(h * HEAD_DIM, (h + 1) * HEAD_DIM)
        vt_ref[h, 0] = v_ref[:, hs].astype(F32).T.astype(vt_ref.dtype)


def _qk_prep(proj, cos, sin, g_q, g_k, n_seq, seq_len):
    rows = proj.shape[0]
    tm = FLASH_TK
    tpos = seq_len // tm
    return pl.pallas_call(
        _prep_kernel,
        grid=(rows // tm,),
        in_specs=[
            pl.BlockSpec((tm, GQA_Q_WIDTH), lambda i: (i, COL_BQ // GQA_Q_WIDTH)),
            pl.BlockSpec((tm, GQA_KV_WIDTH), lambda i: (i, COL_BK // GQA_KV_WIDTH)),
            pl.BlockSpec((tm, GQA_KV_WIDTH), lambda i: (i, COL_BV // GQA_KV_WIDTH)),
            pl.BlockSpec((tm, HEAD_DIM), lambda i: (i % tpos, 0)),
            pl.BlockSpec((tm, HEAD_DIM), lambda i: (i % tpos, 0)),
            pl.BlockSpec((1, HEAD_DIM), lambda i: (0, 0)),
            pl.BlockSpec((1, HEAD_DIM), lambda i: (0, 0)),
            pl.BlockSpec((2 * HEAD_DIM, 2 * HEAD_DIM), lambda i: (0, 0)),
        ],
        out_specs=[pl.BlockSpec((tm, GQA_Q_WIDTH), lambda i: (i, 0)),
                   pl.BlockSpec((tm, GQA_KV_WIDTH), lambda i: (i, 0)),
                   pl.BlockSpec((None, GQA_KV_HEADS, 1, HEAD_DIM, tm), lambda i: (i // tpos, 0, i % tpos, 0, 0))],
        out_shape=[jax.ShapeDtypeStruct((rows, GQA_Q_WIDTH), BF16),
                   jax.ShapeDtypeStruct((rows, GQA_KV_WIDTH), BF16),
                   jax.ShapeDtypeStruct((n_seq, GQA_KV_HEADS, tpos, HEAD_DIM, tm), BF16)],
        compiler_params=pltpu.CompilerParams(
            dimension_semantics=("parallel",), vmem_limit_bytes=_vmem_limit(16 * MIB)),
        name="qk_prep",
    )(proj, proj, proj, cos, sin, g_q, g_k, _swap_halves_matrix())


def _flash_kernel(q_ref, k_ref, vt_ref, o_ref, st_sc, mx_sc, m_sc, acc_sc, *, n_chunks):
    tk = FLASH_TK
    m_sc[...] = jnp.full(m_sc.shape, NEG_INF, F32)
    acc_sc[...] = jnp.zeros(acc_sc.shape, F32)
    ones = jnp.ones((FLASH_ONES_ROWS, tk), BF16)

    def scores(c, slot, g):
        k = k_ref[pl.ds(pl.multiple_of(c * tk, tk), tk), :]
        q = q_ref[:, g * HEAD_DIM:(g + 1) * HEAD_DIM]
        st = lax.dot_general(k, q, (((1,), (1,)), ((), ())), preferred_element_type=F32)
        st_sc[slot, g] = st
        mx_sc[slot, g] = jnp.max(st, axis=0, keepdims=True)

    def consume(c, slot, g):
        vt = jnp.concatenate([vt_ref[c], ones], axis=0)
        m_prev = m_sc[g]
        m_new = jnp.maximum(m_prev, mx_sc[slot, g])
        alpha = jnp.exp2(m_prev - m_new)
        p = jnp.exp2(st_sc[slot, g] - m_new).astype(BF16)
        acc_sc[g] = alpha * acc_sc[g] + jnp.dot(vt, p, preferred_element_type=F32)
        m_sc[g] = m_new

    def step(c, slot, has_next):
        for g in range(GQA_GROUP):
            if has_next:
                scores(c + 1, 1 - slot, g)
            consume(c, slot, g)

    for g in range(GQA_GROUP):
        scores(0, 0, g)

    unroll = math.gcd(FLASH_UNROLL, n_chunks)

    def group(i, carry):
        for u in range(unroll):
            step(unroll * i + u, u % 2, True)
        return carry

    lax.fori_loop(0, n_chunks // unroll - 1, group, 0)
    for u in range(unroll):
        step(n_chunks - unroll + u, u % 2, u < unroll - 1)
    for g in range(GQA_GROUP):
        acc = acc_sc[g]
        o_t = acc[:HEAD_DIM] / acc[HEAD_DIM:HEAD_DIM + 1]
        o_ref[:, g * HEAD_DIM:(g + 1) * HEAD_DIM] = o_t.T.astype(o_ref.dtype)


def _flash(q, k, vt, n_seq, seq_len):
    rows = q.shape[0]
    tq, tk = FLASH_TQ, FLASH_TK
    nq = seq_len // tq
    n_chunks = seq_len // tk
    gw = GQA_GROUP * HEAD_DIM
    acc_rows = HEAD_DIM + FLASH_ONES_ROWS
    assert math.gcd(FLASH_UNROLL, n_chunks) % 2 == 0
    est = (2 * 2 * seq_len * HEAD_DIM * 2 + 4 * tq * gw * 2 + GQA_GROUP * (acc_rows + 8 * 3) * tq * 4
           + (2 * GQA_GROUP + 3) * tk * tq * 4)
    return pl.pallas_call(
        functools.partial(_flash_kernel, n_chunks=n_chunks),
        grid=(n_seq, GQA_KV_HEADS, nq),
        in_specs=[
            pl.BlockSpec((tq, gw), lambda b, h, i: (b * nq + i, h)),
            pl.BlockSpec((seq_len, HEAD_DIM), lambda b, h, i: (b, h)),
            pl.BlockSpec((None, None, n_chunks, HEAD_DIM, tk), lambda b, h, i: (b, h, 0, 0, 0)),
        ],
        out_specs=pl.BlockSpec((tq, gw), lambda b, h, i: (b * nq + i, h)),
        out_shape=jax.ShapeDtypeStruct((rows, GQA_Q_WIDTH), BF16),
        scratch_shapes=[pltpu.VMEM((2, GQA_GROUP, tk, tq), F32),
                        pltpu.VMEM((2, GQA_GROUP, 1, tq), F32),
                        pltpu.VMEM((GQA_GROUP, 1, tq), F32),
                        pltpu.VMEM((GQA_GROUP, acc_rows, tq), F32)],
        compiler_params=pltpu.CompilerParams(
            dimension_semantics=("parallel", "parallel", "arbitrary"), vmem_limit_bytes=_vmem_limit(est)),
        name="gqa_flash",
    )(q, k, vt)


def _sigmoid(x):
    return 1.0 / (1.0 + jnp.exp(-x))


def _merge_kernel(ya_ref, yb_ref, ga0_ref, ga1_ref, gb0_ref, gb1_ref, x_ref, wa_ref, wb_ref, wo_ref, g_ref,
                  x1_ref, h2_ref):
    half = D_MODEL // 2
    ya = ya_ref[...]
    yb = yb_ref[...]
    x1 = x_ref[...]
    for n, (ga_ref, gb_ref) in enumerate(((ga0_ref, gb0_ref), (ga1_ref, gb1_ref))):
        cs = slice(n * half, (n + 1) * half)
        o_a = jnp.dot(ya, wa_ref[:, cs], preferred_element_type=F32)
        o_b = jnp.dot(yb, wb_ref[:, cs], preferred_element_type=F32)
        merged = _sigmoid(ga_ref[...].astype(F32)) * o_a + _sigmoid(gb_ref[...].astype(F32)) * o_b
        x1 = x1 + jnp.dot(merged.astype(BF16), wo_ref[cs, :], preferred_element_type=F32)
    x1_ref[...] = x1
    h2_ref[...] = _rms(x1, g_ref[...]).astype(h2_ref.dtype)


def _merge(y_a, y_b, proj, x, w_a, w_b, w_out, g_mlp):
    rows = x.shape[0]
    tm = MERGE_TM
    half = D_MODEL // 2
    once = pl.Buffered(1)
    est = (2 * D_MODEL * D_MODEL * 2 + 2 * tm * (DIL_WIDTH + GQA_Q_WIDTH) * 2 + 2 * 4 * tm * half * 2
           + 4 * tm * D_MODEL * 4 + 2 * tm * D_MODEL * 2 + 6 * tm * D_MODEL * 4)
    return pl.pallas_call(
        _merge_kernel,
        grid=(rows // tm,),
        in_specs=[
            pl.BlockSpec((tm, DIL_WIDTH), lambda i: (i, 0)),
            pl.BlockSpec((tm, GQA_Q_WIDTH), lambda i: (i, 0)),
            pl.BlockSpec((tm, half), lambda i: (i, COL_GA // half)),
            pl.BlockSpec((tm, half), lambda i: (i, COL_GA // half + 1)),
            pl.BlockSpec((tm, half), lambda i: (i, COL_GB // half)),
            pl.BlockSpec((tm, half), lambda i: (i, COL_GB // half + 1)),
            pl.BlockSpec((tm, D_MODEL), lambda i: (i, 0)),
            pl.BlockSpec((DIL_WIDTH, D_MODEL), lambda i: (0, 0), pipeline_mode=once),
            pl.BlockSpec((GQA_Q_WIDTH, D_MODEL), lambda i: (0, 0), pipeline_mode=once),
            pl.BlockSpec((D_MODEL, D_MODEL), lambda i: (0, 0), pipeline_mode=once),
            pl.BlockSpec((1, D_MODEL), lambda i: (0, 0)),
        ],
        out_specs=[pl.BlockSpec((tm, D_MODEL), lambda i: (i, 0)),
                   pl.BlockSpec((tm, D_MODEL), lambda i: (i, 0))],
        out_shape=[jax.ShapeDtypeStruct((rows, D_MODEL), F32),
                   jax.ShapeDtypeStruct((rows, D_MODEL), BF16)],
        compiler_params=pltpu.CompilerParams(
            dimension_semantics=("parallel",), vmem_limit_bytes=_vmem_limit(est)),
        name="merge_out_proj",
    )(y_a, y_b, proj, proj, proj, proj, x, w_a, w_b, w_out, g_mlp)


def _mlp_kernel(h2_ref, x1_ref, w1_ref, w2_ref, g_ref, o_ref):
    f = pl.program_id(1)

    @pl.when(f == 0)
    def _():
        o_ref[...] = x1_ref[...]

    u = jnp.maximum(jnp.dot(h2_ref[...], w1_ref[...], preferred_element_type=F32), 0.0)
    o_ref[...] += jnp.dot((u * u).astype(BF16), w2_ref[...], preferred_element_type=F32)

    @pl.when(f == pl.num_programs(1) - 1)
    def _():
        o_ref[...] = _rms(o_ref[...], g_ref[...])


def _mlp(h2, x1, w_ff1, w_ff2, g_final):
    rows = h2.shape[0]
    tm, tf = MLP_TM, MLP_TF
    est = 2 * tm * D_MODEL * 2 + 3 * tm * D_MODEL * 4 + 4 * D_MODEL * tf * 2
    return pl.pallas_call(
        _mlp_kernel,
        grid=(rows // tm, D_FF // tf),
        in_specs=[
            pl.BlockSpec((tm, D_MODEL), lambda i, f: (i, 0)),
            pl.BlockSpec((tm, D_MODEL), lambda i, f: (i, 0), pipeline_mode=pl.Buffered(1)),
            pl.BlockSpec((D_MODEL, tf), lambda i, f: (0, f)),
            pl.BlockSpec((tf, D_MODEL), lambda i, f: (f, 0)),
            pl.BlockSpec((1, D_MODEL), lambda i, f: (0, 0)),
        ],
        out_specs=pl.BlockSpec((tm, D_MODEL), lambda i, f: (i, 0)),
        out_shape=jax.ShapeDtypeStruct((rows, D_MODEL), F32),
        compiler_params=pltpu.CompilerParams(
            dimension_semantics=("parallel", "arbitrary"), vmem_limit_bytes=_vmem_limit(est)),
        name="mlp_final_norm",
    )(h2, x1, w_ff1, w_ff2, g_final)


def _trunk(x3, weights, g_final, rope):
    n_seq, seq_len, _ = x3.shape
    x = x3.reshape(n_seq * seq_len, D_MODEL)
    cos, sin = rope
    g_mix, w_main, w_qkv4, w_qkv16, g_q, g_k, w_a, w_b, w_out, g_mlp, w_ff1, w_ff2 = weights
    proj, h = _in_proj(x, g_mix, w_main)
    qkv4 = _in_proj_dil(h, w_qkv4, DIL_GROUPS[1][1], n_seq, seq_len)
    qkv16 = _in_proj_dil(h, w_qkv16, DIL_GROUPS[2][1], n_seq, seq_len)
    y_a = _dilated_mixer(proj, qkv4, qkv16, n_seq, seq_len)
    q, k, vt = _qk_prep(proj, cos, sin, g_q, g_k, n_seq, seq_len)
    y_b = _flash(q, k, vt, n_seq, seq_len)
    x1, h2 = _merge(y_a, y_b, proj, x, w_a, w_b, w_out, g_mlp)
    y = _mlp(h2, x1, w_ff1, w_ff2, g_final)
    return y.reshape(n_seq, seq_len, D_MODEL)


def _group_qkv_columns(w, group):
    cols = [w[:, base + group * DIL_WIDTH: base + (group + 1) * DIL_WIDTH] for base in (W_AQ, W_AK, W_AV)]
    return jnp.concatenate(cols, axis=1)


def kernel(x_prompt, x_sample, g_mix, w_in, g_q, g_k, w_branch, w_out, g_mlp, w_ff1, w_ff2, g_final):
    assert w_in.shape[0] == 1, "single-layer stack only"
    col_scale = jnp.concatenate([jnp.full((W_AK,), ATTN_SCALE, F32), jnp.ones((w_in.shape[2] - W_AK,), F32)])
    w = (w_in[0] * col_scale).astype(BF16)
    w_main = jnp.concatenate([_group_qkv_columns(w, 0), w[:, W_REST:]], axis=1)
    wb = w_branch[0].astype(BF16)
    weights = (g_mix[0][None], w_main, _group_qkv_columns(w, 1), _group_qkv_columns(w, 2),
               g_q[0][None], g_k[0][None], wb[:DIL_WIDTH], wb[DIL_WIDTH:], w_out[0].astype(BF16),
               g_mlp[0][None], w_ff1[0].astype(BF16), w_ff2[0].astype(BF16))
    gf = g_final[None]
    ropes = {}
    outs = []
    for x3 in (x_prompt, x_sample):
        seq_len = x3.shape[1]
        if seq_len not in ropes:
            ropes[seq_len] = _rope_tables(seq_len)
        outs.append(_trunk(x3, weights, gf, ropes[seq_len]))
    return tuple(outs)
```

```python
import functools
import math

import jax
import jax.numpy as jnp
from jax import lax
from jax.experimental import pallas as pl
from jax.experimental.pallas import tpu as pltpu

F32 = jnp.float32
BF16 = jnp.bfloat16

D_MODEL = 2048
HEAD_DIM = 128
DIL_GROUPS = ((128, 1), (512, 4), (2048, 16))
N_DIL_GROUPS = 3
DIL_HEADS = 4
DIL_WIDTH = DIL_HEADS * HEAD_DIM
DIL_QKV_WIDTH = 3 * DIL_WIDTH
GQA_Q_HEADS = 12
GQA_KV_HEADS = 4
GQA_GROUP = GQA_Q_HEADS // GQA_KV_HEADS
GQA_Q_WIDTH = GQA_Q_HEADS * HEAD_DIM
GQA_KV_WIDTH = GQA_KV_HEADS * HEAD_DIM
GRID_W = 64
ROPE_THETA = 10000.0
D_FF = 4 * D_MODEL
W_AQ = 0
W_AK = N_DIL_GROUPS * DIL_WIDTH
W_AV = 2 * N_DIL_GROUPS * DIL_WIDTH
W_REST = 3 * N_DIL_GROUPS * DIL_WIDTH
COL_BQ = DIL_QKV_WIDTH
COL_BK = COL_BQ + GQA_Q_WIDTH
COL_BV = COL_BK + GQA_KV_WIDTH
COL_GA = COL_BV + GQA_KV_WIDTH
COL_GB = COL_GA + D_MODEL
MAIN_WIDTH = COL_GB + D_MODEL
RMS_EPS = 1e-6
NEG_INF = -1e30
ATTN_SCALE = HEAD_DIM ** -0.5
Q_PRESCALE = ATTN_SCALE * 1.4426950408889634

V7X_VMEM_BYTES = 64 * 1024 * 1024
MIB = 1024 * 1024

IN_TM, IN_TN = 1024, 2048
IN_DIL_TN = DIL_QKV_WIDTH
DIL_TQ = 128
DIL_HALO = 64
DIL_UNITS = 4
FLASH_TQ, FLASH_TK = 512, 512
FLASH_ONES_ROWS = 16
FLASH_UNROLL = 8
MERGE_TM = 256
MLP_TM, MLP_TF = 1024, 1024


def _vmem_limit(nbytes):
    return int(min(nbytes + 16 * MIB, V7X_VMEM_BYTES - 8 * MIB))


def _rms(x, g):
    return x * lax.rsqrt(jnp.mean(x * x, axis=-1, keepdims=True) + RMS_EPS) * g


def _in_proj_kernel(x_ref, g_ref, w_ref, o_ref, h_ref):
    @pl.when(pl.program_id(1) == 0)
    def _():
        h_ref[...] = _rms(x_ref[...], g_ref[...]).astype(BF16)

    o_ref[...] = jnp.dot(h_ref[...], w_ref[...], preferred_element_type=F32).astype(o_ref.dtype)


def _in_proj(x, g_mix, w_main):
    rows = x.shape[0]
    tm, tn = IN_TM, IN_TN
    est = 2 * tm * D_MODEL * 4 + 2 * tm * D_MODEL * 2 + 2 * D_MODEL * tn * 2 + 2 * tm * tn * 2
    return pl.pallas_call(
        _in_proj_kernel,
        grid=(rows // tm, MAIN_WIDTH // tn),
        in_specs=[
            pl.BlockSpec((tm, D_MODEL), lambda i, j: (i, 0)),
            pl.BlockSpec((1, D_MODEL), lambda i, j: (0, 0)),
            pl.BlockSpec((D_MODEL, tn), lambda i, j: (0, j)),
        ],
        out_specs=[pl.BlockSpec((tm, tn), lambda i, j: (i, j)),
                   pl.BlockSpec((tm, D_MODEL), lambda i, j: (i, 0))],
        out_shape=[jax.ShapeDtypeStruct((rows, MAIN_WIDTH), BF16),
                   jax.ShapeDtypeStruct((rows, D_MODEL), BF16)],
        compiler_params=pltpu.CompilerParams(
            dimension_semantics=("parallel", "arbitrary"), vmem_limit_bytes=_vmem_limit(est)),
        name="in_proj",
    )(x, g_mix, w_main)


def _in_proj_dil_kernel(h_ref, w_ref, o_ref, res_ref, *, dilation):
    res = jnp.dot(h_ref[...], w_ref[...], preferred_element_type=F32)
    strips, tm, lanes = res_ref.shape
    for s in range(strips):
        res_ref[s] = res[:, s * lanes:(s + 1) * lanes]
    sub = tm // dilation
    for r in range(dilation):
        for s in range(strips):
            o_ref[r, :, s * lanes:(s + 1) * lanes] = (
                res_ref[s, pl.ds(r, sub, stride=dilation), :].astype(o_ref.dtype))


def _in_proj_dil(h, w_qkv, dilation, n_seq, seq_len):
    rows = h.shape[0]
    tm, tn = IN_TM, IN_DIL_TN
    tiles_per_seq = seq_len // tm
    est = 2 * tm * D_MODEL * 2 + 2 * D_MODEL * tn * 2 + 2 * tm * tn * 2 + tm * tn * 4
    return pl.pallas_call(
        functools.partial(_in_proj_dil_kernel, dilation=dilation),
        grid=(rows // tm, DIL_QKV_WIDTH // tn),
        in_specs=[
            pl.BlockSpec((tm, D_MODEL), lambda i, j: (i, 0)),
            pl.BlockSpec((D_MODEL, tn), lambda i, j: (0, j)),
        ],
        out_specs=pl.BlockSpec((None, dilation, tm // dilation, tn),
                               lambda i, j: (i // tiles_per_seq, 0, i % tiles_per_seq, j)),
        out_shape=jax.ShapeDtypeStruct((n_seq, dilation, seq_len // dilation, DIL_QKV_WIDTH), BF16),
        scratch_shapes=[pltpu.VMEM((tn // HEAD_DIM, tm, HEAD_DIM), F32)],
        compiler_params=pltpu.CompilerParams(
            dimension_semantics=("parallel", "parallel"), vmem_limit_bytes=_vmem_limit(est)),
        name=f"in_proj_dil{dilation}",
    )(h, w_qkv)


def _dilated_kernel(*refs, dilation, sub_len, slopes, has_state, emit_final):
    q_ref, kp_ref, kc_ref, kn_ref, vp_ref, vc_ref, vn_ref = refs[:7]
    refs = refs[7:]
    if has_state:
        acc_in, ml_in = refs[:2]
        refs = refs[2:]
    if emit_final:
        y_ref = refs[0]
        refs = refs[1:]
    else:
        acc_out, ml_out = refs[:2]
        refs = refs[2:]
    strided = dilation > 1
    if strided:
        acc_t, ml_t = refs

    tq, tk = DIL_TQ, DIL_TQ + 2 * DIL_HALO
    units = range(DIL_UNITS)
    heads = range(DIL_HEADS)
    cols = [slice(h * HEAD_DIM, (h + 1) * HEAD_DIM) for h in heads]
    step = pl.program_id(1)
    n_rgroups = dilation // DIL_UNITS if strided else 1

    def for_residue_group(fn):
        if n_rgroups == 1:
            fn(0)
        else:
            for g in range(n_rgroups):
                pl.when(pl.program_id(2) == g)(functools.partial(fn, g * DIL_UNITS))

    def state_rows(r0, u):
        return pl.ds(r0 + u, tq, stride=dilation)

    if strided and has_state:
        def gather(r0):
            for u in units:
                for h in heads:
                    acc_t[u, h] = acc_in[h, state_rows(r0, u), :]
                ml_t[u] = ml_in[state_rows(r0, u), :]
        for_residue_group(gather)

    def unit_rows(u):
        return slice(u * tq, (u + 1) * tq)

    def prev_acc(u, h):
        return acc_t[u, h] if strided else acc_in[h, unit_rows(u), :]

    def prev_ml(u):
        return ml_t[u] if strided else ml_in[unit_rows(u), :]

    if strided:
        def q_of(u, h):
            return q_ref[u, :, cols[h]]

        def window(p_ref, c_ref, n_ref, u, h):
            return jnp.concatenate([p_ref[u, :, cols[h]], c_ref[u, :, cols[h]], n_ref[u, :, cols[h]]], axis=0)
    else:
        def q_of(u, h):
            return q_ref[unit_rows(u), cols[h]]

        def window(p_ref, c_ref, n_ref, u, h):
            lo, hi = u * tq - DIL_HALO, u * tq + tq + DIL_HALO
            parts = []
            if lo < 0:
                parts.append(p_ref[:, cols[h]])
            parts.append(c_ref[max(lo, 0):min(hi, DIL_UNITS * tq), cols[h]])
            if hi > DIL_UNITS * tq:
                parts.append(n_ref[:, cols[h]])
            return jnp.concatenate(parts, axis=0) if len(parts) > 1 else parts[0]

    row = lax.broadcasted_iota(jnp.int32, (tq, tk), 0)
    col = lax.broadcasted_iota(jnp.int32, (tq, tk), 1)
    rel = col - DIL_HALO - row
    band = jnp.abs(rel) <= DIL_HALO
    dist = (dilation * jnp.abs(rel)).astype(F32)
    alibi = [slopes[h] * dist for h in heads]

    def mask_of(tile):
        key_idx = tile * tq - DIL_HALO + col
        return band & (key_idx >= 0) & (key_idx < sub_len)

    if strided:
        shared_mask = mask_of(step)
        masks = [shared_mask for _ in units]
    else:
        masks = [mask_of(step * DIL_UNITS + u) for u in units]
    lane = lax.broadcasted_iota(jnp.int32, (tq, HEAD_DIM), 1)
    ones = jnp.ones((tk, HEAD_DIM), BF16)

    pairs = [(u, h) for u in units for h in heads]
    scores, m_news, alphas, probs = {}, {}, {}, {}
    for u, h in pairs:
        s = lax.dot_general(q_of(u, h), window(kp_ref, kc_ref, kn_ref, u, h), (((1,), (1,)), ((), ())),
                            preferred_element_type=F32)
        scores[u, h] = jnp.where(masks[u], s - alibi[h], NEG_INF)
    ml_prevs = [prev_ml(u) for u in units] if has_state else None
    for u, h in pairs:
        m_cur = jnp.max(scores[u, h], axis=-1, keepdims=True)
        if has_state:
            m_prev = ml_prevs[u][:, h:h + 1]
            m_news[u, h] = jnp.maximum(m_prev, m_cur)
            alphas[u, h] = jnp.exp(m_prev - m_news[u, h])
        else:
            m_news[u, h] = m_cur
    for u, h in pairs:
        probs[u, h] = jnp.exp(scores[u, h] - m_news[u, h]).astype(BF16)
    ml_news = [jnp.zeros((tq, HEAD_DIM), F32) for _ in units]
    for u, h in pairs:
        v_aug = jnp.concatenate([window(vp_ref, vc_ref, vn_ref, u, h), ones], axis=1)
        both = jnp.dot(probs[u, h], v_aug, preferred_element_type=F32)
        acc, l_new = both[:, :HEAD_DIM], both[:, HEAD_DIM:]
        if has_state:
            l_new = alphas[u, h] * ml_prevs[u][:, DIL_HEADS + h:DIL_HEADS + h + 1] + l_new
            acc = alphas[u, h] * prev_acc(u, h) + acc
        if emit_final:
            y_ref[unit_rows(u), cols[h]] = (acc / l_new).astype(y_ref.dtype)
        elif strided:
            acc_t[u, h] = acc
        else:
            acc_out[h, unit_rows(u), :] = acc
        ml_news[u] = jnp.where(lane == h, m_news[u, h], ml_news[u])
        ml_news[u] = jnp.where(lane == DIL_HEADS + h, l_new, ml_news[u])
    if not emit_final:
        if strided:
            for u in units:
                ml_t[u] = ml_news[u]

            def scatter(r0):
                for u in units:
                    for h in heads:
                        acc_out[h, state_rows(r0, u), :] = acc_t[u, h]
                    ml_out[state_rows(r0, u), :] = ml_t[u]
            for_residue_group(scatter)
        else:
            for u in units:
                ml_out[unit_rows(u), :] = ml_news[u]


def _dilated_call(qkv, col0, group, state, emit_final, n_seq, seq_len):
    window, dilation = DIL_GROUPS[group]
    assert window == 2 * DIL_HALO * dilation
    assert not (emit_final and dilation > 1), "the bf16 output is written in token order only"
    sub_len = seq_len // dilation
    nt = sub_len // DIL_TQ
    rows = n_seq * seq_len
    n_heads = N_DIL_GROUPS * DIL_HEADS
    slopes = tuple(float(2.0 ** (-8.0 * (group * DIL_HEADS + h + 1) / n_heads)) for h in range(DIL_HEADS))

    halo_blocks = 2 * nt
    if dilation == 1:
        assert nt % DIL_UNITS == 0
        n_steps, n_rgroups = nt // DIL_UNITS, 1
        per_step = 2 * DIL_UNITS
        cur_block, halo_block = (None, None, DIL_UNITS * DIL_TQ, DIL_WIDTH), (None, None, DIL_HALO, DIL_WIDTH)
    else:
        assert dilation % DIL_UNITS == 0
        n_steps, n_rgroups = nt, dilation // DIL_UNITS
        per_step = 2
        cur_block, halo_block = (None, DIL_UNITS, DIL_TQ, DIL_WIDTH), (None, DIL_UNITS, DIL_HALO, DIL_WIDTH)

    def cur_spec(kind):
        return pl.BlockSpec(cur_block, lambda b, i, r: (b, r, i, col0 + kind))

    def prev_spec(kind):
        return pl.BlockSpec(halo_block, lambda b, i, r: (b, r, jnp.maximum(per_step * i - 1, 0), col0 + kind))

    def next_spec(kind):
        return pl.BlockSpec(halo_block,
                            lambda b, i, r: (b, r, jnp.minimum(per_step * (i + 1), halo_blocks - 1), col0 + kind))

    span = seq_len // n_steps
    acc_spec = pl.BlockSpec((DIL_HEADS, span, HEAD_DIM), lambda b, i, r: (0, b * n_steps + i, 0))
    ml_spec = pl.BlockSpec((span, HEAD_DIM), lambda b, i, r: (b * n_steps + i, 0))
    y_spec = pl.BlockSpec((span, DIL_WIDTH), lambda b, i, r: (b * n_steps + i, 0))

    in_specs = [cur_spec(0), prev_spec(1), cur_spec(1), next_spec(1), prev_spec(2), cur_spec(2), next_spec(2)]
    args = [qkv] * 7
    has_state = state is not None
    if has_state:
        in_specs += [acc_spec, ml_spec]
        args += list(state)
    if emit_final:
        out_specs = y_spec
        out_shape = jax.ShapeDtypeStruct((rows, DIL_WIDTH), BF16)
    else:
        out_specs = [acc_spec, ml_spec]
        out_shape = [jax.ShapeDtypeStruct((DIL_HEADS, rows, HEAD_DIM), F32),
                     jax.ShapeDtypeStruct((rows, HEAD_DIM), F32)]
    scratch = []
    if dilation > 1:
        scratch = [pltpu.VMEM((DIL_UNITS, DIL_HEADS, DIL_TQ, HEAD_DIM), F32),
                   pltpu.VMEM((DIL_UNITS, DIL_TQ, HEAD_DIM), F32)]
    kern = functools.partial(_dilated_kernel, dilation=dilation, sub_len=sub_len, slopes=slopes,
                             has_state=has_state, emit_final=emit_final)
    est = (4 * span * (DIL_WIDTH + HEAD_DIM) * 4 + 2 * 7 * DIL_UNITS * DIL_TQ * DIL_WIDTH * 2
           + 2 * DIL_UNITS * DIL_TQ * (DIL_WIDTH + HEAD_DIM) * 4)
    return pl.pallas_call(
        kern,
        grid=(n_seq, n_steps, n_rgroups),
        in_specs=in_specs,
        out_specs=out_specs,
        out_shape=out_shape,
        scratch_shapes=scratch,
        compiler_params=pltpu.CompilerParams(
            dimension_semantics=("parallel", "parallel", "arbitrary"), vmem_limit_bytes=_vmem_limit(est)),
        name=f"dilated_g{group}",
    )(*args)


def _dilated_mixer(proj, qkv4, qkv16, n_seq, seq_len):
    state = _dilated_call(qkv16, 0, 2, None, False, n_seq, seq_len)
    state = _dilated_call(qkv4, 0, 1, state, False, n_seq, seq_len)
    qkv1 = proj.reshape(n_seq, 1, seq_len, MAIN_WIDTH)
    return _dilated_call(qkv1, 0, 0, state, True, n_seq, seq_len)


def _rope_tables(seq_len):
    pos = jnp.arange(seq_len)
    row = (pos // GRID_W).astype(F32)
    col = (pos % GRID_W).astype(F32)
    half = HEAD_DIM // 2
    inv_freq = ROPE_THETA ** (-jnp.arange(0, half, 2, dtype=F32) / half)
    ang_r = row[:, None] * inv_freq
    ang_c = col[:, None] * inv_freq
    cos = jnp.concatenate([jnp.cos(ang_r), jnp.cos(ang_r), jnp.cos(ang_c), jnp.cos(ang_c)], axis=1)
    sin = jnp.concatenate([-jnp.sin(ang_r), jnp.sin(ang_r), -jnp.sin(ang_c), jnp.sin(ang_c)], axis=1)
    return cos, sin


def _swap_halves_matrix():
    quarter = HEAD_DIM // 4
    i = jnp.arange(2 * HEAD_DIM)
    partner = jnp.where((i & quarter) != 0, i - quarter, i + quarter)
    return (i[:, None] == partner[None, :]).astype(BF16)


def _prep_kernel(q_ref, k_ref, v_ref, cos_ref, sin_ref, gq_ref, gk_ref, perm_ref, qo_ref, ko_ref, vt_ref):
    cos = cos_ref[...]
    sin = sin_ref[...]
    perm = perm_ref[...]
    quarter = HEAD_DIM // 4
    lane = lax.broadcasted_iota(jnp.int32, (8, HEAD_DIM), 1)
    upper = (lane & quarter) != 0

    def gain_tables(g_ref, scale):
        g = jnp.broadcast_to(g_ref[...], (8, HEAD_DIM))
        g_partner = jnp.where(upper, pltpu.roll(g, quarter, 1), pltpu.roll(g, HEAD_DIM - quarter, 1))
        return cos * (g[:1] * scale), sin * (g_partner[:1] * scale)

    def norm_rope(src_ref, dst_ref, n_heads, tables):
        cos_g, sin_g = tables
        for pair in range(n_heads // 2):
            ps = slice(2 * pair * HEAD_DIM, 2 * (pair + 1) * HEAD_DIM)
            x2 = src_ref[:, ps]
            swapped = jnp.dot(x2, perm, preferred_element_type=F32)
            xf = x2.astype(F32)
            for half in range(2):
                hs = slice(half * HEAD_DIM, (half + 1) * HEAD_DIM)
                x = xf[:, hs]
                inv = lax.rsqrt(jnp.mean(x * x, axis=-1, keepdims=True) + RMS_EPS)
                out = (x * cos_g + swapped[:, hs] * sin_g) * inv
                dst_ref[:, (2 * pair + half) * HEAD_DIM:(2 * pair + half + 1) * HEAD_DIM] = out.astype(dst_ref.dtype)

    norm_rope(q_ref, qo_ref, GQA_Q_HEADS, gain_tables(gq_ref, Q_PRESCALE))
    norm_rope(k_ref, ko_ref, GQA_KV_HEADS, gain_tables(gk_ref, 1.0))
    for h in range(GQA_KV_HEADS):
        hs = slice(h * HEAD_DIM, (h + 1) * HEAD_DIM)
        vt_ref[h, 0] = v_ref[:, hs].astype(F32).T.astype(vt_ref.dtype)


def _qk_prep(proj, cos, sin, g_q, g_k, n_seq, seq_len):
    rows = proj.shape[0]
    tm = FLASH_TK
    tpos = seq_len // tm
    return pl.pallas_call(
        _prep_kernel,
        grid=(rows // tm,),
        in_specs=[
            pl.BlockSpec((tm, GQA_Q_WIDTH), lambda i: (i, COL_BQ // GQA_Q_WIDTH)),
            pl.BlockSpec((tm, GQA_KV_WIDTH), lambda i: (i, COL_BK // GQA_KV_WIDTH)),
            pl.BlockSpec((tm, GQA_KV_WIDTH), lambda i: (i, COL_BV // GQA_KV_WIDTH)),
            pl.BlockSpec((tm, HEAD_DIM), lambda i: (i % tpos, 0)),
            pl.BlockSpec((tm, HEAD_DIM), lambda i: (i % tpos, 0)),
            pl.BlockSpec((1, HEAD_DIM), lambda i: (0, 0)),
            pl.BlockSpec((1, HEAD_DIM), lambda i: (0, 0)),
            pl.BlockSpec((2 * HEAD_DIM, 2 * HEAD_DIM), lambda i: (0, 0)),
        ],
        out_specs=[pl.BlockSpec((tm, GQA_Q_WIDTH), lambda i: (i, 0)),
                   pl.BlockSpec((tm, GQA_KV_WIDTH), lambda i: (i, 0)),
                   pl.BlockSpec((None, GQA_KV_HEADS, 1, HEAD_DIM, tm), lambda i: (i // tpos, 0, i % tpos, 0, 0))],
        out_shape=[jax.ShapeDtypeStruct((rows, GQA_Q_WIDTH), BF16),
                   jax.ShapeDtypeStruct((rows, GQA_KV_WIDTH), BF16),
                   jax.ShapeDtypeStruct((n_seq, GQA_KV_HEADS, tpos, HEAD_DIM, tm), BF16)],
        compiler_params=pltpu.CompilerParams(
            dimension_semantics=("parallel",), vmem_limit_bytes=_vmem_limit(16 * MIB)),
        name="qk_prep",
    )(proj, proj, proj, cos, sin, g_q, g_k, _swap_halves_matrix())


def _flash_kernel(q_ref, k_ref, vt_ref, o_ref, st_sc, mx_sc, m_sc, acc_sc, *, n_chunks):
    tk = FLASH_TK
    m_sc[...] = jnp.full(m_sc.shape, NEG_INF, F32)
    acc_sc[...] = jnp.zeros(acc_sc.shape, F32)
    ones = jnp.ones((FLASH_ONES_ROWS, tk), BF16)

    def scores(c, slot, g):
        k = k_ref[pl.ds(pl.multiple_of(c * tk, tk), tk), :]
        q = q_ref[:, g * HEAD_DIM:(g + 1) * HEAD_DIM]
        st = lax.dot_general(k, q, (((1,), (1,)), ((), ())), preferred_element_type=F32)
        st_sc[slot, g] = st
        mx_sc[slot, g] = jnp.max(st, axis=0, keepdims=True)

    def consume(c, slot, g):
        vt = jnp.concatenate([vt_ref[c], ones], axis=0)
        m_prev = m_sc[g]
        m_new = jnp.maximum(m_prev, mx_sc[slot, g])
        alpha = jnp.exp2(m_prev - m_new)
        p = jnp.exp2(st_sc[slot, g] - m_new).astype(BF16)
        acc_sc[g] = alpha * acc_sc[g] + jnp.dot(vt, p, preferred_element_type=F32)
        m_sc[g] = m_new

    def step(c, slot, has_next):
        for g in range(GQA_GROUP):
            if has_next:
                scores(c + 1, 1 - slot, g)
            consume(c, slot, g)

    for g in range(GQA_GROUP):
        scores(0, 0, g)

    unroll = math.gcd(FLASH_UNROLL, n_chunks)

    def group(i, carry):
        for u in range(unroll):
            step(unroll * i + u, u % 2, True)
        return carry

    lax.fori_loop(0, n_chunks // unroll - 1, group, 0)
    for u in range(unroll):
        step(n_chunks - unroll + u, u % 2, u < unroll - 1)
    for g in range(GQA_GROUP):
        acc = acc_sc[g]
        o_t = acc[:HEAD_DIM] / acc[HEAD_DIM:HEAD_DIM + 1]
        o_ref[:, g * HEAD_DIM:(g + 1) * HEAD_DIM] = o_t.T.astype(o_ref.dtype)


def _flash(q, k, vt, n_seq, seq_len):
    rows = q.shape[0]
    tq, tk = FLASH_TQ, FLASH_TK
    nq = seq_len // tq
    n_chunks = seq_len // tk
    gw = GQA_GROUP * HEAD_DIM
    acc_rows = HEAD_DIM + FLASH_ONES_ROWS
    assert math.gcd(FLASH_UNROLL, n_chunks) % 2 == 0
    est = (2 * 2 * seq_len * HEAD_DIM * 2 + 4 * tq * gw * 2 + GQA_GROUP * (acc_rows + 8 * 3) * tq * 4
           + (2 * GQA_GROUP + 3) * tk * tq * 4)
    return pl.pallas_call(
        functools.partial(_flash_kernel, n_chunks=n_chunks),
        grid=(n_seq, GQA_KV_HEADS, nq),
        in_specs=[
            pl.BlockSpec((tq, gw), lambda b, h, i: (b * nq + i, h)),
            pl.BlockSpec((seq_len, HEAD_DIM), lambda b, h, i: (b, h)),
            pl.BlockSpec((None, None, n_chunks, HEAD_DIM, tk), lambda b, h, i: (b, h, 0, 0, 0)),
        ],
        out_specs=pl.BlockSpec((tq, gw), lambda b, h, i: (b * nq + i, h)),
        out_shape=jax.ShapeDtypeStruct((rows, GQA_Q_WIDTH), BF16),
        scratch_shapes=[pltpu.VMEM((2, GQA_GROUP, tk, tq), F32),
                        pltpu.VMEM((2, GQA_GROUP, 1, tq), F32),
                        pltpu.VMEM((GQA_GROUP, 1, tq), F32),
                        pltpu.VMEM((GQA_GROUP, acc_rows, tq), F32)],
        compiler_params=pltpu.CompilerParams(
            dimension_semantics=("parallel", "parallel", "arbitrary"), vmem_limit_bytes=_vmem_limit(est)),
        name="gqa_flash",
    )(q, k, vt)


def _sigmoid(x):
    return 1.0 / (1.0 + jnp.exp(-x))


def _merge_kernel(ya_ref, yb_ref, ga0_ref, ga1_ref, gb0_ref, gb1_ref, x_ref, wa_ref, wb_ref, wo_ref, g_ref,
                  x1_ref, h2_ref):
    half = D_MODEL // 2
    ya = ya_ref[...]
    yb = yb_ref[...]
    x1 = x_ref[...]
    for n, (ga_ref, gb_ref) in enumerate(((ga0_ref, gb0_ref), (ga1_ref, gb1_ref))):
        cs = slice(n * half, (n + 1) * half)
        o_a = jnp.dot(ya, wa_ref[:, cs], preferred_element_type=F32)
        o_b = jnp.dot(yb, wb_ref[:, cs], preferred_element_type=F32)
        merged = _sigmoid(ga_ref[...].astype(F32)) * o_a + _sigmoid(gb_ref[...].astype(F32)) * o_b
        x1 = x1 + jnp.dot(merged.astype(BF16), wo_ref[cs, :], preferred_element_type=F32)
    x1_ref[...] = x1
    h2_ref[...] = _rms(x1, g_ref[...]).astype(h2_ref.dtype)


def _merge(y_a, y_b, proj, x, w_a, w_b, w_out, g_mlp):
    rows = x.shape[0]
    tm = MERGE_TM
    half = D_MODEL // 2
    once = pl.Buffered(1)
    est = (2 * D_MODEL * D_MODEL * 2 + 2 * tm * (DIL_WIDTH + GQA_Q_WIDTH) * 2 + 2 * 4 * tm * half * 2
           + 4 * tm * D_MODEL * 4 + 2 * tm * D_MODEL * 2 + 6 * tm * D_MODEL * 4)
    return pl.pallas_call(
        _merge_kernel,
        grid=(rows // tm,),
        in_specs=[
            pl.BlockSpec((tm, DIL_WIDTH), lambda i: (i, 0)),
            pl.BlockSpec((tm, GQA_Q_WIDTH), lambda i: (i, 0)),
            pl.BlockSpec((tm, half), lambda i: (i, COL_GA // half)),
            pl.BlockSpec((tm, half), lambda i: (i, COL_GA // half + 1)),
            pl.BlockSpec((tm, half), lambda i: (i, COL_GB // half)),
            pl.BlockSpec((tm, half), lambda i: (i, COL_GB // half + 1)),
            pl.BlockSpec((tm, D_MODEL), lambda i: (i, 0)),
            pl.BlockSpec((DIL_WIDTH, D_MODEL), lambda i: (0, 0), pipeline_mode=once),
            pl.BlockSpec((GQA_Q_WIDTH, D_MODEL), lambda i: (0, 0), pipeline_mode=once),
            pl.BlockSpec((D_MODEL, D_MODEL), lambda i: (0, 0), pipeline_mode=once),
            pl.BlockSpec((1, D_MODEL), lambda i: (0, 0)),
        ],
        out_specs=[pl.BlockSpec((tm, D_MODEL), lambda i: (i, 0)),
                   pl.BlockSpec((tm, D_MODEL), lambda i: (i, 0))],
        out_shape=[jax.ShapeDtypeStruct((rows, D_MODEL), F32),
                   jax.ShapeDtypeStruct((rows, D_MODEL), BF16)],
        compiler_params=pltpu.CompilerParams(
            dimension_semantics=("parallel",), vmem_limit_bytes=_vmem_limit(est)),
        name="merge_out_proj",
    )(y_a, y_b, proj, proj, proj, proj, x, w_a, w_b, w_out, g_mlp)


def _mlp_kernel(h2_ref, x1_ref, w1_ref, w2_ref, g_ref, o_ref):
    f = pl.program_id(1)

    @pl.when(f == 0)
    def _():
        o_ref[...] = x1_ref[...]

    u = jnp.maximum(jnp.dot(h2_ref[...], w1_ref[...], preferred_element_type=F32), 0.0)
    o_ref[...] += jnp.dot((u * u).astype(BF16), w2_ref[...], preferred_element_type=F32)

    @pl.when(f == pl.num_programs(1) - 1)
    def _():
        o_ref[...] = _rms(o_ref[...], g_ref[...])


def _mlp(h2, x1, w_ff1, w_ff2, g_final):
    rows = h2.shape[0]
    tm, tf = MLP_TM, MLP_TF
    est = 2 * tm * D_MODEL * 2 + 3 * tm * D_MODEL * 4 + 4 * D_MODEL * tf * 2
    return pl.pallas_call(
        _mlp_kernel,
        grid=(rows // tm, D_FF // tf),
        in_specs=[
            pl.BlockSpec((tm, D_MODEL), lambda i, f: (i, 0)),
            pl.BlockSpec((tm, D_MODEL), lambda i, f: (i, 0), pipeline_mode=pl.Buffered(1)),
            pl.BlockSpec((D_MODEL, tf), lambda i, f: (0, f)),
            pl.BlockSpec((tf, D_MODEL), lambda i, f: (f, 0)),
            pl.BlockSpec((1, D_MODEL), lambda i, f: (0, 0)),
        ],
        out_specs=pl.BlockSpec((tm, D_MODEL), lambda i, f: (i, 0)),
        out_shape=jax.ShapeDtypeStruct((rows, D_MODEL), F32),
        compiler_params=pltpu.CompilerParams(
            dimension_semantics=("parallel", "arbitrary"), vmem_limit_bytes=_vmem_limit(est)),
        name="mlp_final_norm",
    )(h2, x1, w_ff1, w_ff2, g_final)


def _trunk(x3, weights, g_final, rope):
    n_seq, seq_len, _ = x3.shape
    x = x3.reshape(n_seq * seq_len, D_MODEL)
    cos, sin = rope
    g_mix, w_main, w_qkv4, w_qkv16, g_q, g_k, w_a, w_b, w_out, g_mlp, w_ff1, w_ff2 = weights
    proj, h = _in_proj(x, g_mix, w_main)
    qkv4 = _in_proj_dil(h, w_qkv4, DIL_GROUPS[1][1], n_seq, seq_len)
    qkv16 = _in_proj_dil(h, w_qkv16, DIL_GROUPS[2][1], n_seq, seq_len)
    y_a = _dilated_mixer(proj, qkv4, qkv16, n_seq, seq_len)
    q, k, vt = _qk_prep(proj, cos, sin, g_q, g_k, n_seq, seq_len)
    y_b = _flash(q, k, vt, n_seq, seq_len)
    x1, h2 = _merge(y_a, y_b, proj, x, w_a, w_b, w_out, g_mlp)
    y = _mlp(h2, x1, w_ff1, w_ff2, g_final)
    return y.reshape(n_seq, seq_len, D_MODEL)


def _group_qkv_columns(w, group):
    q, k, v = (w[:, base + group * DIL_WIDTH: base + (group + 1) * DIL_WIDTH] for base in (W_AQ, W_AK, W_AV))
    return [q * ATTN_SCALE, k, v]


def kernel(x_prompt, x_sample, g_mix, w_in, g_q, g_k, w_branch, w_out, g_mlp, w_ff1, w_ff2, g_final):
    assert w_in.shape[0] == 1, "single-layer stack only"
    w = w_in[0]
    w_main = jnp.concatenate(_group_qkv_columns(w, 0) + [w[:, W_REST:]], axis=1).astype(BF16)
    w_qkv4 = jnp.concatenate(_group_qkv_columns(w, 1), axis=1).astype(BF16)
    w_qkv16 = jnp.concatenate(_group_qkv_columns(w, 2), axis=1).astype(BF16)
    wb = w_branch[0].astype(BF16)
    weights = (g_mix[0][None], w_main, w_qkv4, w_qkv16,
               g_q[0][None], g_k[0][None], wb[:DIL_WIDTH], wb[DIL_WIDTH:], w_out[0].astype(BF16),
               g_mlp[0][None], w_ff1[0].astype(BF16), w_ff2[0].astype(BF16))
    gf = g_final[None]
    ropes = {}
    outs = []
    for x3 in (x_prompt, x_sample):
        seq_len = x3.shape[1]
        if seq_len not in ropes:
            ropes[seq_len] = _rope_tables(seq_len)
        outs.append(_trunk(x3, weights, gf, ropes[seq_len]))
    return tuple(outs)
```

```python
import functools
import math

import jax
import jax.numpy as jnp
from jax import lax
from jax.experimental import pallas as pl
from jax.experimental.pallas import tpu as pltpu

F32 = jnp.float32
BF16 = jnp.bfloat16

D_MODEL = 2048
HEAD_DIM = 128
DIL_GROUPS = ((128, 1), (512, 4), (2048, 16))
N_DIL_GROUPS = 3
DIL_HEADS = 4
DIL_WIDTH = DIL_HEADS * HEAD_DIM
DIL_QKV_WIDTH = 3 * DIL_WIDTH
GQA_Q_HEADS = 12
GQA_KV_HEADS = 4
GQA_GROUP = GQA_Q_HEADS // GQA_KV_HEADS
GQA_Q_WIDTH = GQA_Q_HEADS * HEAD_DIM
GQA_KV_WIDTH = GQA_KV_HEADS * HEAD_DIM
GRID_W = 64
ROPE_THETA = 10000.0
D_FF = 4 * D_MODEL
W_AQ = 0
W_AK = N_DIL_GROUPS * DIL_WIDTH
W_AV = 2 * N_DIL_GROUPS * DIL_WIDTH
W_REST = 3 * N_DIL_GROUPS * DIL_WIDTH
COL_BQ = DIL_QKV_WIDTH
COL_BK = COL_BQ + GQA_Q_WIDTH
COL_BV = COL_BK + GQA_KV_WIDTH
COL_GA = COL_BV + GQA_KV_WIDTH
COL_GB = COL_GA + D_MODEL
MAIN_WIDTH = COL_GB + D_MODEL
RMS_EPS = 1e-6
NEG_INF = -1e30
ATTN_SCALE = HEAD_DIM ** -0.5
Q_PRESCALE = ATTN_SCALE * 1.4426950408889634

V7X_VMEM_BYTES = 64 * 1024 * 1024
MIB = 1024 * 1024

IN_TM, IN_TN = 1024, 2048
IN_DIL_TN = DIL_QKV_WIDTH
DIL_TQ = 128
DIL_HALO = 64
DIL_UNITS = 4
FLASH_TQ, FLASH_TK = 512, 512
FLASH_ONES_ROWS = 16
FLASH_UNROLL = 8
FLASH_TILES_PER_STEP = 4
MERGE_TM = 256
MLP_TM, MLP_TF = 1024, 1024


def _vmem_limit(nbytes):
    return int(min(nbytes + 16 * MIB, V7X_VMEM_BYTES - 8 * MIB))


def _rms(x, g):
    return x * lax.rsqrt(jnp.mean(x * x, axis=-1, keepdims=True) + RMS_EPS) * g


def _in_proj_kernel(x_ref, g_ref, w_ref, o_ref, h_ref):
    @pl.when(pl.program_id(1) == 0)
    def _():
        h_ref[...] = _rms(x_ref[...], g_ref[...]).astype(BF16)

    o_ref[...] = jnp.dot(h_ref[...], w_ref[...], preferred_element_type=F32).astype(o_ref.dtype)


def _in_proj(x, g_mix, w_main):
    rows = x.shape[0]
    tm, tn = IN_TM, IN_TN
    est = 2 * tm * D_MODEL * 4 + 2 * tm * D_MODEL * 2 + 2 * D_MODEL * tn * 2 + 2 * tm * tn * 2
    return pl.pallas_call(
        _in_proj_kernel,
        grid=(rows // tm, MAIN_WIDTH // tn),
        in_specs=[
            pl.BlockSpec((tm, D_MODEL), lambda i, j: (i, 0)),
            pl.BlockSpec((1, D_MODEL), lambda i, j: (0, 0)),
            pl.BlockSpec((D_MODEL, tn), lambda i, j: (0, j)),
        ],
        out_specs=[pl.BlockSpec((tm, tn), lambda i, j: (i, j)),
                   pl.BlockSpec((tm, D_MODEL), lambda i, j: (i, 0))],
        out_shape=[jax.ShapeDtypeStruct((rows, MAIN_WIDTH), BF16),
                   jax.ShapeDtypeStruct((rows, D_MODEL), BF16)],
        compiler_params=pltpu.CompilerParams(
            dimension_semantics=("parallel", "arbitrary"), vmem_limit_bytes=_vmem_limit(est)),
        name="in_proj",
    )(x, g_mix, w_main)


def _in_proj_dil_kernel(h_ref, w_ref, o_ref, res_ref, *, dilation):
    res = jnp.dot(h_ref[...], w_ref[...], preferred_element_type=F32)
    strips, tm, lanes = res_ref.shape
    for s in range(strips):
        res_ref[s] = res[:, s * lanes:(s + 1) * lanes]
    sub = tm // dilation
    for r in range(dilation):
        for s in range(strips):
            o_ref[r, :, s * lanes:(s + 1) * lanes] = (
                res_ref[s, pl.ds(r, sub, stride=dilation), :].astype(o_ref.dtype))


def _in_proj_dil(h, w_qkv, dilation, n_seq, seq_len):
    rows = h.shape[0]
    tm, tn = IN_TM, IN_DIL_TN
    tiles_per_seq = seq_len // tm
    est = 2 * tm * D_MODEL * 2 + 2 * D_MODEL * tn * 2 + 2 * tm * tn * 2 + tm * tn * 4
    return pl.pallas_call(
        functools.partial(_in_proj_dil_kernel, dilation=dilation),
        grid=(rows // tm, DIL_QKV_WIDTH // tn),
        in_specs=[
            pl.BlockSpec((tm, D_MODEL), lambda i, j: (i, 0)),
            pl.BlockSpec((D_MODEL, tn), lambda i, j: (0, j)),
        ],
        out_specs=pl.BlockSpec((None, dilation, tm // dilation, tn),
                               lambda i, j: (i // tiles_per_seq, 0, i % tiles_per_seq, j)),
        out_shape=jax.ShapeDtypeStruct((n_seq, dilation, seq_len // dilation, DIL_QKV_WIDTH), BF16),
        scratch_shapes=[pltpu.VMEM((tn // HEAD_DIM, tm, HEAD_DIM), F32)],
        compiler_params=pltpu.CompilerParams(
            dimension_semantics=("parallel", "parallel"), vmem_limit_bytes=_vmem_limit(est)),
        name=f"in_proj_dil{dilation}",
    )(h, w_qkv)


def _dilated_kernel(*refs, dilation, sub_len, slopes, has_state, emit_final):
    q_ref, kp_ref, kc_ref, kn_ref, vp_ref, vc_ref, vn_ref = refs[:7]
    refs = refs[7:]
    if has_state:
        acc_in, ml_in = refs[:2]
        refs = refs[2:]
    if emit_final:
        y_ref = refs[0]
        refs = refs[1:]
    else:
        acc_out, ml_out = refs[:2]
        refs = refs[2:]
    strided = dilation > 1
    if strided:
        acc_t, ml_t = refs

    tq, tk = DIL_TQ, DIL_TQ + 2 * DIL_HALO
    units = range(DIL_UNITS)
    heads = range(DIL_HEADS)
    cols = [slice(h * HEAD_DIM, (h + 1) * HEAD_DIM) for h in heads]
    step = pl.program_id(1)
    n_rgroups = dilation // DIL_UNITS if strided else 1

    def for_residue_group(fn):
        if n_rgroups == 1:
            fn(0)
        else:
            for g in range(n_rgroups):
                pl.when(pl.program_id(2) == g)(functools.partial(fn, g * DIL_UNITS))

    def state_rows(r0, u):
        return pl.ds(r0 + u, tq, stride=dilation)

    if strided and has_state:
        def gather(r0):
            for u in units:
                for h in heads:
                    acc_t[u, h] = acc_in[h, state_rows(r0, u), :]
                ml_t[u] = ml_in[state_rows(r0, u), :]
        for_residue_group(gather)

    def unit_rows(u):
        return slice(u * tq, (u + 1) * tq)

    def prev_acc(u, h):
        return acc_t[u, h] if strided else acc_in[h, unit_rows(u), :]

    def prev_ml(u):
        return ml_t[u] if strided else ml_in[unit_rows(u), :]

    if strided:
        def q_of(u, h):
            return q_ref[u, :, cols[h]]

        def window(p_ref, c_ref, n_ref, u, h):
            return jnp.concatenate([p_ref[u, :, cols[h]], c_ref[u, :, cols[h]], n_ref[u, :, cols[h]]], axis=0)
    else:
        def q_of(u, h):
            return q_ref[unit_rows(u), cols[h]]

        def window(p_ref, c_ref, n_ref, u, h):
            lo, hi = u * tq - DIL_HALO, u * tq + tq + DIL_HALO
            parts = []
            if lo < 0:
                parts.append(p_ref[:, cols[h]])
            parts.append(c_ref[max(lo, 0):min(hi, DIL_UNITS * tq), cols[h]])
            if hi > DIL_UNITS * tq:
                parts.append(n_ref[:, cols[h]])
            return jnp.concatenate(parts, axis=0) if len(parts) > 1 else parts[0]

    row = lax.broadcasted_iota(jnp.int32, (tq, tk), 0)
    col = lax.broadcasted_iota(jnp.int32, (tq, tk), 1)
    rel = col - DIL_HALO - row
    band = jnp.abs(rel) <= DIL_HALO
    dist = (dilation * jnp.abs(rel)).astype(F32)
    alibi = [slopes[h] * dist for h in heads]

    def mask_of(tile):
        key_idx = tile * tq - DIL_HALO + col
        return band & (key_idx >= 0) & (key_idx < sub_len)

    if strided:
        shared_mask = mask_of(step)
        masks = [shared_mask for _ in units]
    else:
        masks = [mask_of(step * DIL_UNITS + u) for u in units]
    lane = lax.broadcasted_iota(jnp.int32, (tq, HEAD_DIM), 1)
    ones = jnp.ones((tk, HEAD_DIM), BF16)

    pairs = [(u, h) for u in units for h in heads]
    scores, m_news, alphas, probs = {}, {}, {}, {}
    for u, h in pairs:
        s = lax.dot_general(q_of(u, h), window(kp_ref, kc_ref, kn_ref, u, h), (((1,), (1,)), ((), ())),
                            preferred_element_type=F32)
        scores[u, h] = jnp.where(masks[u], s - alibi[h], NEG_INF)
    ml_prevs = [prev_ml(u) for u in units] if has_state else None
    for u, h in pairs:
        m_cur = jnp.max(scores[u, h], axis=-1, keepdims=True)
        if has_state:
            m_prev = ml_prevs[u][:, h:h + 1]
            m_news[u, h] = jnp.maximum(m_prev, m_cur)
            alphas[u, h] = jnp.exp(m_prev - m_news[u, h])
        else:
            m_news[u, h] = m_cur
    for u, h in pairs:
        probs[u, h] = jnp.exp(scores[u, h] - m_news[u, h]).astype(BF16)
    ml_news = [jnp.zeros((tq, HEAD_DIM), F32) for _ in units]
    for u, h in pairs:
        v_aug = jnp.concatenate([window(vp_ref, vc_ref, vn_ref, u, h), ones], axis=1)
        both = jnp.dot(probs[u, h], v_aug, preferred_element_type=F32)
        acc, l_new = both[:, :HEAD_DIM], both[:, HEAD_DIM:]
        if has_state:
            l_new = alphas[u, h] * ml_prevs[u][:, DIL_HEADS + h:DIL_HEADS + h + 1] + l_new
            acc = alphas[u, h] * prev_acc(u, h) + acc
        if emit_final:
            y_ref[unit_rows(u), cols[h]] = (acc / l_new).astype(y_ref.dtype)
        elif strided:
            acc_t[u, h] = acc
        else:
            acc_out[h, unit_rows(u), :] = acc
        ml_news[u] = jnp.where(lane == h, m_news[u, h], ml_news[u])
        ml_news[u] = jnp.where(lane == DIL_HEADS + h, l_new, ml_news[u])
    if not emit_final:
        if strided:
            for u in units:
                ml_t[u] = ml_news[u]

            def scatter(r0):
                for u in units:
                    for h in heads:
                        acc_out[h, state_rows(r0, u), :] = acc_t[u, h]
                    ml_out[state_rows(r0, u), :] = ml_t[u]
            for_residue_group(scatter)
        else:
            for u in units:
                ml_out[unit_rows(u), :] = ml_news[u]


def _dilated_call(qkv, col0, group, state, emit_final, n_seq, seq_len):
    window, dilation = DIL_GROUPS[group]
    assert window == 2 * DIL_HALO * dilation
    assert not (emit_final and dilation > 1), "the bf16 output is written in token order only"
    sub_len = seq_len // dilation
    nt = sub_len // DIL_TQ
    rows = n_seq * seq_len
    n_heads = N_DIL_GROUPS * DIL_HEADS
    slopes = tuple(float(2.0 ** (-8.0 * (group * DIL_HEADS + h + 1) / n_heads)) for h in range(DIL_HEADS))

    halo_blocks = 2 * nt
    if dilation == 1:
        assert nt % DIL_UNITS == 0
        n_steps, n_rgroups = nt // DIL_UNITS, 1
        per_step = 2 * DIL_UNITS
        cur_block, halo_block = (None, None, DIL_UNITS * DIL_TQ, DIL_WIDTH), (None, None, DIL_HALO, DIL_WIDTH)
    else:
        assert dilation % DIL_UNITS == 0
        n_steps, n_rgroups = nt, dilation // DIL_UNITS
        per_step = 2
        cur_block, halo_block = (None, DIL_UNITS, DIL_TQ, DIL_WIDTH), (None, DIL_UNITS, DIL_HALO, DIL_WIDTH)

    def cur_spec(kind):
        return pl.BlockSpec(cur_block, lambda b, i, r: (b, r, i, col0 + kind))

    def prev_spec(kind):
        return pl.BlockSpec(halo_block, lambda b, i, r: (b, r, jnp.maximum(per_step * i - 1, 0), col0 + kind))

    def next_spec(kind):
        return pl.BlockSpec(halo_block,
                            lambda b, i, r: (b, r, jnp.minimum(per_step * (i + 1), halo_blocks - 1), col0 + kind))

    span = seq_len // n_steps
    acc_spec = pl.BlockSpec((DIL_HEADS, span, HEAD_DIM), lambda b, i, r: (0, b * n_steps + i, 0))
    ml_spec = pl.BlockSpec((span, HEAD_DIM), lambda b, i, r: (b * n_steps + i, 0))
    y_spec = pl.BlockSpec((span, DIL_WIDTH), lambda b, i, r: (b * n_steps + i, 0))

    in_specs = [cur_spec(0), prev_spec(1), cur_spec(1), next_spec(1), prev_spec(2), cur_spec(2), next_spec(2)]
    args = [qkv] * 7
    has_state = state is not None
    if has_state:
        in_specs += [acc_spec, ml_spec]
        args += list(state)
    if emit_final:
        out_specs = y_spec
        out_shape = jax.ShapeDtypeStruct((rows, DIL_WIDTH), BF16)
    else:
        out_specs = [acc_spec, ml_spec]
        out_shape = [jax.ShapeDtypeStruct((DIL_HEADS, rows, HEAD_DIM), F32),
                     jax.ShapeDtypeStruct((rows, HEAD_DIM), F32)]
    scratch = []
    if dilation > 1:
        scratch = [pltpu.VMEM((DIL_UNITS, DIL_HEADS, DIL_TQ, HEAD_DIM), F32),
                   pltpu.VMEM((DIL_UNITS, DIL_TQ, HEAD_DIM), F32)]
    kern = functools.partial(_dilated_kernel, dilation=dilation, sub_len=sub_len, slopes=slopes,
                             has_state=has_state, emit_final=emit_final)
    est = (4 * span * (DIL_WIDTH + HEAD_DIM) * 4 + 2 * 7 * DIL_UNITS * DIL_TQ * DIL_WIDTH * 2
           + 2 * DIL_UNITS * DIL_TQ * (DIL_WIDTH + HEAD_DIM) * 4)
    return pl.pallas_call(
        kern,
        grid=(n_seq, n_steps, n_rgroups),
        in_specs=in_specs,
        out_specs=out_specs,
        out_shape=out_shape,
        scratch_shapes=scratch,
        compiler_params=pltpu.CompilerParams(
            dimension_semantics=("parallel", "parallel", "arbitrary"), vmem_limit_bytes=_vmem_limit(est)),
        name=f"dilated_g{group}",
    )(*args)


def _dilated_mixer(proj, qkv4, qkv16, n_seq, seq_len):
    state = _dilated_call(qkv16, 0, 2, None, False, n_seq, seq_len)
    state = _dilated_call(qkv4, 0, 1, state, False, n_seq, seq_len)
    qkv1 = proj.reshape(n_seq, 1, seq_len, MAIN_WIDTH)
    return _dilated_call(qkv1, 0, 0, state, True, n_seq, seq_len)


def _rope_tables(seq_len):
    n_rows = seq_len // GRID_W
    half = HEAD_DIM // 2
    inv_freq = ROPE_THETA ** (-jnp.arange(0, half, 2, dtype=F32) / half)
    ang_r = jnp.arange(n_rows, dtype=F32)[:, None] * inv_freq
    ang_c = jnp.arange(GRID_W, dtype=F32)[:, None] * inv_freq

    def per_position(row_part, col_part):
        shape = (n_rows, GRID_W, half // 2)
        r = jnp.broadcast_to(row_part[:, None, :], shape)
        c = jnp.broadcast_to(col_part[None, :, :], shape)
        return r, c

    cos_r, cos_c = per_position(jnp.cos(ang_r), jnp.cos(ang_c))
    sin_r, sin_c = per_position(jnp.sin(ang_r), jnp.sin(ang_c))
    cos = jnp.concatenate([cos_r, cos_r, cos_c, cos_c], axis=-1).reshape(seq_len, HEAD_DIM)
    sin = jnp.concatenate([-sin_r, sin_r, -sin_c, sin_c], axis=-1).reshape(seq_len, HEAD_DIM)
    return cos, sin


def _swap_halves_matrix():
    quarter = HEAD_DIM // 4
    i = jnp.arange(2 * HEAD_DIM)
    partner = jnp.where((i & quarter) != 0, i - quarter, i + quarter)
    return (i[:, None] == partner[None, :]).astype(BF16)


def _prep_kernel(q_ref, k_ref, v_ref, cos_ref, sin_ref, gq_ref, gk_ref, perm_ref, qo_ref, ko_ref, vt_ref):
    cos = cos_ref[...]
    sin = sin_ref[...]
    perm = perm_ref[...]
    quarter = HEAD_DIM // 4
    lane = lax.broadcasted_iota(jnp.int32, (8, HEAD_DIM), 1)
    upper = (lane & quarter) != 0

    def gain_tables(g_ref, scale):
        g = jnp.broadcast_to(g_ref[...], (8, HEAD_DIM))
        g_partner = jnp.where(upper, pltpu.roll(g, quarter, 1), pltpu.roll(g, HEAD_DIM - quarter, 1))
        return cos * (g[:1] * scale), sin * (g_partner[:1] * scale)

    def norm_rope(src_ref, dst_ref, n_heads, tables):
        cos_g, sin_g = tables
        for pair in range(n_heads // 2):
            ps = slice(2 * pair * HEAD_DIM, 2 * (pair + 1) * HEAD_DIM)
            x2 = src_ref[:, ps]
            swapped = jnp.dot(x2, perm, preferred_element_type=F32)
            xf = x2.astype(F32)
            for half in range(2):
                hs = slice(half * HEAD_DIM, (half + 1) * HEAD_DIM)
                x = xf[:, hs]
                inv = lax.rsqrt(jnp.mean(x * x, axis=-1, keepdims=True) + RMS_EPS)
                out = (x * cos_g + swapped[:, hs] * sin_g) * inv
                dst_ref[:, (2 * pair + half) * HEAD_DIM:(2 * pair + half + 1) * HEAD_DIM] = out.astype(dst_ref.dtype)

    norm_rope(q_ref, qo_ref, GQA_Q_HEADS, gain_tables(gq_ref, Q_PRESCALE))
    norm_rope(k_ref, ko_ref, GQA_KV_HEADS, gain_tables(gk_ref, 1.0))
    for h in range(GQA_KV_HEADS):
        hs = slice(h * HEAD_DIM, (h + 1) * HEAD_DIM)
        vt_ref[h, 0] = v_ref[:, hs].astype(F32).T.astype(vt_ref.dtype)


def _qk_prep(proj, cos, sin, g_q, g_k, n_seq, seq_len):
    rows = proj.shape[0]
    tm = FLASH_TK
    tpos = seq_len // tm
    return pl.pallas_call(
        _prep_kernel,
        grid=(rows // tm,),
        in_specs=[
            pl.BlockSpec((tm, GQA_Q_WIDTH), lambda i: (i, COL_BQ // GQA_Q_WIDTH)),
            pl.BlockSpec((tm, GQA_KV_WIDTH), lambda i: (i, COL_BK // GQA_KV_WIDTH)),
            pl.BlockSpec((tm, GQA_KV_WIDTH), lambda i: (i, COL_BV // GQA_KV_WIDTH)),
            pl.BlockSpec((tm, HEAD_DIM), lambda i: (i % tpos, 0)),
            pl.BlockSpec((tm, HEAD_DIM), lambda i: (i % tpos, 0)),
            pl.BlockSpec((1, HEAD_DIM), lambda i: (0, 0)),
            pl.BlockSpec((1, HEAD_DIM), lambda i: (0, 0)),
            pl.BlockSpec((2 * HEAD_DIM, 2 * HEAD_DIM), lambda i: (0, 0)),
        ],
        out_specs=[pl.BlockSpec((tm, GQA_Q_WIDTH), lambda i: (i, 0)),
                   pl.BlockSpec((tm, GQA_KV_WIDTH), lambda i: (i, 0)),
                   pl.BlockSpec((None, GQA_KV_HEADS, 1, HEAD_DIM, tm), lambda i: (i // tpos, 0, i % tpos, 0, 0))],
        out_shape=[jax.ShapeDtypeStruct((rows, GQA_Q_WIDTH), BF16),
                   jax.ShapeDtypeStruct((rows, GQA_KV_WIDTH), BF16),
                   jax.ShapeDtypeStruct((n_seq, GQA_KV_HEADS, tpos, HEAD_DIM, tm), BF16)],
        compiler_params=pltpu.CompilerParams(
            dimension_semantics=("parallel",), vmem_limit_bytes=_vmem_limit(16 * MIB)),
        name="qk_prep",
    )(proj, proj, proj, cos, sin, g_q, g_k, _swap_halves_matrix())


def _flash_kernel(q_ref, k_ref, vt_ref, o_ref, st_sc, mx_sc, m_sc, acc_sc, *, n_chunks, n_tiles):
    tq, tk = FLASH_TQ, FLASH_TK
    unroll = math.gcd(FLASH_UNROLL, n_chunks)
    groups_per_tile = n_chunks // unroll
    n_items = n_tiles * n_chunks
    ones = jnp.ones((FLASH_ONES_ROWS, tk), BF16)

    def reset():
        m_sc[...] = jnp.full(m_sc.shape, NEG_INF, F32)
        acc_sc[...] = jnp.zeros(acc_sc.shape, F32)

    def scores(item, slot, g):
        item = jnp.minimum(item, n_items - 1)
        tile, c = item // n_chunks, item % n_chunks
        k = k_ref[pl.ds(pl.multiple_of(c * tk, tk), tk), :]
        q = q_ref[pl.ds(pl.multiple_of(tile * tq, tq), tq), g * HEAD_DIM:(g + 1) * HEAD_DIM]
        st = lax.dot_general(k, q, (((1,), (1,)), ((), ())), preferred_element_type=F32)
        st_sc[slot, g] = st
        mx_sc[slot, g] = jnp.max(st, axis=0, keepdims=True)

    def consume(c, slot, g):
        vt = jnp.concatenate([vt_ref[c], ones], axis=0)
        m_prev = m_sc[g]
        m_new = jnp.maximum(m_prev, mx_sc[slot, g])
        alpha = jnp.exp2(m_prev - m_new)
        p = jnp.exp2(st_sc[slot, g] - m_new).astype(BF16)
        acc_sc[g] = alpha * acc_sc[g] + jnp.dot(vt, p, preferred_element_type=F32)
        m_sc[g] = m_new

    def finish(tile):
        rows = pl.ds(pl.multiple_of(tile * tq, tq), tq)
        for g in range(GQA_GROUP):
            acc = acc_sc[g]
            o_t = acc[:HEAD_DIM] / acc[HEAD_DIM:HEAD_DIM + 1]
            o_ref[rows, g * HEAD_DIM:(g + 1) * HEAD_DIM] = o_t.T.astype(o_ref.dtype)
        reset()

    reset()
    for g in range(GQA_GROUP):
        scores(0, 0, g)

    def group(i, carry):
        first_chunk = (i % groups_per_tile) * unroll
        for u in range(unroll):
            for g in range(GQA_GROUP):
                scores(i * unroll + u + 1, (u + 1) % 2, g)
                consume(first_chunk + u, u % 2, g)
        pl.when(i % groups_per_tile == groups_per_tile - 1)(functools.partial(finish, i // groups_per_tile))
        return carry

    lax.fori_loop(0, n_tiles * groups_per_tile, group, 0)


def _flash(q, k, vt, n_seq, seq_len):
    rows = q.shape[0]
    tq, tk = FLASH_TQ, FLASH_TK
    nq = seq_len // tq
    n_chunks = seq_len // tk
    gw = GQA_GROUP * HEAD_DIM
    acc_rows = HEAD_DIM + FLASH_ONES_ROWS
    assert math.gcd(FLASH_UNROLL, n_chunks) % 2 == 0
    n_tiles = math.gcd(FLASH_TILES_PER_STEP, nq)
    n_steps = nq // n_tiles
    est = (2 * 2 * seq_len * HEAD_DIM * 2 + 4 * n_tiles * tq * gw * 2 + GQA_GROUP * (acc_rows + 8 * 3) * tq * 4
           + (2 * GQA_GROUP + 3) * tk * tq * 4)
    return pl.pallas_call(
        functools.partial(_flash_kernel, n_chunks=n_chunks, n_tiles=n_tiles),
        grid=(n_seq, GQA_KV_HEADS, n_steps),
        in_specs=[
            pl.BlockSpec((n_tiles * tq, gw), lambda b, h, i: (b * n_steps + i, h)),
            pl.BlockSpec((seq_len, HEAD_DIM), lambda b, h, i: (b, h)),
            pl.BlockSpec((None, None, n_chunks, HEAD_DIM, tk), lambda b, h, i: (b, h, 0, 0, 0)),
        ],
        out_specs=pl.BlockSpec((n_tiles * tq, gw), lambda b, h, i: (b * n_steps + i, h)),
        out_shape=jax.ShapeDtypeStruct((rows, GQA_Q_WIDTH), BF16),
        scratch_shapes=[pltpu.VMEM((2, GQA_GROUP, tk, tq), F32),
                        pltpu.VMEM((2, GQA_GROUP, 1, tq), F32),
                        pltpu.VMEM((GQA_GROUP, 1, tq), F32),
                        pltpu.VMEM((GQA_GROUP, acc_rows, tq), F32)],
        compiler_params=pltpu.CompilerParams(
            dimension_semantics=("parallel", "parallel", "arbitrary"), vmem_limit_bytes=_vmem_limit(est)),
        name="gqa_flash",
    )(q, k, vt)


def _sigmoid(x):
    return 1.0 / (1.0 + jnp.exp(-x))


def _merge_kernel(ya_ref, yb_ref, ga0_ref, ga1_ref, gb0_ref, gb1_ref, x_ref, wa_ref, wb_ref, wo_ref, g_ref,
                  x1_ref, h2_ref):
    half = D_MODEL // 2
    ya = ya_ref[...]
    yb = yb_ref[...]
    x1 = x_ref[...]
    for n, (ga_ref, gb_ref) in enumerate(((ga0_ref, gb0_ref), (ga1_ref, gb1_ref))):
        cs = slice(n * half, (n + 1) * half)
        o_a = jnp.dot(ya, wa_ref[:, cs], preferred_element_type=F32)
        o_b = jnp.dot(yb, wb_ref[:, cs], preferred_element_type=F32)
        merged = _sigmoid(ga_ref[...].astype(F32)) * o_a + _sigmoid(gb_ref[...].astype(F32)) * o_b
        x1 = x1 + jnp.dot(merged.astype(BF16), wo_ref[cs, :], preferred_element_type=F32)
    x1_ref[...] = x1
    h2_ref[...] = _rms(x1, g_ref[...]).astype(h2_ref.dtype)


def _merge(y_a, y_b, proj, x, w_a, w_b, w_out, g_mlp):
    rows = x.shape[0]
    tm = MERGE_TM
    half = D_MODEL // 2
    once = pl.Buffered(1)
    est = (2 * D_MODEL * D_MODEL * 2 + 2 * tm * (DIL_WIDTH + GQA_Q_WIDTH) * 2 + 2 * 4 * tm * half * 2
           + 4 * tm * D_MODEL * 4 + 2 * tm * D_MODEL * 2 + 6 * tm * D_MODEL * 4)
    return pl.pallas_call(
        _merge_kernel,
        grid=(rows // tm,),
        in_specs=[
            pl.BlockSpec((tm, DIL_WIDTH), lambda i: (i, 0)),
            pl.BlockSpec((tm, GQA_Q_WIDTH), lambda i: (i, 0)),
            pl.BlockSpec((tm, half), lambda i: (i, COL_GA // half)),
            pl.BlockSpec((tm, half), lambda i: (i, COL_GA // half + 1)),
            pl.BlockSpec((tm, half), lambda i: (i, COL_GB // half)),
            pl.BlockSpec((tm, half), lambda i: (i, COL_GB // half + 1)),
            pl.BlockSpec((tm, D_MODEL), lambda i: (i, 0)),
            pl.BlockSpec((DIL_WIDTH, D_MODEL), lambda i: (0, 0), pipeline_mode=once),
            pl.BlockSpec((GQA_Q_WIDTH, D_MODEL), lambda i: (0, 0), pipeline_mode=once),
            pl.BlockSpec((D_MODEL, D_MODEL), lambda i: (0, 0), pipeline_mode=once),
            pl.BlockSpec((1, D_MODEL), lambda i: (0, 0)),
        ],
        out_specs=[pl.BlockSpec((tm, D_MODEL), lambda i: (i, 0)),
                   pl.BlockSpec((tm, D_MODEL), lambda i: (i, 0))],
        out_shape=[jax.ShapeDtypeStruct((rows, D_MODEL), F32),
                   jax.ShapeDtypeStruct((rows, D_MODEL), BF16)],
        compiler_params=pltpu.CompilerParams(
            dimension_semantics=("parallel",), vmem_limit_bytes=_vmem_limit(est)),
        name="merge_out_proj",
    )(y_a, y_b, proj, proj, proj, proj, x, w_a, w_b, w_out, g_mlp)


def _mlp_kernel(h2_ref, x1_ref, w1_ref, w2_ref, g_ref, o_ref):
    f = pl.program_id(1)

    @pl.when(f == 0)
    def _():
        o_ref[...] = x1_ref[...]

    u = jnp.maximum(jnp.dot(h2_ref[...], w1_ref[...], preferred_element_type=F32), 0.0)
    o_ref[...] += jnp.dot((u * u).astype(BF16), w2_ref[...], preferred_element_type=F32)

    @pl.when(f == pl.num_programs(1) - 1)
    def _():
        o_ref[...] = _rms(o_ref[...], g_ref[...])


def _mlp(h2, x1, w_ff1, w_ff2, g_final):
    rows = h2.shape[0]
    tm, tf = MLP_TM, MLP_TF
    est = 2 * tm * D_MODEL * 2 + 3 * tm * D_MODEL * 4 + 4 * D_MODEL * tf * 2
    return pl.pallas_call(
        _mlp_kernel,
        grid=(rows // tm, D_FF // tf),
        in_specs=[
            pl.BlockSpec((tm, D_MODEL), lambda i, f: (i, 0)),
            pl.BlockSpec((tm, D_MODEL), lambda i, f: (i, 0), pipeline_mode=pl.Buffered(1)),
            pl.BlockSpec((D_MODEL, tf), lambda i, f: (0, f)),
            pl.BlockSpec((tf, D_MODEL), lambda i, f: (f, 0)),
            pl.BlockSpec((1, D_MODEL), lambda i, f: (0, 0)),
        ],
        out_specs=pl.BlockSpec((tm, D_MODEL), lambda i, f: (i, 0)),
        out_shape=jax.ShapeDtypeStruct((rows, D_MODEL), F32),
        compiler_params=pltpu.CompilerParams(
            dimension_semantics=("parallel", "arbitrary"), vmem_limit_bytes=_vmem_limit(est)),
        name="mlp_final_norm",
    )(h2, x1, w_ff1, w_ff2, g_final)


def _trunk(x3, weights, g_final, rope):
    n_seq, seq_len, _ = x3.shape
    x = x3.reshape(n_seq * seq_len, D_MODEL)
    cos, sin = rope
    g_mix, w_main, w_qkv4, w_qkv16, g_q, g_k, w_a, w_b, w_out, g_mlp, w_ff1, w_ff2 = weights
    proj, h = _in_proj(x, g_mix, w_main)
    qkv4 = _in_proj_dil(h, w_qkv4, DIL_GROUPS[1][1], n_seq, seq_len)
    qkv16 = _in_proj_dil(h, w_qkv16, DIL_GROUPS[2][1], n_seq, seq_len)
    y_a = _dilated_mixer(proj, qkv4, qkv16, n_seq, seq_len)
    q, k, vt = _qk_prep(proj, cos, sin, g_q, g_k, n_seq, seq_len)
    y_b = _flash(q, k, vt, n_seq, seq_len)
    x1, h2 = _merge(y_a, y_b, proj, x, w_a, w_b, w_out, g_mlp)
    y = _mlp(h2, x1, w_ff1, w_ff2, g_final)
    return y.reshape(n_seq, seq_len, D_MODEL)


def _group_qkv_columns(w, group):
    q, k, v = (w[:, base + group * DIL_WIDTH: base + (group + 1) * DIL_WIDTH] for base in (W_AQ, W_AK, W_AV))
    return [q * ATTN_SCALE, k, v]


def kernel(x_prompt, x_sample, g_mix, w_in, g_q, g_k, w_branch, w_out, g_mlp, w_ff1, w_ff2, g_final):
    assert w_in.shape[0] == 1, "single-layer stack only"
    w = w_in[0]
    w_main = jnp.concatenate(_group_qkv_columns(w, 0) + [w[:, W_REST:]], axis=1).astype(BF16)
    w_qkv4 = jnp.concatenate(_group_qkv_columns(w, 1), axis=1).astype(BF16)
    w_qkv16 = jnp.concatenate(_group_qkv_columns(w, 2), axis=1).astype(BF16)
    wb = w_branch[0].astype(BF16)
    weights = (g_mix[0][None], w_main, w_qkv4, w_qkv16,
               g_q[0][None], g_k[0][None], wb[:DIL_WIDTH], wb[DIL_WIDTH:], w_out[0].astype(BF16),
               g_mlp[0][None], w_ff1[0].astype(BF16), w_ff2[0].astype(BF16))
    gf = g_final[None]
    ropes = {}
    outs = []
    for x3 in (x_prompt, x_sample):
        seq_len = x3.shape[1]
        if seq_len not in ropes:
            ropes[seq_len] = _rope_tables(seq_len)
        outs.append(_trunk(x3, weights, gf, ropes[seq_len]))
    return tuple(outs)
```

```python
import functools
import math

import jax
import jax.numpy as jnp
from jax import lax
from jax.experimental import pallas as pl
from jax.experimental.pallas import tpu as pltpu

F32 = jnp.float32
BF16 = jnp.bfloat16

D_MODEL = 2048
HEAD_DIM = 128
DIL_GROUPS = ((128, 1), (512, 4), (2048, 16))
N_DIL_GROUPS = 3
DIL_HEADS = 4
DIL_WIDTH = DIL_HEADS * HEAD_DIM
DIL_QKV_WIDTH = 3 * DIL_WIDTH
GQA_Q_HEADS = 12
GQA_KV_HEADS = 4
GQA_GROUP = GQA_Q_HEADS // GQA_KV_HEADS
GQA_Q_WIDTH = GQA_Q_HEADS * HEAD_DIM
GQA_KV_WIDTH = GQA_KV_HEADS * HEAD_DIM
GRID_W = 64
ROPE_THETA = 10000.0
D_FF = 4 * D_MODEL
W_AQ = 0
W_AK = N_DIL_GROUPS * DIL_WIDTH
W_AV = 2 * N_DIL_GROUPS * DIL_WIDTH
W_REST = 3 * N_DIL_GROUPS * DIL_WIDTH
COL_BQ = DIL_QKV_WIDTH
COL_BK = COL_BQ + GQA_Q_WIDTH
COL_BV = COL_BK + GQA_KV_WIDTH
COL_GA = COL_BV + GQA_KV_WIDTH
COL_GB = COL_GA + D_MODEL
MAIN_WIDTH = COL_GB + D_MODEL
RMS_EPS = 1e-6
NEG_INF = -1e30
ATTN_SCALE = HEAD_DIM ** -0.5
Q_PRESCALE = ATTN_SCALE * 1.4426950408889634

MIB = 1024 * 1024
V7X_VMEM_BYTES = 64 * MIB
V7X_LANES = 128
V7X_SUBLANES = 8
VMEM_TEMP_BYTES = 16 * MIB
VMEM_RESERVE_BYTES = 8 * MIB
ML_WIDTH = V7X_LANES

IN_TM, IN_TN = 1024, 2048
IN_DIL_TN = DIL_QKV_WIDTH
DIL_TQ = 128
DIL_HALO = 64
DIL_UNITS = 4
FLASH_TQ, FLASH_TK = 512, 512
FLASH_ONES_ROWS = 16
FLASH_UNROLL = 8
MERGE_TM = 256
MLP_TM, MLP_TF = 1024, 1024


def _vmem_limit(nbytes):
    return int(min(nbytes + VMEM_TEMP_BYTES, V7X_VMEM_BYTES - VMEM_RESERVE_BYTES))


def _rms(x, g):
    return x * lax.rsqrt(jnp.mean(x * x, axis=-1, keepdims=True) + RMS_EPS) * g


def _in_proj_kernel(x_ref, g_ref, w_ref, o_ref, h_ref):
    @pl.when(pl.program_id(1) == 0)
    def _():
        h_ref[...] = _rms(x_ref[...], g_ref[...]).astype(BF16)

    o_ref[...] = jnp.dot(h_ref[...], w_ref[...], preferred_element_type=F32).astype(o_ref.dtype)


def _in_proj(x, g_mix, w_main):
    rows = x.shape[0]
    tm, tn = IN_TM, IN_TN
    est = 2 * tm * D_MODEL * 4 + 2 * tm * D_MODEL * 2 + 2 * D_MODEL * tn * 2 + 2 * tm * tn * 2
    return pl.pallas_call(
        _in_proj_kernel,
        grid=(rows // tm, MAIN_WIDTH // tn),
        in_specs=[
            pl.BlockSpec((tm, D_MODEL), lambda i, j: (i, 0)),
            pl.BlockSpec((1, D_MODEL), lambda i, j: (0, 0)),
            pl.BlockSpec((D_MODEL, tn), lambda i, j: (0, j)),
        ],
        out_specs=[pl.BlockSpec((tm, tn), lambda i, j: (i, j)),
                   pl.BlockSpec((tm, D_MODEL), lambda i, j: (i, 0))],
        out_shape=[jax.ShapeDtypeStruct((rows, MAIN_WIDTH), BF16),
                   jax.ShapeDtypeStruct((rows, D_MODEL), BF16)],
        compiler_params=pltpu.CompilerParams(
            dimension_semantics=("parallel", "arbitrary"), vmem_limit_bytes=_vmem_limit(est)),
        name="in_proj",
    )(x, g_mix, w_main)


def _in_proj_dil_kernel(h_ref, w_ref, o_ref, res_ref, *, dilation):
    res = jnp.dot(h_ref[...], w_ref[...], preferred_element_type=F32)
    strips, tm, lanes = res_ref.shape
    for s in range(strips):
        res_ref[s] = res[:, s * lanes:(s + 1) * lanes]
    sub = tm // dilation
    for r in range(dilation):
        for s in range(strips):
            o_ref[r, :, s * lanes:(s + 1) * lanes] = (
                res_ref[s, pl.ds(r, sub, stride=dilation), :].astype(o_ref.dtype))


def _in_proj_dil(h, w_qkv, dilation, n_seq, seq_len):
    rows = h.shape[0]
    tm, tn = IN_TM, IN_DIL_TN
    tiles_per_seq = seq_len // tm
    est = 2 * tm * D_MODEL * 2 + 2 * D_MODEL * tn * 2 + 2 * tm * tn * 2 + tm * tn * 4
    return pl.pallas_call(
        functools.partial(_in_proj_dil_kernel, dilation=dilation),
        grid=(rows // tm, DIL_QKV_WIDTH // tn),
        in_specs=[
            pl.BlockSpec((tm, D_MODEL), lambda i, j: (i, 0)),
            pl.BlockSpec((D_MODEL, tn), lambda i, j: (0, j)),
        ],
        out_specs=pl.BlockSpec((None, dilation, tm // dilation, tn),
                               lambda i, j: (i // tiles_per_seq, 0, i % tiles_per_seq, j)),
        out_shape=jax.ShapeDtypeStruct((n_seq, dilation, seq_len // dilation, DIL_QKV_WIDTH), BF16),
        scratch_shapes=[pltpu.VMEM((tn // V7X_LANES, tm, V7X_LANES), F32)],
        compiler_params=pltpu.CompilerParams(
            dimension_semantics=("parallel", "parallel"), vmem_limit_bytes=_vmem_limit(est)),
        name=f"in_proj_dil{dilation}",
    )(h, w_qkv)


def _dilated_kernel(*refs, dilation, sub_len, slopes, has_state, emit_final):
    q_ref, kp_ref, kc_ref, kn_ref, vp_ref, vc_ref, vn_ref = refs[:7]
    refs = refs[7:]
    if has_state:
        acc_in, ml_in = refs[:2]
        refs = refs[2:]
    if emit_final:
        y_ref = refs[0]
        refs = refs[1:]
    else:
        acc_out, ml_out = refs[:2]
        refs = refs[2:]
    strided = dilation > 1
    if strided:
        acc_t, ml_t = refs

    tq, tk = DIL_TQ, DIL_TQ + 2 * DIL_HALO
    units = range(DIL_UNITS)
    heads = range(DIL_HEADS)
    cols = [slice(h * HEAD_DIM, (h + 1) * HEAD_DIM) for h in heads]
    step = pl.program_id(1)
    n_rgroups = dilation // DIL_UNITS if strided else 1

    def for_residue_group(fn):
        if n_rgroups == 1:
            fn(0)
        else:
            for g in range(n_rgroups):
                pl.when(pl.program_id(2) == g)(functools.partial(fn, g * DIL_UNITS))

    def state_rows(r0, u):
        return pl.ds(r0 + u, tq, stride=dilation)

    if strided and has_state:
        def gather(r0):
            for u in units:
                for h in heads:
                    acc_t[u, h] = acc_in[h, state_rows(r0, u), :]
                ml_t[u] = ml_in[state_rows(r0, u), :]
        for_residue_group(gather)

    def unit_rows(u):
        return slice(u * tq, (u + 1) * tq)

    def prev_acc(u, h):
        return acc_t[u, h] if strided else acc_in[h, unit_rows(u), :]

    def prev_ml(u):
        return ml_t[u] if strided else ml_in[unit_rows(u), :]

    if strided:
        def q_of(u, h):
            return q_ref[u, :, cols[h]]

        def window(p_ref, c_ref, n_ref, u, h):
            return jnp.concatenate([p_ref[u, :, cols[h]], c_ref[u, :, cols[h]], n_ref[u, :, cols[h]]], axis=0)
    else:
        def q_of(u, h):
            return q_ref[unit_rows(u), cols[h]]

        def window(p_ref, c_ref, n_ref, u, h):
            lo, hi = u * tq - DIL_HALO, u * tq + tq + DIL_HALO
            parts = []
            if lo < 0:
                parts.append(p_ref[:, cols[h]])
            parts.append(c_ref[max(lo, 0):min(hi, DIL_UNITS * tq), cols[h]])
            if hi > DIL_UNITS * tq:
                parts.append(n_ref[:, cols[h]])
            return jnp.concatenate(parts, axis=0) if len(parts) > 1 else parts[0]

    row = lax.broadcasted_iota(jnp.int32, (tq, tk), 0)
    col = lax.broadcasted_iota(jnp.int32, (tq, tk), 1)
    rel = col - DIL_HALO - row
    band = jnp.abs(rel) <= DIL_HALO
    dist = (dilation * jnp.abs(rel)).astype(F32)
    alibi = [slopes[h] * dist for h in heads]

    def mask_of(tile):
        key_idx = tile * tq - DIL_HALO + col
        return band & (key_idx >= 0) & (key_idx < sub_len)

    if strided:
        shared_mask = mask_of(step)
        masks = [shared_mask for _ in units]
    else:
        masks = [mask_of(step * DIL_UNITS + u) for u in units]
    lane = lax.broadcasted_iota(jnp.int32, (tq, ML_WIDTH), 1)
    ones = jnp.ones((tk, HEAD_DIM), BF16)

    pairs = [(u, h) for u in units for h in heads]
    scores, m_news, alphas, probs = {}, {}, {}, {}
    for u, h in pairs:
        s = lax.dot_general(q_of(u, h), window(kp_ref, kc_ref, kn_ref, u, h), (((1,), (1,)), ((), ())),
                            preferred_element_type=F32)
        scores[u, h] = jnp.where(masks[u], s - alibi[h], NEG_INF)
    ml_prevs = [prev_ml(u) for u in units] if has_state else None
    for u, h in pairs:
        m_cur = jnp.max(scores[u, h], axis=-1, keepdims=True)
        if has_state:
            m_prev = ml_prevs[u][:, h:h + 1]
            m_news[u, h] = jnp.maximum(m_prev, m_cur)
            alphas[u, h] = jnp.exp(m_prev - m_news[u, h])
        else:
            m_news[u, h] = m_cur
    for u, h in pairs:
        probs[u, h] = jnp.exp(scores[u, h] - m_news[u, h]).astype(BF16)
    ml_news = [jnp.zeros((tq, ML_WIDTH), F32) for _ in units]
    for u, h in pairs:
        v_aug = jnp.concatenate([window(vp_ref, vc_ref, vn_ref, u, h), ones], axis=1)
        both = jnp.dot(probs[u, h], v_aug, preferred_element_type=F32)
        acc, l_new = both[:, :HEAD_DIM], both[:, HEAD_DIM:]
        if has_state:
            l_new = alphas[u, h] * ml_prevs[u][:, DIL_HEADS + h:DIL_HEADS + h + 1] + l_new
            acc = alphas[u, h] * prev_acc(u, h) + acc
        if emit_final:
            y_ref[unit_rows(u), cols[h]] = (acc / l_new).astype(y_ref.dtype)
        elif strided:
            acc_t[u, h] = acc
        else:
            acc_out[h, unit_rows(u), :] = acc
        ml_news[u] = jnp.where(lane == h, m_news[u, h], ml_news[u])
        ml_news[u] = jnp.where(lane == DIL_HEADS + h, l_new, ml_news[u])
    if not emit_final:
        if strided:
            for u in units:
                ml_t[u] = ml_news[u]

            def scatter(r0):
                for u in units:
                    for h in heads:
                        acc_out[h, state_rows(r0, u), :] = acc_t[u, h]
                    ml_out[state_rows(r0, u), :] = ml_t[u]
            for_residue_group(scatter)
        else:
            for u in units:
                ml_out[unit_rows(u), :] = ml_news[u]


def _dilated_call(qkv, col0, group, state, emit_final, n_seq, seq_len):
    window, dilation = DIL_GROUPS[group]
    assert window == 2 * DIL_HALO * dilation
    assert not (emit_final and dilation > 1), "the bf16 output is written in token order only"
    sub_len = seq_len // dilation
    nt = sub_len // DIL_TQ
    rows = n_seq * seq_len
    n_heads = N_DIL_GROUPS * DIL_HEADS
    slopes = tuple(float(2.0 ** (-8.0 * (group * DIL_HEADS + h + 1) / n_heads)) for h in range(DIL_HEADS))

    halo_blocks = 2 * nt
    if dilation == 1:
        assert nt % DIL_UNITS == 0
        n_steps, n_rgroups = nt // DIL_UNITS, 1
        per_step = 2 * DIL_UNITS
        cur_block, halo_block = (None, None, DIL_UNITS * DIL_TQ, DIL_WIDTH), (None, None, DIL_HALO, DIL_WIDTH)
    else:
        assert dilation % DIL_UNITS == 0
        n_steps, n_rgroups = nt, dilation // DIL_UNITS
        per_step = 2
        cur_block, halo_block = (None, DIL_UNITS, DIL_TQ, DIL_WIDTH), (None, DIL_UNITS, DIL_HALO, DIL_WIDTH)

    def cur_spec(kind):
        return pl.BlockSpec(cur_block, lambda b, i, r: (b, r, i, col0 + kind))

    def prev_spec(kind):
        return pl.BlockSpec(halo_block, lambda b, i, r: (b, r, jnp.maximum(per_step * i - 1, 0), col0 + kind))

    def next_spec(kind):
        return pl.BlockSpec(halo_block,
                            lambda b, i, r: (b, r, jnp.minimum(per_step * (i + 1), halo_blocks - 1), col0 + kind))

    span = seq_len // n_steps
    acc_spec = pl.BlockSpec((DIL_HEADS, span, HEAD_DIM), lambda b, i, r: (0, b * n_steps + i, 0))
    ml_spec = pl.BlockSpec((span, ML_WIDTH), lambda b, i, r: (b * n_steps + i, 0))
    y_spec = pl.BlockSpec((span, DIL_WIDTH), lambda b, i, r: (b * n_steps + i, 0))

    in_specs = [cur_spec(0), prev_spec(1), cur_spec(1), next_spec(1), prev_spec(2), cur_spec(2), next_spec(2)]
    args = [qkv] * 7
    has_state = state is not None
    if has_state:
        in_specs += [acc_spec, ml_spec]
        args += list(state)
    if emit_final:
        out_specs = y_spec
        out_shape = jax.ShapeDtypeStruct((rows, DIL_WIDTH), BF16)
    else:
        out_specs = [acc_spec, ml_spec]
        out_shape = [jax.ShapeDtypeStruct((DIL_HEADS, rows, HEAD_DIM), F32),
                     jax.ShapeDtypeStruct((rows, ML_WIDTH), F32)]
    scratch = []
    if dilation > 1:
        scratch = [pltpu.VMEM((DIL_UNITS, DIL_HEADS, DIL_TQ, HEAD_DIM), F32),
                   pltpu.VMEM((DIL_UNITS, DIL_TQ, ML_WIDTH), F32)]
    kern = functools.partial(_dilated_kernel, dilation=dilation, sub_len=sub_len, slopes=slopes,
                             has_state=has_state, emit_final=emit_final)
    est = (4 * span * (DIL_WIDTH + ML_WIDTH) * 4 + 2 * 7 * DIL_UNITS * DIL_TQ * DIL_WIDTH * 2
           + 2 * DIL_UNITS * DIL_TQ * (DIL_WIDTH + ML_WIDTH) * 4)
    return pl.pallas_call(
        kern,
        grid=(n_seq, n_steps, n_rgroups),
        in_specs=in_specs,
        out_specs=out_specs,
        out_shape=out_shape,
        scratch_shapes=scratch,
        compiler_params=pltpu.CompilerParams(
            dimension_semantics=("parallel", "parallel", "arbitrary"), vmem_limit_bytes=_vmem_limit(est)),
        name=f"dilated_g{group}",
    )(*args)


def _dilated_mixer(proj, qkv4, qkv16, n_seq, seq_len):
    state = _dilated_call(qkv16, 0, 2, None, False, n_seq, seq_len)
    state = _dilated_call(qkv4, 0, 1, state, False, n_seq, seq_len)
    qkv1 = proj.reshape(n_seq, 1, seq_len, MAIN_WIDTH)
    return _dilated_call(qkv1, 0, 0, state, True, n_seq, seq_len)


def _rope_tables(seq_len):
    pos = jnp.arange(seq_len)
    row = (pos // GRID_W).astype(F32)
    col = (pos % GRID_W).astype(F32)
    half = HEAD_DIM // 2
    inv_freq = ROPE_THETA ** (-jnp.arange(0, half, 2, dtype=F32) / half)
    ang_r = row[:, None] * inv_freq
    ang_c = col[:, None] * inv_freq
    cos = jnp.concatenate([jnp.cos(ang_r), jnp.cos(ang_r), jnp.cos(ang_c), jnp.cos(ang_c)], axis=1)
    sin = jnp.concatenate([-jnp.sin(ang_r), jnp.sin(ang_r), -jnp.sin(ang_c), jnp.sin(ang_c)], axis=1)
    return cos, sin


def _swap_halves_matrix():
    quarter = HEAD_DIM // 4
    i = jnp.arange(2 * HEAD_DIM)
    partner = jnp.where((i & quarter) != 0, i - quarter, i + quarter)
    return (i[:, None] == partner[None, :]).astype(BF16)


def _prep_kernel(q_ref, k_ref, v_ref, cos_ref, sin_ref, gq_ref, gk_ref, perm_ref, qo_ref, ko_ref, vt_ref):
    cos = cos_ref[...]
    sin = sin_ref[...]
    perm = perm_ref[...]
    quarter = HEAD_DIM // 4
    lane = lax.broadcasted_iota(jnp.int32, (V7X_SUBLANES, HEAD_DIM), 1)
    upper = (lane & quarter) != 0

    def gain_tables(g_ref, scale):
        g = jnp.broadcast_to(g_ref[...], (V7X_SUBLANES, HEAD_DIM))
        g_partner = jnp.where(upper, pltpu.roll(g, quarter, 1), pltpu.roll(g, HEAD_DIM - quarter, 1))
        return cos * (g[:1] * scale), sin * (g_partner[:1] * scale)

    def norm_rope(src_ref, dst_ref, n_heads, tables):
        cos_g, sin_g = tables
        for pair in range(n_heads // 2):
            ps = slice(2 * pair * HEAD_DIM, 2 * (pair + 1) * HEAD_DIM)
            x2 = src_ref[:, ps]
            swapped = jnp.dot(x2, perm, preferred_element_type=F32)
            xf = x2.astype(F32)
            for half in range(2):
                hs = slice(half * HEAD_DIM, (half + 1) * HEAD_DIM)
                x = xf[:, hs]
                inv = lax.rsqrt(jnp.mean(x * x, axis=-1, keepdims=True) + RMS_EPS)
                out = (x * cos_g + swapped[:, hs] * sin_g) * inv
                dst_ref[:, (2 * pair + half) * HEAD_DIM:(2 * pair + half + 1) * HEAD_DIM] = out.astype(dst_ref.dtype)

    norm_rope(q_ref, qo_ref, GQA_Q_HEADS, gain_tables(gq_ref, Q_PRESCALE))
    norm_rope(k_ref, ko_ref, GQA_KV_HEADS, gain_tables(gk_ref, 1.0))
    for h in range(GQA_KV_HEADS):
        hs = slice(h * HEAD_DIM, (h + 1) * HEAD_DIM)
        vt_ref[h, 0] = v_ref[:, hs].astype(F32).T.astype(vt_ref.dtype)


def _qk_prep(proj, cos, sin, g_q, g_k, n_seq, seq_len):
    rows = proj.shape[0]
    tm = FLASH_TK
    tpos = seq_len // tm
    return pl.pallas_call(
        _prep_kernel,
        grid=(rows // tm,),
        in_specs=[
            pl.BlockSpec((tm, GQA_Q_WIDTH), lambda i: (i, COL_BQ // GQA_Q_WIDTH)),
            pl.BlockSpec((tm, GQA_KV_WIDTH), lambda i: (i, COL_BK // GQA_KV_WIDTH)),
            pl.BlockSpec((tm, GQA_KV_WIDTH), lambda i: (i, COL_BV // GQA_KV_WIDTH)),
            pl.BlockSpec((tm, HEAD_DIM), lambda i: (i % tpos, 0)),
            pl.BlockSpec((tm, HEAD_DIM), lambda i: (i % tpos, 0)),
            pl.BlockSpec((1, HEAD_DIM), lambda i: (0, 0)),
            pl.BlockSpec((1, HEAD_DIM), lambda i: (0, 0)),
            pl.BlockSpec((2 * HEAD_DIM, 2 * HEAD_DIM), lambda i: (0, 0)),
        ],
        out_specs=[pl.BlockSpec((tm, GQA_Q_WIDTH), lambda i: (i, 0)),
                   pl.BlockSpec((tm, GQA_KV_WIDTH), lambda i: (i, 0)),
                   pl.BlockSpec((None, GQA_KV_HEADS, 1, HEAD_DIM, tm), lambda i: (i // tpos, 0, i % tpos, 0, 0))],
        out_shape=[jax.ShapeDtypeStruct((rows, GQA_Q_WIDTH), BF16),
                   jax.ShapeDtypeStruct((rows, GQA_KV_WIDTH), BF16),
                   jax.ShapeDtypeStruct((n_seq, GQA_KV_HEADS, tpos, HEAD_DIM, tm), BF16)],
        compiler_params=pltpu.CompilerParams(
            dimension_semantics=("parallel",), vmem_limit_bytes=_vmem_limit(16 * MIB)),
        name="qk_prep",
    )(proj, proj, proj, cos, sin, g_q, g_k, _swap_halves_matrix())


def _flash_kernel(q_ref, k_ref, vt_ref, o_ref, st_sc, mx_sc, m_sc, acc_sc, *, n_chunks):
    tk = FLASH_TK
    m_sc[...] = jnp.full(m_sc.shape, NEG_INF, F32)
    acc_sc[...] = jnp.zeros(acc_sc.shape, F32)
    ones = jnp.ones((FLASH_ONES_ROWS, tk), BF16)

    def scores(c, slot, g):
        k = k_ref[pl.ds(pl.multiple_of(c * tk, tk), tk), :]
        q = q_ref[:, g * HEAD_DIM:(g + 1) * HEAD_DIM]
        st = lax.dot_general(k, q, (((1,), (1,)), ((), ())), preferred_element_type=F32)
        st_sc[slot, g] = st
        mx_sc[slot, g] = jnp.max(st, axis=0, keepdims=True)

    def consume(c, slot, g):
        vt = jnp.concatenate([vt_ref[c], ones], axis=0)
        m_prev = m_sc[g]
        m_new = jnp.maximum(m_prev, mx_sc[slot, g])
        alpha = jnp.exp2(m_prev - m_new)
        p = jnp.exp2(st_sc[slot, g] - m_new).astype(BF16)
        acc_sc[g] = alpha * acc_sc[g] + jnp.dot(vt, p, preferred_element_type=F32)
        m_sc[g] = m_new

    def step(c, slot, has_next):
        for g in range(GQA_GROUP):
            if has_next:
                scores(c + 1, 1 - slot, g)
            consume(c, slot, g)

    for g in range(GQA_GROUP):
        scores(0, 0, g)

    unroll = math.gcd(FLASH_UNROLL, n_chunks)

    def group(i, carry):
        for u in range(unroll):
            step(unroll * i + u, u % 2, True)
        return carry

    lax.fori_loop(0, n_chunks // unroll - 1, group, 0)
    for u in range(unroll):
        step(n_chunks - unroll + u, u % 2, u < unroll - 1)
    for g in range(GQA_GROUP):
        acc = acc_sc[g]
        o_t = acc[:HEAD_DIM] / acc[HEAD_DIM:HEAD_DIM + 1]
        o_ref[:, g * HEAD_DIM:(g + 1) * HEAD_DIM] = o_t.T.astype(o_ref.dtype)


def _flash(q, k, vt, n_seq, seq_len):
    rows = q.shape[0]
    tq, tk = FLASH_TQ, FLASH_TK
    nq = seq_len // tq
    n_chunks = seq_len // tk
    gw = GQA_GROUP * HEAD_DIM
    acc_rows = HEAD_DIM + FLASH_ONES_ROWS
    assert math.gcd(FLASH_UNROLL, n_chunks) % 2 == 0
    est = (2 * 2 * seq_len * HEAD_DIM * 2 + 4 * tq * gw * 2 + GQA_GROUP * (acc_rows + 8 * 3) * tq * 4
           + (2 * GQA_GROUP + 3) * tk * tq * 4)
    return pl.pallas_call(
        functools.partial(_flash_kernel, n_chunks=n_chunks),
        grid=(n_seq, GQA_KV_HEADS, nq),
        in_specs=[
            pl.BlockSpec((tq, gw), lambda b, h, i: (b * nq + i, h)),
            pl.BlockSpec((seq_len, HEAD_DIM), lambda b, h, i: (b, h)),
            pl.BlockSpec((None, None, n_chunks, HEAD_DIM, tk), lambda b, h, i: (b, h, 0, 0, 0)),
        ],
        out_specs=pl.BlockSpec((tq, gw), lambda b, h, i: (b * nq + i, h)),
        out_shape=jax.ShapeDtypeStruct((rows, GQA_Q_WIDTH), BF16),
        scratch_shapes=[pltpu.VMEM((2, GQA_GROUP, tk, tq), F32),
                        pltpu.VMEM((2, GQA_GROUP, 1, tq), F32),
                        pltpu.VMEM((GQA_GROUP, 1, tq), F32),
                        pltpu.VMEM((GQA_GROUP, acc_rows, tq), F32)],
        compiler_params=pltpu.CompilerParams(
            dimension_semantics=("parallel", "parallel", "arbitrary"), vmem_limit_bytes=_vmem_limit(est)),
        name="gqa_flash",
    )(q, k, vt)


def _sigmoid(x):
    return 1.0 / (1.0 + jnp.exp(-x))


def _merge_kernel(ya_ref, yb_ref, ga0_ref, ga1_ref, gb0_ref, gb1_ref, x_ref, wa_ref, wb_ref, wo_ref, g_ref,
                  x1_ref, h2_ref):
    half = D_MODEL // 2
    ya = ya_ref[...]
    yb = yb_ref[...]
    x1 = x_ref[...]
    for n, (ga_ref, gb_ref) in enumerate(((ga0_ref, gb0_ref), (ga1_ref, gb1_ref))):
        cs = slice(n * half, (n + 1) * half)
        o_a = jnp.dot(ya, wa_ref[:, cs], preferred_element_type=F32)
        o_b = jnp.dot(yb, wb_ref[:, cs], preferred_element_type=F32)
        merged = _sigmoid(ga_ref[...].astype(F32)) * o_a + _sigmoid(gb_ref[...].astype(F32)) * o_b
        x1 = x1 + jnp.dot(merged.astype(BF16), wo_ref[cs, :], preferred_element_type=F32)
    x1_ref[...] = x1
    h2_ref[...] = _rms(x1, g_ref[...]).astype(h2_ref.dtype)


def _merge(y_a, y_b, proj, x, w_a, w_b, w_out, g_mlp):
    rows = x.shape[0]
    tm = MERGE_TM
    half = D_MODEL // 2
    once = pl.Buffered(1)
    est = (2 * D_MODEL * D_MODEL * 2 + 2 * tm * (DIL_WIDTH + GQA_Q_WIDTH) * 2 + 2 * 4 * tm * half * 2
           + 4 * tm * D_MODEL * 4 + 2 * tm * D_MODEL * 2 + 6 * tm * D_MODEL * 4)
    return pl.pallas_call(
        _merge_kernel,
        grid=(rows // tm,),
        in_specs=[
            pl.BlockSpec((tm, DIL_WIDTH), lambda i: (i, 0)),
            pl.BlockSpec((tm, GQA_Q_WIDTH), lambda i: (i, 0)),
            pl.BlockSpec((tm, half), lambda i: (i, COL_GA // half)),
            pl.BlockSpec((tm, half), lambda i: (i, COL_GA // half + 1)),
            pl.BlockSpec((tm, half), lambda i: (i, COL_GB // half)),
            pl.BlockSpec((tm, half), lambda i: (i, COL_GB // half + 1)),
            pl.BlockSpec((tm, D_MODEL), lambda i: (i, 0)),
            pl.BlockSpec((DIL_WIDTH, D_MODEL), lambda i: (0, 0), pipeline_mode=once),
            pl.BlockSpec((GQA_Q_WIDTH, D_MODEL), lambda i: (0, 0), pipeline_mode=once),
            pl.BlockSpec((D_MODEL, D_MODEL), lambda i: (0, 0), pipeline_mode=once),
            pl.BlockSpec((1, D_MODEL), lambda i: (0, 0)),
        ],
        out_specs=[pl.BlockSpec((tm, D_MODEL), lambda i: (i, 0)),
                   pl.BlockSpec((tm, D_MODEL), lambda i: (i, 0))],
        out_shape=[jax.ShapeDtypeStruct((rows, D_MODEL), F32),
                   jax.ShapeDtypeStruct((rows, D_MODEL), BF16)],
        compiler_params=pltpu.CompilerParams(
            dimension_semantics=("parallel",), vmem_limit_bytes=_vmem_limit(est)),
        name="merge_out_proj",
    )(y_a, y_b, proj, proj, proj, proj, x, w_a, w_b, w_out, g_mlp)


def _mlp_kernel(h2_ref, x1_ref, w1_ref, w2_ref, g_ref, o_ref):
    f = pl.program_id(1)

    @pl.when(f == 0)
    def _():
        o_ref[...] = x1_ref[...]

    u = jnp.maximum(jnp.dot(h2_ref[...], w1_ref[...], preferred_element_type=F32), 0.0)
    o_ref[...] += jnp.dot((u * u).astype(BF16), w2_ref[...], preferred_element_type=F32)

    @pl.when(f == pl.num_programs(1) - 1)
    def _():
        o_ref[...] = _rms(o_ref[...], g_ref[...])


def _mlp(h2, x1, w_ff1, w_ff2, g_final):
    rows = h2.shape[0]
    tm, tf = MLP_TM, MLP_TF
    est = 2 * tm * D_MODEL * 2 + 3 * tm * D_MODEL * 4 + 4 * D_MODEL * tf * 2
    return pl.pallas_call(
        _mlp_kernel,
        grid=(rows // tm, D_FF // tf),
        in_specs=[
            pl.BlockSpec((tm, D_MODEL), lambda i, f: (i, 0)),
            pl.BlockSpec((tm, D_MODEL), lambda i, f: (i, 0), pipeline_mode=pl.Buffered(1)),
            pl.BlockSpec((D_MODEL, tf), lambda i, f: (0, f)),
            pl.BlockSpec((tf, D_MODEL), lambda i, f: (f, 0)),
            pl.BlockSpec((1, D_MODEL), lambda i, f: (0, 0)),
        ],
        out_specs=pl.BlockSpec((tm, D_MODEL), lambda i, f: (i, 0)),
        out_shape=jax.ShapeDtypeStruct((rows, D_MODEL), F32),
        compiler_params=pltpu.CompilerParams(
            dimension_semantics=("parallel", "arbitrary"), vmem_limit_bytes=_vmem_limit(est)),
        name="mlp_final_norm",
    )(h2, x1, w_ff1, w_ff2, g_final)


def _trunk(x3, weights, g_final, rope):
    n_seq, seq_len, _ = x3.shape
    x = x3.reshape(n_seq * seq_len, D_MODEL)
    cos, sin = rope
    g_mix, w_main, w_qkv4, w_qkv16, g_q, g_k, w_a, w_b, w_out, g_mlp, w_ff1, w_ff2 = weights
    proj, h = _in_proj(x, g_mix, w_main)
    qkv4 = _in_proj_dil(h, w_qkv4, DIL_GROUPS[1][1], n_seq, seq_len)
    qkv16 = _in_proj_dil(h, w_qkv16, DIL_GROUPS[2][1], n_seq, seq_len)
    y_a = _dilated_mixer(proj, qkv4, qkv16, n_seq, seq_len)
    q, k, vt = _qk_prep(proj, cos, sin, g_q, g_k, n_seq, seq_len)
    y_b = _flash(q, k, vt, n_seq, seq_len)
    x1, h2 = _merge(y_a, y_b, proj, x, w_a, w_b, w_out, g_mlp)
    y = _mlp(h2, x1, w_ff1, w_ff2, g_final)
    return y.reshape(n_seq, seq_len, D_MODEL)


def _group_qkv_columns(w, group):
    q, k, v = (w[:, base + group * DIL_WIDTH: base + (group + 1) * DIL_WIDTH] for base in (W_AQ, W_AK, W_AV))
    return [q * ATTN_SCALE, k, v]


def kernel(x_prompt, x_sample, g_mix, w_in, g_q, g_k, w_branch, w_out, g_mlp, w_ff1, w_ff2, g_final):
    assert w_in.shape[0] == 1, "single-layer stack only"
    w = w_in[0]
    w_main = jnp.concatenate(_group_qkv_columns(w, 0) + [w[:, W_REST:]], axis=1).astype(BF16)
    w_qkv4 = jnp.concatenate(_group_qkv_columns(w, 1), axis=1).astype(BF16)
    w_qkv16 = jnp.concatenate(_group_qkv_columns(w, 2), axis=1).astype(BF16)
    wb = w_branch[0].astype(BF16)
    weights = (g_mix[0][None], w_main, w_qkv4, w_qkv16,
               g_q[0][None], g_k[0][None], wb[:DIL_WIDTH], wb[DIL_WIDTH:], w_out[0].astype(BF16),
               g_mlp[0][None], w_ff1[0].astype(BF16), w_ff2[0].astype(BF16))
    gf = g_final[None]
    ropes = {}
    outs = []
    for x3 in (x_prompt, x_sample):
        seq_len = x3.shape[1]
        if seq_len not in ropes:
            ropes[seq_len] = _rope_tables(seq_len)
        outs.append(_trunk(x3, weights, gf, ropes[seq_len]))
    return tuple(outs)
```

```python
import functools
import math

import jax
import jax.numpy as jnp
from jax import lax
from jax.experimental import pallas as pl
from jax.experimental.pallas import tpu as pltpu

F32 = jnp.float32
BF16 = jnp.bfloat16

D_MODEL = 2048
HEAD_DIM = 128
DIL_GROUPS = ((128, 1), (512, 4), (2048, 16))
N_DIL_GROUPS = 3
DIL_HEADS = 4
DIL_WIDTH = DIL_HEADS * HEAD_DIM
DIL_QKV_WIDTH = 3 * DIL_WIDTH
GQA_Q_HEADS = 12
GQA_KV_HEADS = 4
GQA_GROUP = GQA_Q_HEADS // GQA_KV_HEADS
GQA_Q_WIDTH = GQA_Q_HEADS * HEAD_DIM
GQA_KV_WIDTH = GQA_KV_HEADS * HEAD_DIM
GRID_W = 64
ROPE_THETA = 10000.0
D_FF = 4 * D_MODEL
W_AQ = 0
W_AK = N_DIL_GROUPS * DIL_WIDTH
W_AV = 2 * N_DIL_GROUPS * DIL_WIDTH
W_REST = 3 * N_DIL_GROUPS * DIL_WIDTH
COL_BQ = DIL_QKV_WIDTH
COL_BK = COL_BQ + GQA_Q_WIDTH
COL_BV = COL_BK + GQA_KV_WIDTH
COL_GA = COL_BV + GQA_KV_WIDTH
COL_GB = COL_GA + D_MODEL
MAIN_WIDTH = COL_GB + D_MODEL
RMS_EPS = 1e-6
NEG_INF = -1e30
ATTN_SCALE = HEAD_DIM ** -0.5
Q_PRESCALE = ATTN_SCALE * 1.4426950408889634

MIB = 1024 * 1024
V7X_VMEM_BYTES = 64 * MIB
V7X_LANES = 128
V7X_SUBLANES = 8
VMEM_TEMP_BYTES = 16 * MIB
VMEM_RESERVE_BYTES = 8 * MIB
ML_WIDTH = V7X_LANES

IN_TM, IN_TN = 1024, 2048
IN_DIL_TN = DIL_QKV_WIDTH
DIL_TQ = 128
DIL_HALO = 64
DIL_UNITS = 4
DIL_LOOKAHEAD = 12
FLASH_TQ, FLASH_TK = 512, 512
FLASH_ONES_ROWS = 16
FLASH_UNROLL = 8
MERGE_TM = 256
MLP_TM, MLP_TF = 1024, 1024


def _vmem_limit(nbytes):
    return int(min(nbytes + VMEM_TEMP_BYTES, V7X_VMEM_BYTES - VMEM_RESERVE_BYTES))


def _rms(x, g):
    return x * lax.rsqrt(jnp.mean(x * x, axis=-1, keepdims=True) + RMS_EPS) * g


def _in_proj_kernel(x_ref, g_ref, w_ref, o_ref, h_ref):
    @pl.when(pl.program_id(1) == 0)
    def _():
        h_ref[...] = _rms(x_ref[...], g_ref[...]).astype(BF16)

    o_ref[...] = jnp.dot(h_ref[...], w_ref[...], preferred_element_type=F32).astype(o_ref.dtype)


def _in_proj(x, g_mix, w_main):
    rows = x.shape[0]
    tm, tn = IN_TM, IN_TN
    est = 2 * tm * D_MODEL * 4 + 2 * tm * D_MODEL * 2 + 2 * D_MODEL * tn * 2 + 2 * tm * tn * 2
    return pl.pallas_call(
        _in_proj_kernel,
        grid=(rows // tm, MAIN_WIDTH // tn),
        in_specs=[
            pl.BlockSpec((tm, D_MODEL), lambda i, j: (i, 0)),
            pl.BlockSpec((1, D_MODEL), lambda i, j: (0, 0)),
            pl.BlockSpec((D_MODEL, tn), lambda i, j: (0, j)),
        ],
        out_specs=[pl.BlockSpec((tm, tn), lambda i, j: (i, j)),
                   pl.BlockSpec((tm, D_MODEL), lambda i, j: (i, 0))],
        out_shape=[jax.ShapeDtypeStruct((rows, MAIN_WIDTH), BF16),
                   jax.ShapeDtypeStruct((rows, D_MODEL), BF16)],
        compiler_params=pltpu.CompilerParams(
            dimension_semantics=("parallel", "arbitrary"), vmem_limit_bytes=_vmem_limit(est)),
        name="in_proj",
    )(x, g_mix, w_main)


def _in_proj_dil_kernel(h_ref, w_ref, o_ref, res_ref, *, dilation):
    res = jnp.dot(h_ref[...], w_ref[...], preferred_element_type=F32)
    strips, tm, lanes = res_ref.shape
    for s in range(strips):
        res_ref[s] = res[:, s * lanes:(s + 1) * lanes]
    sub = tm // dilation
    for r in range(dilation):
        for s in range(strips):
            o_ref[r, :, s * lanes:(s + 1) * lanes] = (
                res_ref[s, pl.ds(r, sub, stride=dilation), :].astype(o_ref.dtype))


def _in_proj_dil(h, w_qkv, dilation, n_seq, seq_len):
    rows = h.shape[0]
    tm, tn = IN_TM, IN_DIL_TN
    tiles_per_seq = seq_len // tm
    est = 2 * tm * D_MODEL * 2 + 2 * D_MODEL * tn * 2 + 2 * tm * tn * 2 + tm * tn * 4
    return pl.pallas_call(
        functools.partial(_in_proj_dil_kernel, dilation=dilation),
        grid=(rows // tm, DIL_QKV_WIDTH // tn),
        in_specs=[
            pl.BlockSpec((tm, D_MODEL), lambda i, j: (i, 0)),
            pl.BlockSpec((D_MODEL, tn), lambda i, j: (0, j)),
        ],
        out_specs=pl.BlockSpec((None, dilation, tm // dilation, tn),
                               lambda i, j: (i // tiles_per_seq, 0, i % tiles_per_seq, j)),
        out_shape=jax.ShapeDtypeStruct((n_seq, dilation, seq_len // dilation, DIL_QKV_WIDTH), BF16),
        scratch_shapes=[pltpu.VMEM((tn // V7X_LANES, tm, V7X_LANES), F32)],
        compiler_params=pltpu.CompilerParams(
            dimension_semantics=("parallel", "parallel"), vmem_limit_bytes=_vmem_limit(est)),
        name=f"in_proj_dil{dilation}",
    )(h, w_qkv)


def _dilated_kernel(*refs, dilation, sub_len, slopes, has_state, emit_final, state_in_by_unit, state_out_by_unit):
    q_ref, kp_ref, kc_ref, kn_ref, vp_ref, vc_ref, vn_ref = refs[:7]
    refs = refs[7:]
    if has_state:
        acc_in, ml_in = refs[:2]
        refs = refs[2:]
    if emit_final:
        y_ref = refs[0]
        refs = refs[1:]
    else:
        acc_out, ml_out = refs[:2]
        refs = refs[2:]
    strided = dilation > 1
    if strided:
        acc_t, ml_t = refs

    tq, tk = DIL_TQ, DIL_TQ + 2 * DIL_HALO
    units = range(DIL_UNITS)
    heads = range(DIL_HEADS)
    cols = [slice(h * HEAD_DIM, (h + 1) * HEAD_DIM) for h in heads]
    step = pl.program_id(1)
    n_rgroups = dilation // DIL_UNITS if strided else 1

    def for_residue_group(fn):
        if n_rgroups == 1:
            fn(0)
        else:
            for g in range(n_rgroups):
                pl.when(pl.program_id(2) == g)(functools.partial(fn, g * DIL_UNITS))

    def state_rows(r0, u):
        return pl.ds(r0 + u, tq, stride=dilation)

    if strided and has_state and not state_in_by_unit:
        def gather(r0):
            for u in units:
                for h in heads:
                    acc_t[u, h] = acc_in[h, state_rows(r0, u), :]
                ml_t[u] = ml_in[state_rows(r0, u), :]
        for_residue_group(gather)

    def unit_rows(u):
        return slice(u * tq, (u + 1) * tq)

    def prev_acc(u, h):
        if state_in_by_unit:
            return acc_in[h, u]
        return acc_t[u, h] if strided else acc_in[h, unit_rows(u), :]

    def prev_ml(u):
        if state_in_by_unit:
            return ml_in[u]
        return ml_t[u] if strided else ml_in[unit_rows(u), :]

    if strided:
        def q_of(u, h):
            return q_ref[u, :, cols[h]]

        def window(p_ref, c_ref, n_ref, u, h):
            return jnp.concatenate([p_ref[u, :, cols[h]], c_ref[u, :, cols[h]], n_ref[u, :, cols[h]]], axis=0)
    else:
        def q_of(u, h):
            return q_ref[unit_rows(u), cols[h]]

        def window(p_ref, c_ref, n_ref, u, h):
            lo, hi = u * tq - DIL_HALO, u * tq + tq + DIL_HALO
            parts = []
            if lo < 0:
                parts.append(p_ref[:, cols[h]])
            parts.append(c_ref[max(lo, 0):min(hi, DIL_UNITS * tq), cols[h]])
            if hi > DIL_UNITS * tq:
                parts.append(n_ref[:, cols[h]])
            return jnp.concatenate(parts, axis=0) if len(parts) > 1 else parts[0]

    row = lax.broadcasted_iota(jnp.int32, (tq, tk), 0)
    col = lax.broadcasted_iota(jnp.int32, (tq, tk), 1)
    rel = col - DIL_HALO - row
    band = jnp.abs(rel) <= DIL_HALO
    dist = (dilation * jnp.abs(rel)).astype(F32)
    alibi = [slopes[h] * dist for h in heads]

    def mask_of(tile):
        key_idx = tile * tq - DIL_HALO + col
        return band & (key_idx >= 0) & (key_idx < sub_len)

    if strided:
        shared_mask = mask_of(step)
        masks = [shared_mask for _ in units]
    else:
        masks = [mask_of(step * DIL_UNITS + u) for u in units]
    lane = lax.broadcasted_iota(jnp.int32, (tq, ML_WIDTH), 1)
    ones = jnp.ones((tk, HEAD_DIM), BF16)

    scores, m_news, alphas, probs = {}, {}, {}, {}
    ml_prevs = [prev_ml(u) for u in units] if has_state else None
    ml_news = [jnp.zeros((tq, ML_WIDTH), F32) for _ in units]

    def score_pass(u, h):
        s = lax.dot_general(q_of(u, h), window(kp_ref, kc_ref, kn_ref, u, h), (((1,), (1,)), ((), ())),
                            preferred_element_type=F32)
        scores[u, h] = jnp.where(masks[u], s - alibi[h], NEG_INF)

    def max_pass(u, h):
        m_cur = jnp.max(scores[u, h], axis=-1, keepdims=True)
        if has_state:
            m_prev = ml_prevs[u][:, h:h + 1]
            m_news[u, h] = jnp.maximum(m_prev, m_cur)
            alphas[u, h] = jnp.exp(m_prev - m_news[u, h])
        else:
            m_news[u, h] = m_cur

    def exp_pass(u, h):
        probs[u, h] = jnp.exp(scores[u, h] - m_news[u, h]).astype(BF16)

    def value_pass(u, h):
        v_aug = jnp.concatenate([window(vp_ref, vc_ref, vn_ref, u, h), ones], axis=1)
        both = jnp.dot(probs[u, h], v_aug, preferred_element_type=F32)
        acc, l_new = both[:, :HEAD_DIM], both[:, HEAD_DIM:]
        if has_state:
            l_new = alphas[u, h] * ml_prevs[u][:, DIL_HEADS + h:DIL_HEADS + h + 1] + l_new
            acc = alphas[u, h] * prev_acc(u, h) + acc
        if emit_final:
            y_ref[unit_rows(u), cols[h]] = (acc / l_new).astype(y_ref.dtype)
        elif strided:
            acc_t[u, h] = acc
        else:
            acc_out[h, unit_rows(u), :] = acc
        ml_news[u] = jnp.where(lane == h, m_news[u, h], ml_news[u])
        ml_news[u] = jnp.where(lane == DIL_HEADS + h, l_new, ml_news[u])

    pairs = [(u, h) for u in units for h in heads]
    lookahead = DIL_LOOKAHEAD
    for pair in pairs[:lookahead]:
        score_pass(*pair)
    for n, pair in enumerate(pairs):
        if n + lookahead < len(pairs):
            score_pass(*pairs[n + lookahead])
        max_pass(*pair)
        exp_pass(*pair)
        value_pass(*pair)
    if not emit_final:
        if strided:
            for u in units:
                ml_t[u] = ml_news[u]

            def scatter(r0):
                for u in units:
                    if state_out_by_unit:
                        rows = pl.ds(r0 // DIL_UNITS, tq, stride=dilation // DIL_UNITS)
                        for h in heads:
                            acc_out[h, u, rows, :] = acc_t[u, h]
                        ml_out[u, rows, :] = ml_t[u]
                    else:
                        for h in heads:
                            acc_out[h, state_rows(r0, u), :] = acc_t[u, h]
                        ml_out[state_rows(r0, u), :] = ml_t[u]
            for_residue_group(scatter)
        else:
            for u in units:
                ml_out[unit_rows(u), :] = ml_news[u]


def _dilated_call(qkv, col0, group, state, emit_final, n_seq, seq_len,
                  state_in_by_unit=False, state_out_by_unit=False):
    window, dilation = DIL_GROUPS[group]
    assert window == 2 * DIL_HALO * dilation
    assert not (emit_final and dilation > 1), "the bf16 output is written in token order only"
    sub_len = seq_len // dilation
    nt = sub_len // DIL_TQ
    rows = n_seq * seq_len
    n_heads = N_DIL_GROUPS * DIL_HEADS
    slopes = tuple(float(2.0 ** (-8.0 * (group * DIL_HEADS + h + 1) / n_heads)) for h in range(DIL_HEADS))

    halo_blocks = 2 * nt
    if dilation == 1:
        assert nt % DIL_UNITS == 0
        n_steps, n_rgroups = nt // DIL_UNITS, 1
        per_step = 2 * DIL_UNITS
        cur_block, halo_block = (None, None, DIL_UNITS * DIL_TQ, DIL_WIDTH), (None, None, DIL_HALO, DIL_WIDTH)
    else:
        assert dilation % DIL_UNITS == 0
        n_steps, n_rgroups = nt, dilation // DIL_UNITS
        per_step = 2
        cur_block, halo_block = (None, DIL_UNITS, DIL_TQ, DIL_WIDTH), (None, DIL_UNITS, DIL_HALO, DIL_WIDTH)

    def cur_spec(kind):
        return pl.BlockSpec(cur_block, lambda b, i, r: (b, r, i, col0 + kind))

    def prev_spec(kind):
        return pl.BlockSpec(halo_block, lambda b, i, r: (b, r, jnp.maximum(per_step * i - 1, 0), col0 + kind))

    def next_spec(kind):
        return pl.BlockSpec(halo_block,
                            lambda b, i, r: (b, r, jnp.minimum(per_step * (i + 1), halo_blocks - 1), col0 + kind))

    span = seq_len // n_steps
    acc_spec = pl.BlockSpec((DIL_HEADS, span, HEAD_DIM), lambda b, i, r: (0, b * n_steps + i, 0))
    ml_spec = pl.BlockSpec((span, ML_WIDTH), lambda b, i, r: (b * n_steps + i, 0))
    y_spec = pl.BlockSpec((span, DIL_WIDTH), lambda b, i, r: (b * n_steps + i, 0))

    unit_len = seq_len // DIL_UNITS
    by_unit_rows = unit_len // n_steps
    acc_unit_spec = pl.BlockSpec((DIL_HEADS, None, DIL_UNITS, by_unit_rows, HEAD_DIM),
                                 lambda b, i, r: (0, b, 0, i, 0))
    ml_unit_spec = pl.BlockSpec((None, DIL_UNITS, by_unit_rows, ML_WIDTH), lambda b, i, r: (b, 0, i, 0))
    if state_in_by_unit:
        assert dilation == DIL_UNITS and by_unit_rows == DIL_TQ
    if state_out_by_unit:
        assert dilation > DIL_UNITS and dilation % DIL_UNITS == 0 and not emit_final

    in_specs = [cur_spec(0), prev_spec(1), cur_spec(1), next_spec(1), prev_spec(2), cur_spec(2), next_spec(2)]
    args = [qkv] * 7
    has_state = state is not None
    if has_state:
        in_specs += [acc_unit_spec, ml_unit_spec] if state_in_by_unit else [acc_spec, ml_spec]
        args += list(state)
    if emit_final:
        out_specs = y_spec
        out_shape = jax.ShapeDtypeStruct((rows, DIL_WIDTH), BF16)
    elif state_out_by_unit:
        out_specs = [acc_unit_spec, ml_unit_spec]
        out_shape = [jax.ShapeDtypeStruct((DIL_HEADS, n_seq, DIL_UNITS, unit_len, HEAD_DIM), F32),
                     jax.ShapeDtypeStruct((n_seq, DIL_UNITS, unit_len, ML_WIDTH), F32)]
    else:
        out_specs = [acc_spec, ml_spec]
        out_shape = [jax.ShapeDtypeStruct((DIL_HEADS, rows, HEAD_DIM), F32),
                     jax.ShapeDtypeStruct((rows, ML_WIDTH), F32)]
    scratch = []
    if dilation > 1:
        scratch = [pltpu.VMEM((DIL_UNITS, DIL_HEADS, DIL_TQ, HEAD_DIM), F32),
                   pltpu.VMEM((DIL_UNITS, DIL_TQ, ML_WIDTH), F32)]
    kern = functools.partial(_dilated_kernel, dilation=dilation, sub_len=sub_len, slopes=slopes,
                             has_state=has_state, emit_final=emit_final,
                             state_in_by_unit=state_in_by_unit, state_out_by_unit=state_out_by_unit)
    est = (4 * span * (DIL_WIDTH + ML_WIDTH) * 4 + 2 * 7 * DIL_UNITS * DIL_TQ * DIL_WIDTH * 2
           + 2 * DIL_UNITS * DIL_TQ * (DIL_WIDTH + ML_WIDTH) * 4)
    return pl.pallas_call(
        kern,
        grid=(n_seq, n_steps, n_rgroups),
        in_specs=in_specs,
        out_specs=out_specs,
        out_shape=out_shape,
        scratch_shapes=scratch,
        compiler_params=pltpu.CompilerParams(
            dimension_semantics=("parallel", "parallel", "arbitrary"), vmem_limit_bytes=_vmem_limit(est)),
        name=f"dilated_g{group}",
    )(*args)


def _dilated_mixer(proj, qkv4, qkv16, n_seq, seq_len):
    state = _dilated_call(qkv16, 0, 2, None, False, n_seq, seq_len, state_out_by_unit=True)
    state = _dilated_call(qkv4, 0, 1, state, False, n_seq, seq_len, state_in_by_unit=True)
    qkv1 = proj.reshape(n_seq, 1, seq_len, MAIN_WIDTH)
    return _dilated_call(qkv1, 0, 0, state, True, n_seq, seq_len)


def _rope_tables(seq_len):
    pos = jnp.arange(seq_len)
    row = (pos // GRID_W).astype(F32)
    col = (pos % GRID_W).astype(F32)
    half = HEAD_DIM // 2
    inv_freq = ROPE_THETA ** (-jnp.arange(0, half, 2, dtype=F32) / half)
    ang_r = row[:, None] * inv_freq
    ang_c = col[:, None] * inv_freq
    cos = jnp.concatenate([jnp.cos(ang_r), jnp.cos(ang_r), jnp.cos(ang_c), jnp.cos(ang_c)], axis=1)
    sin = jnp.concatenate([-jnp.sin(ang_r), jnp.sin(ang_r), -jnp.sin(ang_c), jnp.sin(ang_c)], axis=1)
    return cos, sin


def _swap_halves_matrix():
    quarter = HEAD_DIM // 4
    i = jnp.arange(2 * HEAD_DIM)
    partner = jnp.where((i & quarter) != 0, i - quarter, i + quarter)
    return (i[:, None] == partner[None, :]).astype(BF16)


def _prep_kernel(q_ref, k_ref, v_ref, cos_ref, sin_ref, gq_ref, gk_ref, perm_ref, qo_ref, ko_ref, vt_ref):
    cos = cos_ref[...]
    sin = sin_ref[...]
    perm = perm_ref[...]
    quarter = HEAD_DIM // 4
    lane = lax.broadcasted_iota(jnp.int32, (V7X_SUBLANES, HEAD_DIM), 1)
    upper = (lane & quarter) != 0

    def gain_tables(g_ref, scale):
        g = jnp.broadcast_to(g_ref[...], (V7X_SUBLANES, HEAD_DIM))
        g_partner = jnp.where(upper, pltpu.roll(g, quarter, 1), pltpu.roll(g, HEAD_DIM - quarter, 1))
        return cos * (g[:1] * scale), sin * (g_partner[:1] * scale)

    def norm_rope(src_ref, dst_ref, n_heads, tables):
        cos_g, sin_g = tables
        for pair in range(n_heads // 2):
            ps = slice(2 * pair * HEAD_DIM, 2 * (pair + 1) * HEAD_DIM)
            x2 = src_ref[:, ps]
            swapped = jnp.dot(x2, perm, preferred_element_type=F32)
            xf = x2.astype(F32)
            for half in range(2):
                hs = slice(half * HEAD_DIM, (half + 1) * HEAD_DIM)
                x = xf[:, hs]
                inv = lax.rsqrt(jnp.mean(x * x, axis=-1, keepdims=True) + RMS_EPS)
                out = (x * cos_g + swapped[:, hs] * sin_g) * inv
                dst_ref[:, (2 * pair + half) * HEAD_DIM:(2 * pair + half + 1) * HEAD_DIM] = out.astype(dst_ref.dtype)

    norm_rope(q_ref, qo_ref, GQA_Q_HEADS, gain_tables(gq_ref, Q_PRESCALE))
    norm_rope(k_ref, ko_ref, GQA_KV_HEADS, gain_tables(gk_ref, 1.0))
    for h in range(GQA_KV_HEADS):
        hs = slice(h * HEAD_DIM, (h + 1) * HEAD_DIM)
        vt_ref[h, 0] = v_ref[:, hs].astype(F32).T.astype(vt_ref.dtype)


def _qk_prep(proj, cos, sin, g_q, g_k, n_seq, seq_len):
    rows = proj.shape[0]
    tm = FLASH_TK
    tpos = seq_len // tm
    return pl.pallas_call(
        _prep_kernel,
        grid=(rows // tm,),
        in_specs=[
            pl.BlockSpec((tm, GQA_Q_WIDTH), lambda i: (i, COL_BQ // GQA_Q_WIDTH)),
            pl.BlockSpec((tm, GQA_KV_WIDTH), lambda i: (i, COL_BK // GQA_KV_WIDTH)),
            pl.BlockSpec((tm, GQA_KV_WIDTH), lambda i: (i, COL_BV // GQA_KV_WIDTH)),
            pl.BlockSpec((tm, HEAD_DIM), lambda i: (i % tpos, 0)),
            pl.BlockSpec((tm, HEAD_DIM), lambda i: (i % tpos, 0)),
            pl.BlockSpec((1, HEAD_DIM), lambda i: (0, 0)),
            pl.BlockSpec((1, HEAD_DIM), lambda i: (0, 0)),
            pl.BlockSpec((2 * HEAD_DIM, 2 * HEAD_DIM), lambda i: (0, 0)),
        ],
        out_specs=[pl.BlockSpec((tm, GQA_Q_WIDTH), lambda i: (i, 0)),
                   pl.BlockSpec((tm, GQA_KV_WIDTH), lambda i: (i, 0)),
                   pl.BlockSpec((None, GQA_KV_HEADS, 1, HEAD_DIM, tm), lambda i: (i // tpos, 0, i % tpos, 0, 0))],
        out_shape=[jax.ShapeDtypeStruct((rows, GQA_Q_WIDTH), BF16),
                   jax.ShapeDtypeStruct((rows, GQA_KV_WIDTH), BF16),
                   jax.ShapeDtypeStruct((n_seq, GQA_KV_HEADS, tpos, HEAD_DIM, tm), BF16)],
        compiler_params=pltpu.CompilerParams(
            dimension_semantics=("parallel",), vmem_limit_bytes=_vmem_limit(16 * MIB)),
        name="qk_prep",
    )(proj, proj, proj, cos, sin, g_q, g_k, _swap_halves_matrix())


def _flash_kernel(q_ref, k_ref, vt_ref, o_ref, st_sc, mx_sc, m_sc, acc_sc, *, n_chunks):
    tk = FLASH_TK
    m_sc[...] = jnp.full(m_sc.shape, NEG_INF, F32)
    acc_sc[...] = jnp.zeros(acc_sc.shape, F32)
    ones = jnp.ones((FLASH_ONES_ROWS, tk), BF16)

    def scores(c, slot, g):
        k = k_ref[pl.ds(pl.multiple_of(c * tk, tk), tk), :]
        q = q_ref[:, g * HEAD_DIM:(g + 1) * HEAD_DIM]
        st = lax.dot_general(k, q, (((1,), (1,)), ((), ())), preferred_element_type=F32)
        st_sc[slot, g] = st
        mx_sc[slot, g] = jnp.max(st, axis=0, keepdims=True)

    def consume(c, slot, g):
        vt = jnp.concatenate([vt_ref[c], ones], axis=0)
        m_prev = m_sc[g]
        m_new = jnp.maximum(m_prev, mx_sc[slot, g])
        alpha = jnp.exp2(m_prev - m_new)
        p = jnp.exp2(st_sc[slot, g] - m_new).astype(BF16)
        acc_sc[g] = alpha * acc_sc[g] + jnp.dot(vt, p, preferred_element_type=F32)
        m_sc[g] = m_new

    def step(c, slot, has_next):
        for g in range(GQA_GROUP):
            if has_next:
                scores(c + 1, 1 - slot, g)
            consume(c, slot, g)

    for g in range(GQA_GROUP):
        scores(0, 0, g)

    unroll = math.gcd(FLASH_UNROLL, n_chunks)

    def group(i, carry):
        for u in range(unroll):
            step(unroll * i + u, u % 2, True)
        return carry

    lax.fori_loop(0, n_chunks // unroll - 1, group, 0)
    for u in range(unroll):
        step(n_chunks - unroll + u, u % 2, u < unroll - 1)
    for g in range(GQA_GROUP):
        acc = acc_sc[g]
        o_t = acc[:HEAD_DIM] / acc[HEAD_DIM:HEAD_DIM + 1]
        o_ref[:, g * HEAD_DIM:(g + 1) * HEAD_DIM] = o_t.T.astype(o_ref.dtype)


def _flash(q, k, vt, n_seq, seq_len):
    rows = q.shape[0]
    tq, tk = FLASH_TQ, FLASH_TK
    nq = seq_len // tq
    n_chunks = seq_len // tk
    gw = GQA_GROUP * HEAD_DIM
    acc_rows = HEAD_DIM + FLASH_ONES_ROWS
    assert math.gcd(FLASH_UNROLL, n_chunks) % 2 == 0
    est = (2 * 2 * seq_len * HEAD_DIM * 2 + 4 * tq * gw * 2 + GQA_GROUP * (acc_rows + 8 * 3) * tq * 4
           + (2 * GQA_GROUP + 3) * tk * tq * 4)
    return pl.pallas_call(
        functools.partial(_flash_kernel, n_chunks=n_chunks),
        grid=(n_seq, GQA_KV_HEADS, nq),
        in_specs=[
            pl.BlockSpec((tq, gw), lambda b, h, i: (b * nq + i, h)),
            pl.BlockSpec((seq_len, HEAD_DIM), lambda b, h, i: (b, h)),
            pl.BlockSpec((None, None, n_chunks, HEAD_DIM, tk), lambda b, h, i: (b, h, 0, 0, 0)),
        ],
        out_specs=pl.BlockSpec((tq, gw), lambda b, h, i: (b * nq + i, h)),
        out_shape=jax.ShapeDtypeStruct((rows, GQA_Q_WIDTH), BF16),
        scratch_shapes=[pltpu.VMEM((2, GQA_GROUP, tk, tq), F32),
                        pltpu.VMEM((2, GQA_GROUP, 1, tq), F32),
                        pltpu.VMEM((GQA_GROUP, 1, tq), F32),
                        pltpu.VMEM((GQA_GROUP, acc_rows, tq), F32)],
        compiler_params=pltpu.CompilerParams(
            dimension_semantics=("parallel", "parallel", "arbitrary"), vmem_limit_bytes=_vmem_limit(est)),
        name="gqa_flash",
    )(q, k, vt)


def _sigmoid(x):
    return 1.0 / (1.0 + jnp.exp(-x))


def _merge_kernel(ya_ref, yb_ref, ga0_ref, ga1_ref, gb0_ref, gb1_ref, x_ref, wa_ref, wb_ref, wo_ref, g_ref,
                  x1_ref, h2_ref):
    half = D_MODEL // 2
    ya = ya_ref[...]
    yb = yb_ref[...]
    x1 = x_ref[...]
    for n, (ga_ref, gb_ref) in enumerate(((ga0_ref, gb0_ref), (ga1_ref, gb1_ref))):
        cs = slice(n * half, (n + 1) * half)
        o_a = jnp.dot(ya, wa_ref[:, cs], preferred_element_type=F32)
        o_b = jnp.dot(yb, wb_ref[:, cs], preferred_element_type=F32)
        merged = _sigmoid(ga_ref[...].astype(F32)) * o_a + _sigmoid(gb_ref[...].astype(F32)) * o_b
        x1 = x1 + jnp.dot(merged.astype(BF16), wo_ref[cs, :], preferred_element_type=F32)
    x1_ref[...] = x1
    h2_ref[...] = _rms(x1, g_ref[...]).astype(h2_ref.dtype)


def _merge(y_a, y_b, proj, x, w_a, w_b, w_out, g_mlp):
    rows = x.shape[0]
    tm = MERGE_TM
    half = D_MODEL // 2
    once = pl.Buffered(1)
    est = (2 * D_MODEL * D_MODEL * 2 + 2 * tm * (DIL_WIDTH + GQA_Q_WIDTH) * 2 + 2 * 4 * tm * half * 2
           + 4 * tm * D_MODEL * 4 + 2 * tm * D_MODEL * 2 + 6 * tm * D_MODEL * 4)
    return pl.pallas_call(
        _merge_kernel,
        grid=(rows // tm,),
        in_specs=[
            pl.BlockSpec((tm, DIL_WIDTH), lambda i: (i, 0)),
            pl.BlockSpec((tm, GQA_Q_WIDTH), lambda i: (i, 0)),
            pl.BlockSpec((tm, half), lambda i: (i, COL_GA // half)),
            pl.BlockSpec((tm, half), lambda i: (i, COL_GA // half + 1)),
            pl.BlockSpec((tm, half), lambda i: (i, COL_GB // half)),
            pl.BlockSpec((tm, half), lambda i: (i, COL_GB // half + 1)),
            pl.BlockSpec((tm, D_MODEL), lambda i: (i, 0)),
            pl.BlockSpec((DIL_WIDTH, D_MODEL), lambda i: (0, 0), pipeline_mode=once),
            pl.BlockSpec((GQA_Q_WIDTH, D_MODEL), lambda i: (0, 0), pipeline_mode=once),
            pl.BlockSpec((D_MODEL, D_MODEL), lambda i: (0, 0), pipeline_mode=once),
            pl.BlockSpec((1, D_MODEL), lambda i: (0, 0)),
        ],
        out_specs=[pl.BlockSpec((tm, D_MODEL), lambda i: (i, 0)),
                   pl.BlockSpec((tm, D_MODEL), lambda i: (i, 0))],
        out_shape=[jax.ShapeDtypeStruct((rows, D_MODEL), F32),
                   jax.ShapeDtypeStruct((rows, D_MODEL), BF16)],
        compiler_params=pltpu.CompilerParams(
            dimension_semantics=("parallel",), vmem_limit_bytes=_vmem_limit(est)),
        name="merge_out_proj",
    )(y_a, y_b, proj, proj, proj, proj, x, w_a, w_b, w_out, g_mlp)


def _mlp_kernel(h2_ref, x1_ref, w1_ref, w2_ref, g_ref, o_ref):
    f = pl.program_id(1)

    @pl.when(f == 0)
    def _():
        o_ref[...] = x1_ref[...]

    u = jnp.maximum(jnp.dot(h2_ref[...], w1_ref[...], preferred_element_type=F32), 0.0)
    o_ref[...] += jnp.dot((u * u).astype(BF16), w2_ref[...], preferred_element_type=F32)

    @pl.when(f == pl.num_programs(1) - 1)
    def _():
        o_ref[...] = _rms(o_ref[...], g_ref[...])


def _mlp(h2, x1, w_ff1, w_ff2, g_final):
    rows = h2.shape[0]
    tm, tf = MLP_TM, MLP_TF
    est = 2 * tm * D_MODEL * 2 + 3 * tm * D_MODEL * 4 + 4 * D_MODEL * tf * 2
    return pl.pallas_call(
        _mlp_kernel,
        grid=(rows // tm, D_FF // tf),
        in_specs=[
            pl.BlockSpec((tm, D_MODEL), lambda i, f: (i, 0)),
            pl.BlockSpec((tm, D_MODEL), lambda i, f: (i, 0), pipeline_mode=pl.Buffered(1)),
            pl.BlockSpec((D_MODEL, tf), lambda i, f: (0, f)),
            pl.BlockSpec((tf, D_MODEL), lambda i, f: (f, 0)),
            pl.BlockSpec((1, D_MODEL), lambda i, f: (0, 0)),
        ],
        out_specs=pl.BlockSpec((tm, D_MODEL), lambda i, f: (i, 0)),
        out_shape=jax.ShapeDtypeStruct((rows, D_MODEL), F32),
        compiler_params=pltpu.CompilerParams(
            dimension_semantics=("parallel", "arbitrary"), vmem_limit_bytes=_vmem_limit(est)),
        name="mlp_final_norm",
    )(h2, x1, w_ff1, w_ff2, g_final)


def _trunk(x3, weights, g_final, rope):
    n_seq, seq_len, _ = x3.shape
    x = x3.reshape(n_seq * seq_len, D_MODEL)
    cos, sin = rope
    g_mix, w_main, w_qkv4, w_qkv16, g_q, g_k, w_a, w_b, w_out, g_mlp, w_ff1, w_ff2 = weights
    proj, h = _in_proj(x, g_mix, w_main)
    qkv4 = _in_proj_dil(h, w_qkv4, DIL_GROUPS[1][1], n_seq, seq_len)
    qkv16 = _in_proj_dil(h, w_qkv16, DIL_GROUPS[2][1], n_seq, seq_len)
    y_a = _dilated_mixer(proj, qkv4, qkv16, n_seq, seq_len)
    q, k, vt = _qk_prep(proj, cos, sin, g_q, g_k, n_seq, seq_len)
    y_b = _flash(q, k, vt, n_seq, seq_len)
    x1, h2 = _merge(y_a, y_b, proj, x, w_a, w_b, w_out, g_mlp)
    y = _mlp(h2, x1, w_ff1, w_ff2, g_final)
    return y.reshape(n_seq, seq_len, D_MODEL)


def _group_qkv_columns(w, group):
    q, k, v = (w[:, base + group * DIL_WIDTH: base + (group + 1) * DIL_WIDTH] for base in (W_AQ, W_AK, W_AV))
    return [q * ATTN_SCALE, k, v]


def kernel(x_prompt, x_sample, g_mix, w_in, g_q, g_k, w_branch, w_out, g_mlp, w_ff1, w_ff2, g_final):
    assert w_in.shape[0] == 1, "single-layer stack only"
    w = w_in[0]
    w_main = jnp.concatenate(_group_qkv_columns(w, 0) + [w[:, W_REST:]], axis=1).astype(BF16)
    w_qkv4 = jnp.concatenate(_group_qkv_columns(w, 1), axis=1).astype(BF16)
    w_qkv16 = jnp.concatenate(_group_qkv_columns(w, 2), axis=1).astype(BF16)
    wb = w_branch[0].astype(BF16)
    weights = (g_mix[0][None], w_main, w_qkv4, w_qkv16,
               g_q[0][None], g_k[0][None], wb[:DIL_WIDTH], wb[DIL_WIDTH:], w_out[0].astype(BF16),
               g_mlp[0][None], w_ff1[0].astype(BF16), w_ff2[0].astype(BF16))
    gf = g_final[None]
    ropes = {}
    outs = []
    for x3 in (x_prompt, x_sample):
        seq_len = x3.shape[1]
        if seq_len not in ropes:
            ropes[seq_len] = _rope_tables(seq_len)
        outs.append(_trunk(x3, weights, gf, ropes[seq_len]))
    return tuple(outs)
```

```python
import functools
import math

import jax
import jax.numpy as jnp
from jax import lax
from jax.experimental import pallas as pl
from jax.experimental.pallas import tpu as pltpu

F32 = jnp.float32
BF16 = jnp.bfloat16

D_MODEL = 2048
HEAD_DIM = 128
DIL_GROUPS = ((128, 1), (512, 4), (2048, 16))
N_DIL_GROUPS = 3
DIL_HEADS = 4
DIL_WIDTH = DIL_HEADS * HEAD_DIM
DIL_QKV_WIDTH = 3 * DIL_WIDTH
GQA_Q_HEADS = 12
GQA_KV_HEADS = 4
GQA_GROUP = GQA_Q_HEADS // GQA_KV_HEADS
GQA_Q_WIDTH = GQA_Q_HEADS * HEAD_DIM
GQA_KV_WIDTH = GQA_KV_HEADS * HEAD_DIM
GRID_W = 64
ROPE_THETA = 10000.0
D_FF = 4 * D_MODEL
W_AQ = 0
W_AK = N_DIL_GROUPS * DIL_WIDTH
W_AV = 2 * N_DIL_GROUPS * DIL_WIDTH
W_REST = 3 * N_DIL_GROUPS * DIL_WIDTH
COL_BQ = DIL_QKV_WIDTH
COL_BK = COL_BQ + GQA_Q_WIDTH
COL_BV = COL_BK + GQA_KV_WIDTH
COL_GA = COL_BV + GQA_KV_WIDTH
COL_GB = COL_GA + D_MODEL
MAIN_WIDTH = COL_GB + D_MODEL
RMS_EPS = 1e-6
NEG_INF = -1e30
ATTN_SCALE = HEAD_DIM ** -0.5
Q_PRESCALE = ATTN_SCALE * 1.4426950408889634

MIB = 1024 * 1024
V7X_VMEM_BYTES = 64 * MIB
V7X_LANES = 128
V7X_SUBLANES = 8
VMEM_TEMP_BYTES = 16 * MIB
VMEM_RESERVE_BYTES = 8 * MIB
ML_WIDTH = V7X_LANES

IN_TM, IN_TN = 1024, 2048
IN_DIL_TN = DIL_QKV_WIDTH
DIL_TQ = 128
DIL_HALO = 64
DIL_UNITS = 4
DIL_LOOKAHEAD = 12
FLASH_TQ, FLASH_TK = 512, 512
FLASH_ONES_ROWS = 16
FLASH_UNROLL = 8
MERGE_TM = 256
MLP_TM, MLP_TF = 1024, 1024


def _vmem_limit(nbytes):
    return int(min(nbytes + VMEM_TEMP_BYTES, V7X_VMEM_BYTES - VMEM_RESERVE_BYTES))


def _rms(x, g):
    return x * lax.rsqrt(jnp.mean(x * x, axis=-1, keepdims=True) + RMS_EPS) * g


def _in_proj_kernel(x_ref, g_ref, w_ref, o_ref, h_ref):
    @pl.when(pl.program_id(1) == 0)
    def _():
        h_ref[...] = _rms(x_ref[...], g_ref[...]).astype(BF16)

    o_ref[...] = jnp.dot(h_ref[...], w_ref[...], preferred_element_type=F32).astype(o_ref.dtype)


def _in_proj(x, g_mix, w_main):
    rows = x.shape[0]
    tm, tn = IN_TM, IN_TN
    est = 2 * tm * D_MODEL * 4 + 2 * tm * D_MODEL * 2 + 2 * D_MODEL * tn * 2 + 2 * tm * tn * 2
    return pl.pallas_call(
        _in_proj_kernel,
        grid=(rows // tm, MAIN_WIDTH // tn),
        in_specs=[
            pl.BlockSpec((tm, D_MODEL), lambda i, j: (i, 0)),
            pl.BlockSpec((1, D_MODEL), lambda i, j: (0, 0)),
            pl.BlockSpec((D_MODEL, tn), lambda i, j: (0, j)),
        ],
        out_specs=[pl.BlockSpec((tm, tn), lambda i, j: (i, j)),
                   pl.BlockSpec((tm, D_MODEL), lambda i, j: (i, 0))],
        out_shape=[jax.ShapeDtypeStruct((rows, MAIN_WIDTH), BF16),
                   jax.ShapeDtypeStruct((rows, D_MODEL), BF16)],
        compiler_params=pltpu.CompilerParams(
            dimension_semantics=("parallel", "arbitrary"), vmem_limit_bytes=_vmem_limit(est)),
        name="in_proj",
    )(x, g_mix, w_main)


def _in_proj_dil_kernel(h_ref, w_ref, o_ref, res_ref, *, dilation):
    res = jnp.dot(h_ref[...], w_ref[...], preferred_element_type=F32)
    strips, tm, lanes = res_ref.shape
    for s in range(strips):
        res_ref[s] = res[:, s * lanes:(s + 1) * lanes]
    sub = tm // dilation
    for r in range(dilation):
        for s in range(strips):
            o_ref[r, :, s * lanes:(s + 1) * lanes] = (
                res_ref[s, pl.ds(r, sub, stride=dilation), :].astype(o_ref.dtype))


def _in_proj_dil(h, w_qkv, dilation, n_seq, seq_len):
    rows = h.shape[0]
    tm, tn = IN_TM, IN_DIL_TN
    tiles_per_seq = seq_len // tm
    est = 2 * tm * D_MODEL * 2 + 2 * D_MODEL * tn * 2 + 2 * tm * tn * 2 + tm * tn * 4
    return pl.pallas_call(
        functools.partial(_in_proj_dil_kernel, dilation=dilation),
        grid=(rows // tm, DIL_QKV_WIDTH // tn),
        in_specs=[
            pl.BlockSpec((tm, D_MODEL), lambda i, j: (i, 0)),
            pl.BlockSpec((D_MODEL, tn), lambda i, j: (0, j)),
        ],
        out_specs=pl.BlockSpec((None, dilation, tm // dilation, tn),
                               lambda i, j: (i // tiles_per_seq, 0, i % tiles_per_seq, j)),
        out_shape=jax.ShapeDtypeStruct((n_seq, dilation, seq_len // dilation, DIL_QKV_WIDTH), BF16),
        scratch_shapes=[pltpu.VMEM((tn // V7X_LANES, tm, V7X_LANES), F32)],
        compiler_params=pltpu.CompilerParams(
            dimension_semantics=("parallel", "parallel"), vmem_limit_bytes=_vmem_limit(est)),
        name=f"in_proj_dil{dilation}",
    )(h, w_qkv)


def _dilated_kernel(*refs, dilation, sub_len, slopes, has_state, emit_final, state_in_by_unit, state_out_by_unit):
    q_ref, kp_ref, kc_ref, kn_ref, vp_ref, vc_ref, vn_ref = refs[:7]
    refs = refs[7:]
    if has_state:
        acc_in, ml_in = refs[:2]
        refs = refs[2:]
    if emit_final:
        y_ref = refs[0]
        refs = refs[1:]
    else:
        acc_out, ml_out = refs[:2]
        refs = refs[2:]
    strided = dilation > 1
    if strided:
        acc_t, ml_t = refs

    tq, tk = DIL_TQ, DIL_TQ + 2 * DIL_HALO
    units = range(DIL_UNITS)
    heads = range(DIL_HEADS)
    cols = [slice(h * HEAD_DIM, (h + 1) * HEAD_DIM) for h in heads]
    step = pl.program_id(1)
    n_rgroups = dilation // DIL_UNITS if strided else 1

    def for_residue_group(fn):
        if n_rgroups == 1:
            fn(0)
        else:
            for g in range(n_rgroups):
                pl.when(pl.program_id(2) == g)(functools.partial(fn, g * DIL_UNITS))

    def state_rows(r0, u):
        return pl.ds(r0 + u, tq, stride=dilation)

    if strided and has_state and not state_in_by_unit:
        def gather(r0):
            for u in units:
                for h in heads:
                    acc_t[u, h] = acc_in[h, state_rows(r0, u), :]
                ml_t[u] = ml_in[state_rows(r0, u), :]
        for_residue_group(gather)

    def unit_rows(u):
        return slice(u * tq, (u + 1) * tq)

    def prev_acc(u, h):
        if state_in_by_unit:
            return acc_in[h, u]
        return acc_t[u, h] if strided else acc_in[h, unit_rows(u), :]

    def prev_ml(u):
        if state_in_by_unit:
            return ml_in[u]
        return ml_t[u] if strided else ml_in[unit_rows(u), :]

    if strided:
        def q_of(u, h):
            return q_ref[u, :, cols[h]]

        def window(p_ref, c_ref, n_ref, u, h):
            return jnp.concatenate([p_ref[u, :, cols[h]], c_ref[u, :, cols[h]], n_ref[u, :, cols[h]]], axis=0)
    else:
        def q_of(u, h):
            return q_ref[unit_rows(u), cols[h]]

        def window(p_ref, c_ref, n_ref, u, h):
            lo, hi = u * tq - DIL_HALO, u * tq + tq + DIL_HALO
            parts = []
            if lo < 0:
                parts.append(p_ref[:, cols[h]])
            parts.append(c_ref[max(lo, 0):min(hi, DIL_UNITS * tq), cols[h]])
            if hi > DIL_UNITS * tq:
                parts.append(n_ref[:, cols[h]])
            return jnp.concatenate(parts, axis=0) if len(parts) > 1 else parts[0]

    row = lax.broadcasted_iota(jnp.int32, (tq, tk), 0)
    col = lax.broadcasted_iota(jnp.int32, (tq, tk), 1)
    rel = col - DIL_HALO - row
    band = jnp.abs(rel) <= DIL_HALO
    dist = (dilation * jnp.abs(rel)).astype(F32)
    alibi = [slopes[h] * dist for h in heads]

    def mask_of(tile):
        key_idx = tile * tq - DIL_HALO + col
        return band & (key_idx >= 0) & (key_idx < sub_len)

    if strided:
        shared_mask = mask_of(step)
        masks = [shared_mask for _ in units]
    else:
        masks = [mask_of(step * DIL_UNITS + u) for u in units]
    lane = lax.broadcasted_iota(jnp.int32, (tq, ML_WIDTH), 1)
    ones = jnp.ones((tk, HEAD_DIM), BF16)

    scores, m_news, alphas, probs = {}, {}, {}, {}
    ml_prevs = [prev_ml(u) for u in units] if has_state else None
    ml_news = [jnp.zeros((tq, ML_WIDTH), F32) for _ in units]

    def score_pass(u, h):
        s = lax.dot_general(q_of(u, h), window(kp_ref, kc_ref, kn_ref, u, h), (((1,), (1,)), ((), ())),
                            preferred_element_type=F32)
        scores[u, h] = jnp.where(masks[u], s - alibi[h], NEG_INF)

    def max_pass(u, h):
        m_cur = jnp.max(scores[u, h], axis=-1, keepdims=True)
        if has_state:
            m_prev = ml_prevs[u][:, h:h + 1]
            m_news[u, h] = jnp.maximum(m_prev, m_cur)
            alphas[u, h] = jnp.exp(m_prev - m_news[u, h])
        else:
            m_news[u, h] = m_cur

    def exp_pass(u, h):
        probs[u, h] = jnp.exp(scores[u, h] - m_news[u, h]).astype(BF16)

    def value_pass(u, h):
        v_aug = jnp.concatenate([window(vp_ref, vc_ref, vn_ref, u, h), ones], axis=1)
        both = jnp.dot(probs[u, h], v_aug, preferred_element_type=F32)
        acc, l_new = both[:, :HEAD_DIM], both[:, HEAD_DIM:]
        if has_state:
            l_new = alphas[u, h] * ml_prevs[u][:, DIL_HEADS + h:DIL_HEADS + h + 1] + l_new
            acc = alphas[u, h] * prev_acc(u, h) + acc
        if emit_final:
            y_ref[unit_rows(u), cols[h]] = (acc / l_new).astype(y_ref.dtype)
        elif strided:
            acc_t[u, h] = acc
        else:
            acc_out[h, unit_rows(u), :] = acc
        ml_news[u] = jnp.where(lane == h, m_news[u, h], ml_news[u])
        ml_news[u] = jnp.where(lane == DIL_HEADS + h, l_new, ml_news[u])

    pairs = [(u, h) for u in units for h in heads]
    lookahead = DIL_LOOKAHEAD
    for pair in pairs[:lookahead]:
        score_pass(*pair)
    for n, pair in enumerate(pairs):
        if n + lookahead < len(pairs):
            score_pass(*pairs[n + lookahead])
        max_pass(*pair)
        exp_pass(*pair)
        value_pass(*pair)
    if not emit_final:
        if strided:
            for u in units:
                ml_t[u] = ml_news[u]

            def scatter(r0):
                for u in units:
                    if state_out_by_unit:
                        rows = pl.ds(r0 // DIL_UNITS, tq, stride=dilation // DIL_UNITS)
                        for h in heads:
                            acc_out[h, u, rows, :] = acc_t[u, h]
                        ml_out[u, rows, :] = ml_t[u]
                    else:
                        for h in heads:
                            acc_out[h, state_rows(r0, u), :] = acc_t[u, h]
                        ml_out[state_rows(r0, u), :] = ml_t[u]
            for_residue_group(scatter)
        else:
            for u in units:
                ml_out[unit_rows(u), :] = ml_news[u]


def _dilated_call(qkv, col0, group, state, emit_final, n_seq, seq_len,
                  state_in_by_unit=False, state_out_by_unit=False):
    window, dilation = DIL_GROUPS[group]
    assert window == 2 * DIL_HALO * dilation
    assert not (emit_final and dilation > 1), "the bf16 output is written in token order only"
    sub_len = seq_len // dilation
    nt = sub_len // DIL_TQ
    rows = n_seq * seq_len
    n_heads = N_DIL_GROUPS * DIL_HEADS
    slopes = tuple(float(2.0 ** (-8.0 * (group * DIL_HEADS + h + 1) / n_heads)) for h in range(DIL_HEADS))

    halo_blocks = 2 * nt
    if dilation == 1:
        assert nt % DIL_UNITS == 0
        n_steps, n_rgroups = nt // DIL_UNITS, 1
        per_step = 2 * DIL_UNITS
        cur_block, halo_block = (None, None, DIL_UNITS * DIL_TQ, DIL_WIDTH), (None, None, DIL_HALO, DIL_WIDTH)
    else:
        assert dilation % DIL_UNITS == 0
        n_steps, n_rgroups = nt, dilation // DIL_UNITS
        per_step = 2
        cur_block, halo_block = (None, DIL_UNITS, DIL_TQ, DIL_WIDTH), (None, DIL_UNITS, DIL_HALO, DIL_WIDTH)

    def cur_spec(kind):
        return pl.BlockSpec(cur_block, lambda b, i, r: (b, r, i, col0 + kind))

    def prev_spec(kind):
        return pl.BlockSpec(halo_block, lambda b, i, r: (b, r, jnp.maximum(per_step * i - 1, 0), col0 + kind))

    def next_spec(kind):
        return pl.BlockSpec(halo_block,
                            lambda b, i, r: (b, r, jnp.minimum(per_step * (i + 1), halo_blocks - 1), col0 + kind))

    span = seq_len // n_steps
    acc_spec = pl.BlockSpec((DIL_HEADS, span, HEAD_DIM), lambda b, i, r: (0, b * n_steps + i, 0))
    ml_spec = pl.BlockSpec((span, ML_WIDTH), lambda b, i, r: (b * n_steps + i, 0))
    y_spec = pl.BlockSpec((span, DIL_WIDTH), lambda b, i, r: (b * n_steps + i, 0))

    unit_len = seq_len // DIL_UNITS
    by_unit_rows = unit_len // n_steps
    acc_unit_spec = pl.BlockSpec((DIL_HEADS, None, DIL_UNITS, by_unit_rows, HEAD_DIM),
                                 lambda b, i, r: (0, b, 0, i, 0))
    ml_unit_spec = pl.BlockSpec((None, DIL_UNITS, by_unit_rows, ML_WIDTH), lambda b, i, r: (b, 0, i, 0))
    if state_in_by_unit:
        assert dilation == DIL_UNITS and by_unit_rows == DIL_TQ
    if state_out_by_unit:
        assert dilation > DIL_UNITS and dilation % DIL_UNITS == 0 and not emit_final

    in_specs = [cur_spec(0), prev_spec(1), cur_spec(1), next_spec(1), prev_spec(2), cur_spec(2), next_spec(2)]
    args = [qkv] * 7
    has_state = state is not None
    if has_state:
        in_specs += [acc_unit_spec, ml_unit_spec] if state_in_by_unit else [acc_spec, ml_spec]
        args += list(state)
    if emit_final:
        out_specs = y_spec
        out_shape = jax.ShapeDtypeStruct((rows, DIL_WIDTH), BF16)
    elif state_out_by_unit:
        out_specs = [acc_unit_spec, ml_unit_spec]
        out_shape = [jax.ShapeDtypeStruct((DIL_HEADS, n_seq, DIL_UNITS, unit_len, HEAD_DIM), F32),
                     jax.ShapeDtypeStruct((n_seq, DIL_UNITS, unit_len, ML_WIDTH), F32)]
    else:
        out_specs = [acc_spec, ml_spec]
        out_shape = [jax.ShapeDtypeStruct((DIL_HEADS, rows, HEAD_DIM), F32),
                     jax.ShapeDtypeStruct((rows, ML_WIDTH), F32)]
    scratch = []
    if dilation > 1:
        scratch = [pltpu.VMEM((DIL_UNITS, DIL_HEADS, DIL_TQ, HEAD_DIM), F32),
                   pltpu.VMEM((DIL_UNITS, DIL_TQ, ML_WIDTH), F32)]
    kern = functools.partial(_dilated_kernel, dilation=dilation, sub_len=sub_len, slopes=slopes,
                             has_state=has_state, emit_final=emit_final,
                             state_in_by_unit=state_in_by_unit, state_out_by_unit=state_out_by_unit)
    est = (4 * span * (DIL_WIDTH + ML_WIDTH) * 4 + 2 * 7 * DIL_UNITS * DIL_TQ * DIL_WIDTH * 2
           + 2 * DIL_UNITS * DIL_TQ * (DIL_WIDTH + ML_WIDTH) * 4)
    return pl.pallas_call(
        kern,
        grid=(n_seq, n_steps, n_rgroups),
        in_specs=in_specs,
        out_specs=out_specs,
        out_shape=out_shape,
        scratch_shapes=scratch,
        compiler_params=pltpu.CompilerParams(
            dimension_semantics=("parallel", "parallel", "arbitrary"), vmem_limit_bytes=_vmem_limit(est)),
        name=f"dilated_g{group}",
    )(*args)


def _dilated_mixer(proj, qkv4, qkv16, n_seq, seq_len):
    state = _dilated_call(qkv16, 0, 2, None, False, n_seq, seq_len, state_out_by_unit=True)
    state = _dilated_call(qkv4, 0, 1, state, False, n_seq, seq_len, state_in_by_unit=True)
    qkv1 = proj.reshape(n_seq, 1, seq_len, MAIN_WIDTH)
    return _dilated_call(qkv1, 0, 0, state, True, n_seq, seq_len)


def _rope_tables(seq_len):
    pos = jnp.arange(seq_len)
    row = (pos // GRID_W).astype(F32)
    col = (pos % GRID_W).astype(F32)
    half = HEAD_DIM // 2
    inv_freq = ROPE_THETA ** (-jnp.arange(0, half, 2, dtype=F32) / half)
    ang_r = row[:, None] * inv_freq
    ang_c = col[:, None] * inv_freq
    cos = jnp.concatenate([jnp.cos(ang_r), jnp.cos(ang_r), jnp.cos(ang_c), jnp.cos(ang_c)], axis=1)
    sin = jnp.concatenate([-jnp.sin(ang_r), jnp.sin(ang_r), -jnp.sin(ang_c), jnp.sin(ang_c)], axis=1)
    return cos, sin


def _swap_halves_matrix():
    quarter = HEAD_DIM // 4
    i = jnp.arange(2 * HEAD_DIM)
    partner = jnp.where((i & quarter) != 0, i - quarter, i + quarter)
    return (i[:, None] == partner[None, :]).astype(BF16)


def _prep_kernel(q_ref, k_ref, v_ref, cos_ref, sin_ref, gq_ref, gk_ref, perm_ref, qo_ref, ko_ref, vt_ref):
    cos = cos_ref[...]
    sin = sin_ref[...]
    perm = perm_ref[...]
    quarter = HEAD_DIM // 4
    lane = lax.broadcasted_iota(jnp.int32, (V7X_SUBLANES, HEAD_DIM), 1)
    upper = (lane & quarter) != 0

    def gain_tables(g_ref, scale):
        g = jnp.broadcast_to(g_ref[...], (V7X_SUBLANES, HEAD_DIM))
        g_partner = jnp.where(upper, pltpu.roll(g, quarter, 1), pltpu.roll(g, HEAD_DIM - quarter, 1))
        return cos * (g[:1] * scale), sin * (g_partner[:1] * scale)

    def norm_rope(src_ref, dst_ref, n_heads, tables):
        cos_g, sin_g = tables
        for pair in range(n_heads // 2):
            ps = slice(2 * pair * HEAD_DIM, 2 * (pair + 1) * HEAD_DIM)
            x2 = src_ref[:, ps]
            swapped = jnp.dot(x2, perm, preferred_element_type=F32)
            xf = x2.astype(F32)
            for half in range(2):
                hs = slice(half * HEAD_DIM, (half + 1) * HEAD_DIM)
                x = xf[:, hs]
                inv = lax.rsqrt(jnp.mean(x * x, axis=-1, keepdims=True) + RMS_EPS)
                out = (x * cos_g + swapped[:, hs] * sin_g) * inv
                dst_ref[:, (2 * pair + half) * HEAD_DIM:(2 * pair + half + 1) * HEAD_DIM] = out.astype(dst_ref.dtype)

    norm_rope(q_ref, qo_ref, GQA_Q_HEADS, gain_tables(gq_ref, Q_PRESCALE))
    norm_rope(k_ref, ko_ref, GQA_KV_HEADS, gain_tables(gk_ref, 1.0))
    for h in range(GQA_KV_HEADS):
        hs = slice(h * HEAD_DIM, (h + 1) * HEAD_DIM)
        vt_ref[h, 0] = v_ref[:, hs].astype(F32).T.astype(vt_ref.dtype)


def _qk_prep(proj, cos, sin, g_q, g_k, n_seq, seq_len):
    rows = proj.shape[0]
    tm = FLASH_TK
    tpos = seq_len // tm
    return pl.pallas_call(
        _prep_kernel,
        grid=(rows // tm,),
        in_specs=[
            pl.BlockSpec((tm, GQA_Q_WIDTH), lambda i: (i, COL_BQ // GQA_Q_WIDTH)),
            pl.BlockSpec((tm, GQA_KV_WIDTH), lambda i: (i, COL_BK // GQA_KV_WIDTH)),
            pl.BlockSpec((tm, GQA_KV_WIDTH), lambda i: (i, COL_BV // GQA_KV_WIDTH)),
            pl.BlockSpec((tm, HEAD_DIM), lambda i: (i % tpos, 0)),
            pl.BlockSpec((tm, HEAD_DIM), lambda i: (i % tpos, 0)),
            pl.BlockSpec((1, HEAD_DIM), lambda i: (0, 0)),
            pl.BlockSpec((1, HEAD_DIM), lambda i: (0, 0)),
            pl.BlockSpec((2 * HEAD_DIM, 2 * HEAD_DIM), lambda i: (0, 0)),
        ],
        out_specs=[pl.BlockSpec((tm, GQA_Q_WIDTH), lambda i: (i, 0)),
                   pl.BlockSpec((tm, GQA_KV_WIDTH), lambda i: (i, 0)),
                   pl.BlockSpec((None, GQA_KV_HEADS, 1, HEAD_DIM, tm), lambda i: (i // tpos, 0, i % tpos, 0, 0))],
        out_shape=[jax.ShapeDtypeStruct((rows, GQA_Q_WIDTH), BF16),
                   jax.ShapeDtypeStruct((rows, GQA_KV_WIDTH), BF16),
                   jax.ShapeDtypeStruct((n_seq, GQA_KV_HEADS, tpos, HEAD_DIM, tm), BF16)],
        compiler_params=pltpu.CompilerParams(
            dimension_semantics=("parallel",), vmem_limit_bytes=_vmem_limit(16 * MIB)),
        name="qk_prep",
    )(proj, proj, proj, cos, sin, g_q, g_k, _swap_halves_matrix())


def _flash_kernel(q_ref, k_ref, vt_ref, o_ref, st_sc, mx_sc, m_sc, acc_sc, *, n_chunks):
    tk = FLASH_TK
    m_sc[...] = jnp.full(m_sc.shape, NEG_INF, F32)
    acc_sc[...] = jnp.zeros(acc_sc.shape, F32)
    ones = jnp.ones((FLASH_ONES_ROWS, tk), BF16)

    def scores(c, slot, g):
        k = k_ref[pl.ds(pl.multiple_of(c * tk, tk), tk), :]
        q = q_ref[:, g * HEAD_DIM:(g + 1) * HEAD_DIM]
        st = lax.dot_general(k, q, (((1,), (1,)), ((), ())), preferred_element_type=F32)
        st_sc[slot, g] = st
        mx_sc[slot, g] = jnp.max(st, axis=0, keepdims=True)

    def consume(c, slot, g):
        vt = jnp.concatenate([vt_ref[c], ones], axis=0)
        m_prev = m_sc[g]
        m_new = jnp.maximum(m_prev, mx_sc[slot, g])
        alpha = jnp.exp2(m_prev - m_new)
        p = jnp.exp2(st_sc[slot, g] - m_new).astype(BF16)
        acc_sc[g] = alpha * acc_sc[g] + jnp.dot(vt, p, preferred_element_type=F32)
        m_sc[g] = m_new

    def step(c, slot, has_next):
        for g in range(GQA_GROUP):
            if has_next:
                scores(c + 1, 1 - slot, g)
            consume(c, slot, g)

    for g in range(GQA_GROUP):
        scores(0, 0, g)

    unroll = math.gcd(FLASH_UNROLL, n_chunks)

    def group(i, carry):
        for u in range(unroll):
            step(unroll * i + u, u % 2, True)
        return carry

    lax.fori_loop(0, n_chunks // unroll - 1, group, 0)
    for u in range(unroll):
        step(n_chunks - unroll + u, u % 2, u < unroll - 1)
    for g in range(GQA_GROUP):
        acc = acc_sc[g]
        o_t = acc[:HEAD_DIM] / acc[HEAD_DIM:HEAD_DIM + 1]
        o_ref[:, g * HEAD_DIM:(g + 1) * HEAD_DIM] = o_t.T.astype(o_ref.dtype)


def _flash(q, k, vt, n_seq, seq_len):
    rows = q.shape[0]
    tq, tk = FLASH_TQ, FLASH_TK
    nq = seq_len // tq
    n_chunks = seq_len // tk
    gw = GQA_GROUP * HEAD_DIM
    acc_rows = HEAD_DIM + FLASH_ONES_ROWS
    assert math.gcd(FLASH_UNROLL, n_chunks) % 2 == 0
    est = (2 * 2 * seq_len * HEAD_DIM * 2 + 4 * tq * gw * 2 + GQA_GROUP * (acc_rows + 8 * 3) * tq * 4
           + (2 * GQA_GROUP + 3) * tk * tq * 4)
    return pl.pallas_call(
        functools.partial(_flash_kernel, n_chunks=n_chunks),
        grid=(n_seq, GQA_KV_HEADS, nq),
        in_specs=[
            pl.BlockSpec((tq, gw), lambda b, h, i: (b * nq + i, h)),
            pl.BlockSpec((seq_len, HEAD_DIM), lambda b, h, i: (b, h)),
            pl.BlockSpec((None, None, n_chunks, HEAD_DIM, tk), lambda b, h, i: (b, h, 0, 0, 0)),
        ],
        out_specs=pl.BlockSpec((tq, gw), lambda b, h, i: (b * nq + i, h)),
        out_shape=jax.ShapeDtypeStruct((rows, GQA_Q_WIDTH), BF16),
        scratch_shapes=[pltpu.VMEM((2, GQA_GROUP, tk, tq), F32),
                        pltpu.VMEM((2, GQA_GROUP, 1, tq), F32),
                        pltpu.VMEM((GQA_GROUP, 1, tq), F32),
                        pltpu.VMEM((GQA_GROUP, acc_rows, tq), F32)],
        compiler_params=pltpu.CompilerParams(
            dimension_semantics=("parallel", "parallel", "arbitrary"), vmem_limit_bytes=_vmem_limit(est)),
        name="gqa_flash",
    )(q, k, vt)


def _sigmoid(x):
    return 1.0 / (1.0 + jnp.exp(-x))


def _merge_kernel(ya_ref, yb_ref, ga0_ref, ga1_ref, gb0_ref, gb1_ref, x_ref, wa_ref, wb_ref, wo_ref, g_ref,
                  x1_ref, h2_ref):
    half = D_MODEL // 2
    ya = ya_ref[...]
    yb = yb_ref[...]
    for n, (ga_ref, gb_ref) in enumerate(((ga0_ref, gb0_ref), (ga1_ref, gb1_ref))):
        cs = slice(n * half, (n + 1) * half)
        o_a = jnp.dot(ya, wa_ref[:, cs], preferred_element_type=F32)
        o_b = jnp.dot(yb, wb_ref[:, cs], preferred_element_type=F32)
        merged = _sigmoid(ga_ref[...].astype(F32)) * o_a + _sigmoid(gb_ref[...].astype(F32)) * o_b
        update = jnp.dot(merged.astype(BF16), wo_ref[cs, :], preferred_element_type=F32)
        if n == 0:
            x1_ref[...] = x_ref[...] + update
        else:
            x1_ref[...] += update
    h2_ref[...] = _rms(x1_ref[...], g_ref[...]).astype(h2_ref.dtype)


def _merge(y_a, y_b, proj, x, w_a, w_b, w_out, g_mlp):
    rows = x.shape[0]
    tm = MERGE_TM
    half = D_MODEL // 2
    once = pl.Buffered(1)
    est = (2 * D_MODEL * D_MODEL * 2 + 2 * tm * (DIL_WIDTH + GQA_Q_WIDTH) * 2 + 2 * 4 * tm * half * 2
           + 4 * tm * D_MODEL * 4 + 2 * tm * D_MODEL * 2 + 6 * tm * D_MODEL * 4)
    return pl.pallas_call(
        _merge_kernel,
        grid=(rows // tm,),
        in_specs=[
            pl.BlockSpec((tm, DIL_WIDTH), lambda i: (i, 0)),
            pl.BlockSpec((tm, GQA_Q_WIDTH), lambda i: (i, 0)),
            pl.BlockSpec((tm, half), lambda i: (i, COL_GA // half)),
            pl.BlockSpec((tm, half), lambda i: (i, COL_GA // half + 1)),
            pl.BlockSpec((tm, half), lambda i: (i, COL_GB // half)),
            pl.BlockSpec((tm, half), lambda i: (i, COL_GB // half + 1)),
            pl.BlockSpec((tm, D_MODEL), lambda i: (i, 0)),
            pl.BlockSpec((DIL_WIDTH, D_MODEL), lambda i: (0, 0), pipeline_mode=once),
            pl.BlockSpec((GQA_Q_WIDTH, D_MODEL), lambda i: (0, 0), pipeline_mode=once),
            pl.BlockSpec((D_MODEL, D_MODEL), lambda i: (0, 0), pipeline_mode=once),
            pl.BlockSpec((1, D_MODEL), lambda i: (0, 0)),
        ],
        out_specs=[pl.BlockSpec((tm, D_MODEL), lambda i: (i, 0)),
                   pl.BlockSpec((tm, D_MODEL), lambda i: (i, 0))],
        out_shape=[jax.ShapeDtypeStruct((rows, D_MODEL), F32),
                   jax.ShapeDtypeStruct((rows, D_MODEL), BF16)],
        compiler_params=pltpu.CompilerParams(
            dimension_semantics=("parallel",), vmem_limit_bytes=_vmem_limit(est)),
        name="merge_out_proj",
    )(y_a, y_b, proj, proj, proj, proj, x, w_a, w_b, w_out, g_mlp)


def _mlp_kernel(h2_ref, x1_ref, w1_ref, w2_ref, g_ref, o_ref):
    f = pl.program_id(1)

    @pl.when(f == 0)
    def _():
        o_ref[...] = x1_ref[...]

    u = jnp.maximum(jnp.dot(h2_ref[...], w1_ref[...], preferred_element_type=F32), 0.0)
    o_ref[...] += jnp.dot((u * u).astype(BF16), w2_ref[...], preferred_element_type=F32)

    @pl.when(f == pl.num_programs(1) - 1)
    def _():
        o_ref[...] = _rms(o_ref[...], g_ref[...])


def _mlp(h2, x1, w_ff1, w_ff2, g_final):
    rows = h2.shape[0]
    tm, tf = MLP_TM, MLP_TF
    est = 2 * tm * D_MODEL * 2 + 3 * tm * D_MODEL * 4 + 4 * D_MODEL * tf * 2
    return pl.pallas_call(
        _mlp_kernel,
        grid=(rows // tm, D_FF // tf),
        in_specs=[
            pl.BlockSpec((tm, D_MODEL), lambda i, f: (i, 0)),
            pl.BlockSpec((tm, D_MODEL), lambda i, f: (i, 0), pipeline_mode=pl.Buffered(1)),
            pl.BlockSpec((D_MODEL, tf), lambda i, f: (0, f)),
            pl.BlockSpec((tf, D_MODEL), lambda i, f: (f, 0)),
            pl.BlockSpec((1, D_MODEL), lambda i, f: (0, 0)),
        ],
        out_specs=pl.BlockSpec((tm, D_MODEL), lambda i, f: (i, 0)),
        out_shape=jax.ShapeDtypeStruct((rows, D_MODEL), F32),
        compiler_params=pltpu.CompilerParams(
            dimension_semantics=("parallel", "arbitrary"), vmem_limit_bytes=_vmem_limit(est)),
        name="mlp_final_norm",
    )(h2, x1, w_ff1, w_ff2, g_final)


def _trunk(x3, weights, g_final, rope):
    n_seq, seq_len, _ = x3.shape
    x = x3.reshape(n_seq * seq_len, D_MODEL)
    cos, sin = rope
    g_mix, w_main, w_qkv4, w_qkv16, g_q, g_k, w_a, w_b, w_out, g_mlp, w_ff1, w_ff2 = weights
    proj, h = _in_proj(x, g_mix, w_main)
    qkv4 = _in_proj_dil(h, w_qkv4, DIL_GROUPS[1][1], n_seq, seq_len)
    qkv16 = _in_proj_dil(h, w_qkv16, DIL_GROUPS[2][1], n_seq, seq_len)
    y_a = _dilated_mixer(proj, qkv4, qkv16, n_seq, seq_len)
    q, k, vt = _qk_prep(proj, cos, sin, g_q, g_k, n_seq, seq_len)
    y_b = _flash(q, k, vt, n_seq, seq_len)
    x1, h2 = _merge(y_a, y_b, proj, x, w_a, w_b, w_out, g_mlp)
    y = _mlp(h2, x1, w_ff1, w_ff2, g_final)
    return y.reshape(n_seq, seq_len, D_MODEL)


def _group_qkv_columns(w, group):
    q, k, v = (w[:, base + group * DIL_WIDTH: base + (group + 1) * DIL_WIDTH] for base in (W_AQ, W_AK, W_AV))
    return [q * ATTN_SCALE, k, v]


def kernel(x_prompt, x_sample, g_mix, w_in, g_q, g_k, w_branch, w_out, g_mlp, w_ff1, w_ff2, g_final):
    assert w_in.shape[0] == 1, "single-layer stack only"
    w = w_in[0]
    w_main = jnp.concatenate(_group_qkv_columns(w, 0) + [w[:, W_REST:]], axis=1).astype(BF16)
    w_qkv4 = jnp.concatenate(_group_qkv_columns(w, 1), axis=1).astype(BF16)
    w_qkv16 = jnp.concatenate(_group_qkv_columns(w, 2), axis=1).astype(BF16)
    wb = w_branch[0].astype(BF16)
    weights = (g_mix[0][None], w_main, w_qkv4, w_qkv16,
               g_q[0][None], g_k[0][None], wb[:DIL_WIDTH], wb[DIL_WIDTH:], w_out[0].astype(BF16),
               g_mlp[0][None], w_ff1[0].astype(BF16), w_ff2[0].astype(BF16))
    gf = g_final[None]
    ropes = {}
    outs = []
    for x3 in (x_prompt, x_sample):
        seq_len = x3.shape[1]
        if seq_len not in ropes:
            ropes[seq_len] = _rope_tables(seq_len)
        outs.append(_trunk(x3, weights, gf, ropes[seq_len]))
    return tuple(outs)
```
